```python
import jax
import jax.numpy as jnp
from jax import lax
import numpy as np

D_MODEL = 4096
BATCH = 4
SEQ = 4096
DEPTH = 2
DEC_BATCH = 8
DEC_SEQ = 2048
PAST_LEN = 128

A_HEAD = 64
A_WIDTH = D_MODEL // 2
A_HEADS = A_WIDTH // A_HEAD
A_DECAY_LORA = 64
A_AAA_LORA = 64
A_GATE_LORA = 128
A_GN_EPS = 64e-5
B_NOPE = 128
B_ROPE = 64
B_VDIM = 128
B_HEADS = D_MODEL // 256
B_WIDTH = B_HEADS * B_VDIM
B_Q_LORA = 1536
B_KV_LORA = 512
Q_BLOCK = 128
C_HEAD = 256
C_HEADS = D_MODEL // 512
C_WIDTH = C_HEADS * C_HEAD
C_CHUNK = 128
C_GN_EPS = 1e-5
FFN_HIDDEN = 256 * ((8 * D_MODEL // 3 + 255) // 256)
CONV_WIDTH = 3
N_BRANCH = 3
ROPE_BASE = 10000.0
NORM_EPS = 1e-6

A_IN = 3 * A_WIDTH + 2 * A_DECAY_LORA + 2 * A_AAA_LORA + A_GATE_LORA
B_IN = B_Q_LORA + B_KV_LORA + B_ROPE
C_IN = 4 * C_WIDTH
G_IN = N_BRANCH * D_MODEL
N_IN = A_IN + B_IN + C_IN + G_IN
IN_SPLITS = [A_IN, A_IN + B_IN, A_IN + B_IN + C_IN]
A_SPLITS = [A_WIDTH, 2 * A_WIDTH, 3 * A_WIDTH, 3 * A_WIDTH + 2 * A_DECAY_LORA, 3 * A_WIDTH + 2 * A_DECAY_LORA + 2 * A_AAA_LORA]
B_SPLITS = [B_Q_LORA, B_Q_LORA + B_KV_LORA]
BR_SPLITS = [A_WIDTH, A_WIDTH + B_WIDTH]

kernel_name = 'hybrid_bidir_rwkv7_mla_retnet_encoder'


def rmsnorm(x, g):
    xf = x.astype(jnp.float32)
    y = xf * lax.rsqrt(jnp.mean(xf * xf, axis=-1, keepdims=True) + NORM_EPS)
    return (y * g.astype(jnp.float32)).astype(x.dtype)


def head_norm(x, eps):
    xf = x.astype(jnp.float32)
    mu = jnp.mean(xf, axis=-1, keepdims=True)
    var = jnp.mean(jnp.square(xf - mu), axis=-1, keepdims=True)
    return (xf - mu) * lax.rsqrt(var + eps)


def shift_prev(x):
    return jnp.pad(x[:, :-1], ((0, 0), (1, 0), (0, 0)))


def shift_next(x):
    return jnp.pad(x[:, 1:], ((0, 0), (0, 1), (0, 0)))


def rope(x, pos):
    d = x.shape[-1]
    inv = ROPE_BASE ** (-jnp.arange(0, d, 2, dtype=jnp.float32) / d)
    ang = pos.astype(jnp.float32)[:, None] * inv[None, :]
    ang = ang.reshape(ang.shape[0], *([1] * (x.ndim - 3)), d // 2)
    cos, sin = jnp.cos(ang), jnp.sin(ang)
    xf = x.astype(jnp.float32)
    x1, x2 = xf[..., : d // 2], xf[..., d // 2:]
    return jnp.concatenate([x1 * cos - x2 * sin, x1 * sin + x2 * cos], axis=-1).astype(x.dtype)


def depthwise_conv3(x, w, b):
    return w[0] * shift_prev(x) + w[1] * x + w[2] * shift_next(x) + b


def rwkv7_scan(r, w, k, v, kk, a, reverse):
    def step(S, inp):
        r_t, w_t, k_t, v_t, kk_t, a_t = inp
        sk = jnp.einsum('bhij,bhj->bhi', S, kk_t)
        S = S * w_t[:, :, None, :] - sk[..., None] * (kk_t * a_t)[:, :, None, :] + v_t[..., None] * k_t[:, :, None, :]
        return S, jnp.einsum('bhij,bhj->bhi', S, r_t)
    S0 = jnp.zeros((r.shape[1], A_HEADS, A_HEAD, A_HEAD), jnp.float32)
    _, y = lax.scan(step, S0, (r, w, k, v, kk, a), reverse=reverse)
    return y


def rwkv7_branch(ha, a_mu, a_w0, a_w_up, a_a0, a_a_up, a_g_up, a_k_k, a_k_a, a_r_k, a_ln_w, a_ln_b):
    bsz, T, _ = ha.shape
    f32 = jnp.float32
    hs = (ha + a_mu[0] * (shift_prev(ha) - ha) + a_mu[1] * (shift_next(ha) - ha)).astype(f32)
    r, k, v, wd, ad, gd = jnp.split(hs, A_SPLITS, axis=-1)
    wd = wd.reshape(bsz, T, 2, A_DECAY_LORA)
    ad = ad.reshape(bsz, T, 2, A_AAA_LORA)
    w_log = -jax.nn.softplus(-(a_w0 + jnp.einsum('btzl,zlc->btzc', jnp.tanh(wd), a_w_up))) - 0.5
    decay = jnp.exp(-jnp.exp(w_log))
    a = jax.nn.sigmoid(a_a0 + jnp.einsum('btzl,zlc->btzc', ad, a_a_up))
    g = jax.nn.sigmoid(gd) @ a_g_up
    kk = (k * a_k_k).reshape(bsz, T, A_HEADS, A_HEAD)
    kk = kk / jnp.maximum(jnp.sqrt(jnp.sum(kk * kk, axis=-1, keepdims=True)), 1e-12)
    kz = k[:, :, None, :] * (1.0 + (a - 1.0) * a_k_a)
    tm = lambda t: jnp.moveaxis(t.reshape(bsz, T, A_HEADS, A_HEAD), 1, 0)
    r_t, v_t, kk_t = tm(r), tm(v), tm(kk)
    y = rwkv7_scan(r_t, tm(decay[:, :, 0]), tm(kz[:, :, 0]), v_t, kk_t, tm(a[:, :, 0]), False)
    y = y + rwkv7_scan(r_t, tm(decay[:, :, 1]), tm(kz[:, :, 1]), v_t, kk_t, tm(a[:, :, 1]), True)
    y = jnp.moveaxis(y, 0, 1)
    y = head_norm(y, A_GN_EPS).reshape(bsz, T, A_WIDTH) * a_ln_w + a_ln_b
    rh = r.reshape(bsz, T, A_HEADS, A_HEAD)
    kh = k.reshape(bsz, T, A_HEADS, A_HEAD)
    vh = v.reshape(bsz, T, A_HEADS, A_HEAD)
    bonus = (jnp.sum(rh * kh * a_r_k, axis=-1, keepdims=True) * vh).reshape(bsz, T, A_WIDTH)
    return ((y + bonus) * g).astype(ha.dtype)


def mla_branch(hb, pos, b_q_norm, b_q_up, b_kv_norm, b_kv_up):
    bsz, T, _ = hb.shape
    q_c, kv_c, k_pe = jnp.split(hb, B_SPLITS, axis=-1)
    q = (rmsnorm(q_c, b_q_norm) @ b_q_up).reshape(bsz, T, B_HEADS, B_NOPE + B_ROPE)
    q_nope, q_pe = q[..., :B_NOPE], rope(q[..., B_NOPE:], pos)
    kv = (rmsnorm(kv_c, b_kv_norm) @ b_kv_up).reshape(bsz, T, B_HEADS, B_NOPE + B_VDIM)
    k_nope, v = kv[..., :B_NOPE], kv[..., B_NOPE:]
    k_pe = rope(k_pe, pos)
    scale = (B_NOPE + B_ROPE) ** -0.5
    nb = T // Q_BLOCK
    to_blocks = lambda t: jnp.moveaxis(t.reshape(bsz, nb, Q_BLOCK, *t.shape[2:]), 1, 0)

    def block(args):
        qn, qp = args
        s = jnp.einsum('bqhd,bkhd->bhqk', qn, k_nope) + jnp.einsum('bqhr,bkr->bhqk', qp, k_pe)
        p = jax.nn.softmax(s.astype(jnp.float32) * scale, axis=-1).astype(v.dtype)
        return jnp.einsum('bhqk,bkhd->bqhd', p, v)

    o = lax.map(block, (to_blocks(q_nope), to_blocks(q_pe)))
    return jnp.moveaxis(o, 0, 1).reshape(bsz, T, B_WIDTH)


def retention_cross(q, k, v, log_gamma, reverse):
    chunk_decay = jnp.exp(log_gamma * C_CHUNK)[:, None, None]

    def step(S, inp):
        q_n, k_n, v_n = inp
        out = jnp.einsum('bchd,bhdv->bchv', q_n, S)
        S = S * chunk_decay + jnp.einsum('bchd,bchv->bhdv', k_n, v_n)
        return S, out

    S0 = jnp.zeros((q.shape[1], C_HEADS, C_HEAD, C_HEAD), jnp.float32)
    _, o = lax.scan(step, S0, (q, k, v), reverse=reverse)
    return o


def retention_branch(hc, pos):
    bsz, T, _ = hc.shape
    f32 = jnp.float32
    q, k, v, g = jnp.split(hc, 4, axis=-1)
    heads = lambda t: t.reshape(bsz, T, C_HEADS, C_HEAD)
    q = rope(heads(q), pos).astype(f32) * (C_HEAD ** -0.5)
    k = rope(heads(k), pos).astype(f32)
    v = heads(v).astype(f32)
    log_gamma = jnp.log1p(-(2.0 ** (-5.0 - jnp.arange(C_HEADS, dtype=f32))))
    nc = T // C_CHUNK
    idx = jnp.arange(C_CHUNK, dtype=f32)
    chunk = lambda t: t.reshape(bsz, nc, C_CHUNK, C_HEADS, C_HEAD)
    qc, kc, vc = chunk(q), chunk(k), chunk(v)
    dmask = jnp.exp(log_gamma[:, None, None] * jnp.abs(idx[:, None] - idx[None, :]))
    inner = jnp.einsum('bnhce,bnehv->bnchv', jnp.einsum('bnchd,bnehd->bnhce', qc, kc) * dmask, vc)
    pw = lambda e: jnp.exp(e[:, None] * log_gamma[None, :])[:, :, None]
    tm = lambda t: jnp.moveaxis(t, 1, 0)
    o_f = retention_cross(tm(qc * pw(idx + 1.0)), tm(kc * pw(C_CHUNK - 1.0 - idx)), tm(vc), log_gamma, False)
    o_b = retention_cross(tm(qc * pw(C_CHUNK - idx)), tm(kc * pw(idx)), tm(vc), log_gamma, True)
    o = inner + jnp.moveaxis(o_f + o_b, 0, 1)
    o = head_norm(o, C_GN_EPS).reshape(bsz, T, C_WIDTH)
    return (jax.nn.silu(g.astype(f32)) * o).astype(hc.dtype)


def layer(x, pos, attn_norm, w_in, a_mu, a_w0, a_w_up, a_a0, a_a_up, a_g_up, a_k_k, a_k_a, a_r_k, a_ln_w, a_ln_b,
          b_q_norm, b_q_up, b_kv_norm, b_kv_up, w_branch, w_out, ffn_norm, ffn_up, ffn_conv, ffn_conv_b, ffn_down):
    bsz, T, _ = x.shape
    h = rmsnorm(x, attn_norm) @ w_in
    ha, hb, hc, hg = jnp.split(h, IN_SPLITS, axis=-1)
    oa = rwkv7_branch(ha, a_mu, a_w0, a_w_up, a_a0, a_a_up, a_g_up, a_k_k, a_k_a, a_r_k, a_ln_w, a_ln_b)
    ob = mla_branch(hb, pos, b_q_norm, b_q_up, b_kv_norm, b_kv_up)
    oc = retention_branch(hc, pos)
    wa, wb, wc = jnp.split(w_branch, BR_SPLITS, axis=0)
    gates = jax.nn.sigmoid(hg.reshape(bsz, T, N_BRANCH, D_MODEL))
    merged = gates[:, :, 0] * (oa @ wa) + gates[:, :, 1] * (ob @ wb) + gates[:, :, 2] * (oc @ wc)
    x = x + merged @ w_out
    gate, val = jnp.split(rmsnorm(x, ffn_norm) @ ffn_up, 2, axis=-1)
    gate = depthwise_conv3(gate, ffn_conv, ffn_conv_b)
    return x + (jax.nn.silu(gate) * val) @ ffn_down


def encoder(x, attn_norm, w_in, a_mu, a_w0, a_w_up, a_a0, a_a_up, a_g_up, a_k_k, a_k_a, a_r_k, a_ln_w, a_ln_b,
            b_q_norm, b_q_up, b_kv_norm, b_kv_up, w_branch, w_out, ffn_norm, ffn_up, ffn_conv, ffn_conv_b, ffn_down,
            final_norm):
    pos = jnp.arange(x.shape[1])
    for l in range(DEPTH):
        x = layer(x, pos, attn_norm[l], w_in[l], a_mu[l], a_w0[l], a_w_up[l], a_a0[l], a_a_up[l], a_g_up[l],
                  a_k_k[l], a_k_a[l], a_r_k[l], a_ln_w[l], a_ln_b[l], b_q_norm[l], b_q_up[l], b_kv_norm[l],
                  b_kv_up[l], w_branch[l], w_out[l], ffn_norm[l], ffn_up[l], ffn_conv[l], ffn_conv_b[l], ffn_down[l])
    return rmsnorm(x, final_norm)


def setup_inputs(seed: int = 0) -> dict:
    key = jax.random.key(seed)
    ks = iter(jax.random.split(key, 32))
    f32 = jnp.float32
    nrm = lambda shape, scale: jax.random.normal(next(ks), shape, f32) * scale
    gain = lambda shape: 1.0 + 0.02 * jax.random.normal(next(ks), shape, f32)
    L = DEPTH
    return {
        'x_prompt': nrm((BATCH, SEQ, D_MODEL), 1.0),
        'x_sample': nrm((DEC_BATCH, DEC_SEQ, D_MODEL), 1.0),
        'attn_norm': gain((L, D_MODEL)),
        'w_in': nrm((L, D_MODEL, N_IN), D_MODEL ** -0.5),
        'a_mu': jax.random.uniform(next(ks), (L, 2, A_IN), f32, 0.0, 0.5),
        'a_w0': jax.random.uniform(next(ks), (L, 2, A_WIDTH), f32, -6.0, -1.0),
        'a_w_up': nrm((L, 2, A_DECAY_LORA, A_WIDTH), A_DECAY_LORA ** -0.5),
        'a_a0': nrm((L, 2, A_WIDTH), 0.1),
        'a_a_up': nrm((L, 2, A_AAA_LORA, A_WIDTH), 0.5 * A_AAA_LORA ** -0.5),
        'a_g_up': nrm((L, A_GATE_LORA, A_WIDTH), A_GATE_LORA ** -0.5),
        'a_k_k': 0.85 + nrm((L, A_WIDTH), 0.05),
        'a_k_a': 1.0 + nrm((L, A_WIDTH), 0.05),
        'a_r_k': nrm((L, A_HEADS, A_HEAD), 0.1),
        'a_ln_w': gain((L, A_WIDTH)),
        'a_ln_b': nrm((L, A_WIDTH), 0.02),
        'b_q_norm': gain((L, B_Q_LORA)),
        'b_q_up': nrm((L, B_Q_LORA, B_HEADS * (B_NOPE + B_ROPE)), B_Q_LORA ** -0.5),
        'b_kv_norm': gain((L, B_KV_LORA)),
        'b_kv_up': nrm((L, B_KV_LORA, B_HEADS * (B_NOPE + B_VDIM)), B_KV_LORA ** -0.5),
        'w_branch': nrm((L, A_WIDTH + B_WIDTH + C_WIDTH, D_MODEL), A_WIDTH ** -0.5),
        'w_out': nrm((L, D_MODEL, D_MODEL), D_MODEL ** -0.5),
        'ffn_norm': gain((L, D_MODEL)),
        'ffn_up': nrm((L, D_MODEL, 2 * FFN_HIDDEN), D_MODEL ** -0.5),
        'ffn_conv': nrm((L, CONV_WIDTH, FFN_HIDDEN), CONV_WIDTH ** -0.5),
        'ffn_conv_b': nrm((L, FFN_HIDDEN), 0.02),
        'ffn_down': nrm((L, FFN_HIDDEN, D_MODEL), FFN_HIDDEN ** -0.5),
        'final_norm': gain((D_MODEL,)),
    }


def reference(x_prompt, x_sample, attn_norm, w_in, a_mu, a_w0, a_w_up, a_a0, a_a_up, a_g_up, a_k_k, a_k_a, a_r_k,
              a_ln_w, a_ln_b, b_q_norm, b_q_up, b_kv_norm, b_kv_up, w_branch, w_out, ffn_norm, ffn_up, ffn_conv,
              ffn_conv_b, ffn_down, final_norm):
    weights = (attn_norm, w_in, a_mu, a_w0, a_w_up, a_a0, a_a_up, a_g_up, a_k_k, a_k_a, a_r_k, a_ln_w, a_ln_b,
               b_q_norm, b_q_up, b_kv_norm, b_kv_up, w_branch, w_out, ffn_norm, ffn_up, ffn_conv, ffn_conv_b,
               ffn_down, final_norm)
    y_prompt = encoder(x_prompt, *weights)
    y_sample = encoder(x_sample, *weights)
    return (y_prompt, y_sample)
```

```python
import functools
import math

import jax
import jax.numpy as jnp
from jax import lax
from jax.experimental import pallas as pl
from jax.experimental.pallas import tpu as pltpu

F32 = jnp.float32
BF16 = jnp.bfloat16

D_MODEL = 4096
DEPTH = 2
A_HEAD = 64
A_WIDTH = 2048
A_LORA_IN = 384
A_IN = 3 * A_WIDTH + A_LORA_IN
A_GN_EPS = 64e-5
B_NOPE = 128
B_ROPE = 64
B_VDIM = 128
B_HEADS = 16
B_Q_LORA = 1536
B_KV_LORA = 512
B_QK_PAD = 256
C_HEAD = 256
C_HEADS = 8
C_WIDTH = 2048
C_GN_EPS = 1e-5
FFN_HIDDEN = 11008
ROPE_BASE = 10000.0
NORM_EPS = 1e-6

LANES = 128
SUBLANES = 8
VMEM_LIMIT_BYTES = 56 * 2**20

RWKV_CHUNK = 64
RWKV_HEADS_PER_GROUP = 4
RET_CHUNK = 256


def _params(sem):
    return pltpu.CompilerParams(dimension_semantics=sem, vmem_limit_bytes=VMEM_LIMIT_BYTES)


def _dot(a, b):
    return jnp.dot(a, b, preferred_element_type=F32)


def _dot_nt(a, b):
    return lax.dot_general(a, b, (((1,), (1,)), ((), ())), preferred_element_type=F32)


def _dot_tn(a, b):
    return lax.dot_general(a, b, (((0,), (0,)), ((), ())), preferred_element_type=F32)


def _split(x):
    hi = x.astype(BF16)
    lo = (x - hi.astype(F32)).astype(BF16)
    return hi, lo


def _dot_f32(a, b):
    ah, al = _split(a)
    bh, bl = _split(b)
    return _dot(ah, bh) + _dot(ah, bl) + _dot(al, bh)


def _dot_exact_rhs(a, b_bf16):
    ah, al = _split(a)
    return _dot(ah, b_bf16) + _dot(al, b_bf16)


def _sigmoid(x):
    return 1.0 / (1.0 + jnp.exp(-x))


def _silu(x):
    return x * _sigmoid(x)


def _rmsnorm_body(x_ref, g_ref, o_ref):
    x = x_ref[...]
    y = x * lax.rsqrt(jnp.mean(x * x, axis=-1, keepdims=True) + NORM_EPS)
    o_ref[...] = (y * g_ref[...]).astype(o_ref.dtype)


def _rmsnorm(x, g, out_dtype, tm=256):
    m, d = x.shape
    return pl.pallas_call(
        _rmsnorm_body,
        grid=(m // tm,),
        in_specs=[pl.BlockSpec((tm, d), lambda i: (i, 0)), pl.BlockSpec((1, d), lambda i: (0, 0))],
        out_specs=pl.BlockSpec((tm, d), lambda i: (i, 0)),
        out_shape=jax.ShapeDtypeStruct((m, d), out_dtype),
        compiler_params=_params(("parallel",)),
        name="rmsnorm",
    )(x, g.reshape(1, d))


def _mm_body(a_ref, b_ref, *rest, n_extra, epilogue):
    extras = rest[:n_extra]
    outs = rest[n_extra:]
    acc = _dot(a_ref[...], b_ref[...])
    res = epilogue(acc, *extras)
    for o_ref, r in zip(outs, res):
        o_ref[...] = r.astype(o_ref.dtype)


def _matmul(name, a, b, *, tm, tn, epilogue, out_dtypes, out_widths=None, extras=(), extra_specs=(),
            b_head_major=False):
    m, k = a.shape
    if b_head_major:
        nj = b.shape[0]
        b_spec = pl.BlockSpec((None, k, tn), lambda i, j: (j, 0, 0))
    else:
        nj = b.shape[1] // tn
        b_spec = pl.BlockSpec((k, tn), lambda i, j: (0, j))
    out_widths = out_widths or [tn] * len(out_dtypes)
    if b_head_major:
        out_specs = [pl.BlockSpec((None, tm, w), lambda i, j: (j, i, 0)) for w in out_widths]
        out_shape = [jax.ShapeDtypeStruct((nj, m, w), dt) for w, dt in zip(out_widths, out_dtypes)]
    else:
        out_specs = [pl.BlockSpec((tm, w), lambda i, j: (i, j)) for w in out_widths]
        out_shape = [jax.ShapeDtypeStruct((m, nj * w), dt) for w, dt in zip(out_widths, out_dtypes)]
    return pl.pallas_call(
        functools.partial(_mm_body, n_extra=len(extras), epilogue=epilogue),
        grid=(m // tm, nj),
        in_specs=[pl.BlockSpec((tm, k), lambda i, j: (i, 0)), b_spec, *extra_specs],
        out_specs=out_specs,
        out_shape=out_shape,
        compiler_params=_params(("parallel", "arbitrary")),
        name=name,
    )(a, b, *extras)


def _ep_plain(acc):
    return (acc,)


def _ep_sigmoid(acc):
    return (_sigmoid(acc),)


def _ep_residual(acc, res_ref):
    return (acc + res_ref[...],)


def _ep_qkv_norms(acc, gq_ref, gkv_ref):
    q = acc[:, :B_Q_LORA]
    kv = acc[:, B_Q_LORA:]
    qn = q * lax.rsqrt(jnp.mean(q * q, axis=-1, keepdims=True) + NORM_EPS) * gq_ref[...]
    kvn = kv * lax.rsqrt(jnp.mean(kv * kv, axis=-1, keepdims=True) + NORM_EPS) * gkv_ref[...]
    return qn, kvn


def _rope64_paired(x, cos_ref, sin_ref):
    return x * cos_ref[...] + pltpu.roll(x, 64, axis=1) * sin_ref[...]


def _ep_kpe(acc, cos_ref, sin_ref):
    return (_rope64_paired(acc, cos_ref, sin_ref),)


def _ep_q_up(acc, cos_ref, sin_ref):
    scale = (B_NOPE + B_ROPE) ** -0.5
    return (jnp.concatenate([acc[:, :B_NOPE], _rope64_paired(acc[:, B_NOPE:], cos_ref, sin_ref)], axis=1) * scale,)


def _ep_kv_up(acc, kpe_ref):
    k = jnp.concatenate([acc[:, :B_NOPE].astype(BF16), kpe_ref[...]], axis=1)
    return k, acc[:, B_NOPE:]


def _ep_ret_qk(acc, cos_ref, sin_ref, *, q_tiles):
    cos = cos_ref[...]
    sin = sin_ref[...]
    half = C_HEAD // 2
    pieces = []
    for h in range(acc.shape[1] // C_HEAD):
        x1 = acc[:, h * C_HEAD:h * C_HEAD + half]
        x2 = acc[:, h * C_HEAD + half:(h + 1) * C_HEAD]
        pieces += [x1 * cos - x2 * sin, x1 * sin + x2 * cos]
    out = jnp.concatenate(pieces, axis=1)
    scale = jnp.where(pl.program_id(1) < q_tiles, C_HEAD ** -0.5, 1.0)
    return (out * scale,)


def _ep_ret_vg(acc, *, v_tiles):
    return (jnp.where(pl.program_id(1) < v_tiles, acc, _silu(acc)),)


def _merge_body(oa_ref, ob_ref, oc_ref, w_ref, g_ref, out_ref, acc_ref):
    br = pl.program_id(2)
    g = g_ref[...].astype(F32)

    @pl.when(br == 0)
    def _():
        acc_ref[...] = g * _dot(oa_ref[...], w_ref[...])

    @pl.when(br == 1)
    def _():
        acc_ref[...] += g * _dot(ob_ref[...], w_ref[...])

    @pl.when(br == 2)
    def _():
        out_ref[...] = (acc_ref[...] + g * _dot(oc_ref[...], w_ref[...])).astype(out_ref.dtype)


def _merge(oa, ob, oc, w3, gates, tm=1024, tn=512):
    m, kb = oa.shape
    n = w3.shape[2]
    nj = n // tn
    o_spec = pl.BlockSpec((tm, kb), lambda i, j, br: (i, 0))
    return pl.pallas_call(
        _merge_body,
        grid=(m // tm, nj, 3),
        in_specs=[o_spec, o_spec, o_spec,
                  pl.BlockSpec((None, kb, tn), lambda i, j, br: (br, 0, j)),
                  pl.BlockSpec((tm, tn), lambda i, j, br: (i, br * nj + j))],
        out_specs=pl.BlockSpec((tm, tn), lambda i, j, br: (i, j)),
        out_shape=jax.ShapeDtypeStruct((m, n), BF16),
        scratch_shapes=[pltpu.VMEM((tm, tn), F32)],
        compiler_params=_params(("parallel", "arbitrary", "arbitrary")),
        name="branch_merge",
    )(oa, ob, oc, w3, gates)


def _halo_specs(tm, w, col_block, n_row_blocks8):
    r8 = tm // SUBLANES
    main = pl.BlockSpec((tm, w), lambda i, j: (i, col_block(j)))
    prev = pl.BlockSpec((SUBLANES, w), lambda i, j: (jnp.maximum(i * r8 - 1, 0), col_block(j)))
    nxt = pl.BlockSpec((SUBLANES, w), lambda i, j: (jnp.minimum((i + 1) * r8, n_row_blocks8 - 1), col_block(j)))
    return [main, prev, nxt]


def _shifted(x, prev_ref, next_ref, tiles_per_seq):
    tm = x.shape[0]
    i = pl.program_id(0)
    pos = i % tiles_per_seq
    prev_row = jnp.where(pos == 0, 0.0, prev_ref[SUBLANES - 1:SUBLANES, :])
    next_row = jnp.where(pos == tiles_per_seq - 1, 0.0, next_ref[0:1, :])
    row = lax.broadcasted_iota(jnp.int32, x.shape, 0)
    xp = jnp.where(row == 0, prev_row, pltpu.roll(x, 1, axis=0))
    xn = jnp.where(row == tm - 1, next_row, pltpu.roll(x, tm - 1, axis=0))
    return xp, xn


def _head_sum64(x):
    w = x.shape[1]
    r = lax.broadcasted_iota(jnp.int32, (w, w), 0) >> 6
    c = lax.broadcasted_iota(jnp.int32, (w, w), 1) >> 6
    ones = jnp.where(r == c, 1.0, 0.0).astype(BF16)
    return _dot_exact_rhs(x, ones)


def _rwkv_pre_body(r_ref, rp_ref, rn_ref, k_ref, kp_ref, kn_ref, v_ref, vp_ref, vn_ref, l_ref, lp_ref, ln_ref,
                   mur_ref, muk_ref, muv_ref, mul_ref, w0_ref, a0_ref, wup_ref, aup_ref, gup_ref,
                   kk_w_ref, ka_ref, rk_ref,
                   r_out, v_out, kk_out, g_out, bonus_out, lw0_out, lw1_out, kz0_out, kz1_out, b0_out, b1_out,
                   *, tiles_per_seq):
    def mix(x_ref, p_ref, n_ref, mu_ref):
        x = x_ref[...]
        xp, xn = _shifted(x, p_ref, n_ref, tiles_per_seq)
        return x + mu_ref[0:1, :] * (xp - x) + mu_ref[1:2, :] * (xn - x)

    r = mix(r_ref, rp_ref, rn_ref, mur_ref)
    k = mix(k_ref, kp_ref, kn_ref, muk_ref)
    v = mix(v_ref, vp_ref, vn_ref, muv_ref)
    lo = mix(l_ref, lp_ref, ln_ref, mul_ref)
    wd = jnp.tanh(lo[:, 0:LANES])
    ad = lo[:, LANES:2 * LANES]
    gd = _sigmoid(lo[:, 2 * LANES:3 * LANES])
    lane = lax.broadcasted_iota(jnp.int32, (1, LANES), 1)

    g_out[...] = _dot_f32(gd, gup_ref[...])
    kk = k * kk_w_ref[...]
    kk = kk / jnp.maximum(jnp.sqrt(_head_sum64(kk * kk)), 1e-12)
    r_out[...] = r
    v_out[...] = v
    kk_out[...] = kk
    bonus_out[...] = _head_sum64(r * k * rk_ref[...]) * v
    for z, (lw_out, kz_out, b_out) in enumerate(((lw0_out, kz0_out, b0_out), (lw1_out, kz1_out, b1_out))):
        sel = (lane >> 6) == z
        w_pre = w0_ref[z:z + 1, :] + _dot_f32(jnp.where(sel, wd, 0.0), wup_ref[...])
        w_log = -(jnp.maximum(-w_pre, 0.0) + jnp.log1p(jnp.exp(-jnp.abs(w_pre)))) - 0.5
        lw_out[...] = -jnp.exp(w_log)
        a = _sigmoid(a0_ref[z:z + 1, :] + _dot_f32(jnp.where(sel, ad, 0.0), aup_ref[...]))
        kz_out[...] = k * (1.0 + (a - 1.0) * ka_ref[...])
        b_out[...] = kk * a


def _rwkv_prologue(ha, a_mu, a_w0, a_w_up, a_a0, a_a_up, a_g_up, a_k_k, a_k_a, a_r_k, seq_len, tm=128, tw=512):
    m = ha.shape[0]
    ncb = A_WIDTH // tw
    n8 = m // SUBLANES
    lora_block = 3 * A_WIDTH // A_LORA_IN
    in_specs = []
    for s in range(3):
        in_specs += _halo_specs(tm, tw, lambda j, s=s: s * ncb + j, n8)
    in_specs += _halo_specs(tm, A_LORA_IN, lambda j: lora_block, n8)
    for s in range(3):
        in_specs.append(pl.BlockSpec((2, tw), lambda i, j, s=s: (0, s * ncb + j)))
    in_specs.append(pl.BlockSpec((2, A_LORA_IN), lambda i, j: (0, lora_block)))
    vec2 = pl.BlockSpec((2, tw), lambda i, j: (0, j))
    up = pl.BlockSpec((LANES, tw), lambda i, j: (0, j))
    vec1 = pl.BlockSpec((1, tw), lambda i, j: (0, j))
    in_specs += [vec2, vec2, up, up, up, vec1, vec1, vec1]
    out_spec = pl.BlockSpec((tm, tw), lambda i, j: (i, j))
    out = jax.ShapeDtypeStruct((m, A_WIDTH), F32)
    return pl.pallas_call(
        functools.partial(_rwkv_pre_body, tiles_per_seq=seq_len // tm),
        grid=(m // tm, ncb),
        in_specs=in_specs,
        out_specs=[out_spec] * 11,
        out_shape=[out] * 11,
        compiler_params=_params(("parallel", "arbitrary")),
        name="rwkv_prologue",
    )(ha, ha, ha, ha, ha, ha, ha, ha, ha, ha, ha, ha, a_mu, a_mu, a_mu, a_mu, a_w0, a_a0,
      a_w_up.reshape(LANES, A_WIDTH), a_a_up.reshape(LANES, A_WIDTH), a_g_up,
      a_k_k.reshape(1, A_WIDTH), a_k_a.reshape(1, A_WIDTH), a_r_k.reshape(1, A_WIDTH))


def _bf(x):
    return x.astype(BF16)


def _rwkv_chunk_body(r_ref, v_ref, kk_ref, lw_ref, kz_ref, b_ref, y_ref, state_ref, *, rev, hp, chunk):
    c = chunk
    w = A_HEAD * hp
    rows = c * hp

    @pl.when(pl.program_id(2) == 0)
    def _():
        state_ref[...] = jnp.zeros_like(state_ref)

    lw = lw_ref[...]
    ti = lax.broadcasted_iota(jnp.int32, (c, c), 0)
    si = lax.broadcasted_iota(jnp.int32, (c, c), 1)
    tri = jnp.where((si >= ti) if rev else (si <= ti), 1.0, 0.0).astype(BF16)
    hi = lw.astype(BF16)
    mid = (lw - hi.astype(F32)).astype(BF16)
    lo = (lw - hi.astype(F32) - mid.astype(F32)).astype(BF16)
    cum = _dot(tri, hi) + _dot(tri, mid) + _dot(tri, lo)
    tot = cum[0:1, :] if rev else cum[c - 1:c, :]
    e_neg = jnp.exp(-cum)
    e_rest = jnp.exp(tot - cum)
    kz = kz_ref[...]
    b = b_ref[...]

    lane_head = lax.broadcasted_iota(jnp.int32, (c, w), 1) >> 6

    def stack(x):
        return _bf(jnp.concatenate([jnp.where(lane_head == h, x, 0.0) for h in range(hp)], axis=0))

    xs = stack(kk_ref[...] * jnp.exp(cum - lw))
    rs_f = jnp.concatenate([jnp.where(lane_head == h, r_ref[...] * jnp.exp(cum), 0.0) for h in range(hp)], axis=0)
    rs = _bf(rs_f)
    bs = stack(b * e_neg)
    ks = stack(kz * e_neg)
    bh = stack(b * e_rest)
    kh = stack(kz * e_rest)
    vs = stack(v_ref[...])

    gram = _dot_nt(jnp.concatenate([xs, rs], axis=0), jnp.concatenate([bs, ks], axis=0))
    rt = lax.broadcasted_iota(jnp.int32, (rows, rows), 0) & (c - 1)
    cs = lax.broadcasted_iota(jnp.int32, (rows, rows), 1) & (c - 1)
    strict = (cs > rt) if rev else (cs < rt)
    incl = (cs >= rt) if rev else (cs <= rt)
    l_b = jnp.where(strict, gram[:rows, :rows], 0.0)
    l_k = jnp.where(strict, gram[:rows, rows:], 0.0)
    m_b = jnp.where(incl, gram[rows:, :rows], 0.0)
    m_k = jnp.where(incl, gram[rows:, rows:], 0.0)

    eye = jnp.where(lax.broadcasted_iota(jnp.int32, (rows, rows), 0) == lax.broadcasted_iota(jnp.int32, (rows, rows), 1), 1.0, 0.0)
    t_inv = eye - l_b
    l_pow = l_b
    for _ in range(int(math.log2(c)) - 1):
        lp = _bf(l_pow)
        l_pow = _dot(lp, lp)
        t_inv = t_inv + _dot(_bf(t_inv), _bf(l_pow))

    w_m = _dot(_bf(l_k), vs)
    gh = _dot(_bf(t_inv), jnp.concatenate([xs, _bf(w_m)], axis=1))
    gh_b = _bf(gh)
    bt_gh = _dot_tn(bh, gh_b)
    kt_v = _dot_tn(kh, vs)
    ew = lax.broadcasted_iota(jnp.int32, (w, w), 0) == lax.broadcasted_iota(jnp.int32, (w, w), 1)
    phi = jnp.where(ew, jnp.exp(tot), 0.0) - bt_gh[:, :w]
    psi = kt_v - bt_gh[:, w:]
    mb_gh = _dot(_bf(m_b), gh_b)
    q_p = rs_f - mb_gh[:, :w]
    y0 = _dot(_bf(m_k), vs) - mb_gh[:, w:]

    a_prev = state_ref[...]
    a_prev_b = _bf(a_prev)
    y_st = _dot(_bf(q_p), a_prev_b) + y0
    state_ref[...] = _dot(_bf(phi), a_prev_b) + psi
    y = y_st[0:c, :]
    for h in range(1, hp):
        y = y + y_st[h * c:(h + 1) * c, :]
    y_ref[...] = y


def _rwkv_scan(r, v, kk, lw, kz, b, batch, seq_len, rev, hp=RWKV_HEADS_PER_GROUP, chunk=RWKV_CHUNK):
    m = r.shape[0]
    w = A_HEAD * hp
    nc = seq_len // chunk
    if rev:
        idx = lambda bi, g, n: (bi * nc + nc - 1 - n, g)
    else:
        idx = lambda bi, g, n: (bi * nc + n, g)
    spec = pl.BlockSpec((chunk, w), idx)
    return pl.pallas_call(
        functools.partial(_rwkv_chunk_body, rev=rev, hp=hp, chunk=chunk),
        grid=(batch, A_WIDTH // w, nc),
        in_specs=[spec] * 6,
        out_specs=spec,
        out_shape=jax.ShapeDtypeStruct((m, A_WIDTH), F32),
        scratch_shapes=[pltpu.VMEM((w, w), F32)],
        compiler_params=_params(("parallel", "parallel", "arbitrary")),
        name="rwkv_scan_bwd" if rev else "rwkv_scan_fwd",
    )(r, v, kk, lw, kz, b)


def _rwkv_post_body(yf_ref, yb_ref, bonus_ref, g_ref, lnw_ref, lnb_ref, o_ref):
    y = yf_ref[...] + yb_ref[...]
    inv_n = 1.0 / A_HEAD
    mu = _head_sum64(y) * inv_n
    d = y - mu
    var = _head_sum64(d * d) * inv_n
    yn = d * lax.rsqrt(var + A_GN_EPS) * lnw_ref[...] + lnb_ref[...]
    o_ref[...] = ((yn + bonus_ref[...]) * g_ref[...]).astype(o_ref.dtype)


def _rwkv_epilogue(yf, yb, bonus, g, ln_w, ln_b, tm=256, tw=512):
    m = yf.shape[0]
    spec = pl.BlockSpec((tm, tw), lambda i, j: (i, j))
    vec = pl.BlockSpec((1, tw), lambda i, j: (0, j))
    return pl.pallas_call(
        _rwkv_post_body,
        grid=(m // tm, A_WIDTH // tw),
        in_specs=[spec, spec, spec, spec, vec, vec],
        out_specs=spec,
        out_shape=jax.ShapeDtypeStruct((m, A_WIDTH), BF16),
        compiler_params=_params(("parallel", "parallel")),
        name="rwkv_epilogue",
    )(yf, yb, bonus, g, ln_w.reshape(1, A_WIDTH), ln_b.reshape(1, A_WIDTH))


def _attn_body(q_ref, k_ref, v_ref, o_ref):
    s = _dot_nt(q_ref[...], k_ref[...])
    p = jnp.exp(s - jnp.max(s, axis=-1, keepdims=True))
    l = jnp.sum(p, axis=-1, keepdims=True)
    o_ref[...] = (_dot(_bf(p), v_ref[...]) / l).astype(o_ref.dtype)


def _attention(q, k, v, batch, seq_len, tq):
    m = q.shape[1]
    nq = seq_len // tq
    return pl.pallas_call(
        _attn_body,
        grid=(B_HEADS, batch, nq),
        in_specs=[pl.BlockSpec((None, tq, B_QK_PAD), lambda h, bi, qi: (h, bi * nq + qi, 0)),
                  pl.BlockSpec((None, seq_len, B_QK_PAD), lambda h, bi, qi: (h, bi, 0)),
                  pl.BlockSpec((None, seq_len, B_VDIM), lambda h, bi, qi: (h, bi, 0))],
        out_specs=pl.BlockSpec((tq, B_VDIM), lambda h, bi, qi: (bi * nq + qi, h)),
        out_shape=jax.ShapeDtypeStruct((m, B_HEADS * B_VDIM), BF16),
        compiler_params=_params(("parallel", "parallel", "arbitrary")),
        name="mla_attention",
    )(q, k, v)


def _ret_body(lg_ref, q_ref, k_ref, v_ref, *rest, rev, final, chunk):
    if final:
        other_ref, gate_ref, o_ref, state_ref = rest
    else:
        o_ref, state_ref = rest
    c = chunk

    @pl.when(pl.program_id(2) == 0)
    def _():
        state_ref[...] = jnp.zeros_like(state_ref)

    lg = lg_ref[0:1, 0:1]
    q = q_ref[...]
    k = k_ref[...]
    v = v_ref[...]
    idx = lax.broadcasted_iota(jnp.int32, (c, 1), 0).astype(F32)
    if rev:
        q_pow, k_pow = c - idx, idx
    else:
        q_pow, k_pow = idx + 1.0, c - 1.0 - idx
    qd = _bf(q.astype(F32) * jnp.exp(lg * q_pow))
    kd = _bf(k.astype(F32) * jnp.exp(lg * k_pow))
    state = state_ref[...]
    cross = _dot(qd, _bf(state))
    state_ref[...] = state * jnp.exp(lg * float(c)) + _dot_tn(kd, v)
    if final:
        ti = lax.broadcasted_iota(jnp.int32, (c, c), 0)
        si = lax.broadcasted_iota(jnp.int32, (c, c), 1)
        dmask = jnp.exp(lg * jnp.abs(ti - si).astype(F32))
        inner = _dot(_bf(_dot_nt(q, k) * dmask), v)
        o = inner + cross + other_ref[...]
        mu = jnp.mean(o, axis=-1, keepdims=True)
        d = o - mu
        var = jnp.mean(d * d, axis=-1, keepdims=True)
        o_ref[...] = (gate_ref[...].astype(F32) * (d * lax.rsqrt(var + C_GN_EPS))).astype(o_ref.dtype)
    else:
        o_ref[...] = cross


def _retention_pass(log_gamma, qk, vg, other, batch, seq_len, rev, final, chunk=RET_CHUNK):
    m = qk.shape[0]
    nc = seq_len // chunk
    if rev:
        row = lambda bi, n: bi * nc + nc - 1 - n
    else:
        row = lambda bi, n: bi * nc + n
    blk = lambda off: pl.BlockSpec((chunk, C_HEAD), lambda h, bi, n: (row(bi, n), h + off))
    in_specs = [pl.BlockSpec((None, SUBLANES, LANES), lambda h, bi, n: (h, 0, 0)), blk(0), blk(C_HEADS), blk(0)]
    args = [log_gamma, qk, qk, vg]
    if final:
        in_specs += [blk(0), blk(C_HEADS)]
        args += [other, vg]
    return pl.pallas_call(
        functools.partial(_ret_body, rev=rev, final=final, chunk=chunk),
        grid=(C_HEADS, batch, nc),
        in_specs=in_specs,
        out_specs=blk(0),
        out_shape=jax.ShapeDtypeStruct((m, C_WIDTH), BF16 if final else F32),
        scratch_shapes=[pltpu.VMEM((C_HEAD, C_HEAD), F32)],
        compiler_params=_params(("parallel", "parallel", "arbitrary")),
        name="retention_final" if final else "retention_cross",
    )(*args)


def _ffn_act_body(g_ref, gp_ref, gn_ref, val_ref, cw_ref, cb_ref, o_ref, *, tiles_per_seq):
    g = g_ref[...]
    gp, gn = _shifted(g, gp_ref, gn_ref, tiles_per_seq)
    conv = cw_ref[0:1, :] * gp + cw_ref[1:2, :] * g + cw_ref[2:3, :] * gn + cb_ref[...]
    o_ref[...] = (_silu(conv) * val_ref[...]).astype(o_ref.dtype)


def _ffn_act(gv, conv_w, conv_b, seq_len, tm=256, tw=256):
    m = gv.shape[0]
    ncb = FFN_HIDDEN // tw
    return pl.pallas_call(
        functools.partial(_ffn_act_body, tiles_per_seq=seq_len // tm),
        grid=(m // tm, ncb),
        in_specs=[*_halo_specs(tm, tw, lambda j: j, m // SUBLANES),
                  pl.BlockSpec((tm, tw), lambda i, j: (i, ncb + j)),
                  pl.BlockSpec((3, tw), lambda i, j: (0, j)),
                  pl.BlockSpec((1, tw), lambda i, j: (0, j))],
        out_specs=pl.BlockSpec((tm, tw), lambda i, j: (i, j)),
        out_shape=jax.ShapeDtypeStruct((m, FFN_HIDDEN), BF16),
        compiler_params=_params(("parallel", "arbitrary")),
        name="ffn_conv_gate",
    )(gv, gv, gv, gv, conv_w, conv_b.reshape(1, FFN_HIDDEN))


def _rope_tables(seq_len, dim):
    inv = ROPE_BASE ** (-jnp.arange(0, dim, 2, dtype=F32) / dim)
    ang = jnp.arange(seq_len, dtype=F32)[:, None] * inv[None, :]
    return jnp.cos(ang), jnp.sin(ang)


def _swap_halves_cols(w):
    h = w.shape[-1] // 2
    return jnp.concatenate([w[..., h:], w[..., :h]], axis=-1)


def _prep_layer(w_in, b_q_up, b_kv_up, w_branch, w_out, ffn_up, ffn_down):
    o_b = A_IN
    o_kpe = o_b + B_Q_LORA + B_KV_LORA
    o_c = o_kpe + B_ROPE
    o_g = o_c + 4 * C_WIDTH
    w_kpe = w_in[:, o_kpe:o_c]
    qh = b_q_up.reshape(B_Q_LORA, B_HEADS, B_NOPE + B_ROPE)
    q_pe = qh[..., B_NOPE:]
    w_q = jnp.concatenate([qh[..., :B_NOPE], q_pe, _swap_halves_cols(q_pe)], axis=-1)
    return dict(
        w_a=jnp.pad(w_in[:, :A_IN], ((0, 0), (0, LANES))).astype(BF16),
        w_b=w_in[:, o_b:o_kpe].astype(BF16),
        w_kpe=jnp.concatenate([w_kpe, _swap_halves_cols(w_kpe)], axis=-1).astype(BF16),
        w_cqk=w_in[:, o_c:o_c + 2 * C_WIDTH].astype(BF16),
        w_cvg=w_in[:, o_c + 2 * C_WIDTH:o_g].astype(BF16),
        w_g=w_in[:, o_g:].astype(BF16),
        w_q=jnp.transpose(w_q, (1, 0, 2)).astype(BF16),
        w_kv=jnp.transpose(b_kv_up.reshape(B_KV_LORA, B_HEADS, B_NOPE + B_VDIM), (1, 0, 2)).astype(BF16),
        w_br=w_branch.reshape(3, A_WIDTH, D_MODEL).astype(BF16),
        w_out=w_out.astype(BF16),
        ffn_up=ffn_up.astype(BF16),
        ffn_down=ffn_down.astype(BF16),
    )


def _layer(x, batch, seq_len, tabs, pw, attn_norm, a_mu, a_w0, a_w_up, a_a0, a_a_up, a_g_up, a_k_k, a_k_a, a_r_k,
           a_ln_w, a_ln_b, b_q_norm, b_kv_norm, ffn_norm, ffn_conv, ffn_conv_b):
    m = x.shape[0]
    tps = lambda tm: seq_len // tm
    xn = _rmsnorm(x, attn_norm, BF16)

    (ha,) = _matmul("in_proj_a", xn, pw["w_a"], tm=1024, tn=512, epilogue=_ep_plain, out_dtypes=[F32])
    r, v, kk, g, bonus, lw0, lw1, kz0, kz1, b0, b1 = _rwkv_prologue(
        ha, a_mu, a_w0, a_w_up, a_a0, a_a_up, a_g_up, a_k_k, a_k_a, a_r_k, seq_len)
    yf = _rwkv_scan(r, v, kk, lw0, kz0, b0, batch, seq_len, rev=False)
    yb = _rwkv_scan(r, v, kk, lw1, kz1, b1, batch, seq_len, rev=True)
    oa = _rwkv_epilogue(yf, yb, bonus, g, a_ln_w, a_ln_b)

    tm_b = 512
    vec = lambda n: pl.BlockSpec((1, n), lambda i, j: (0, 0))
    qn, kvn = _matmul("in_proj_b", xn, pw["w_b"], tm=tm_b, tn=B_Q_LORA + B_KV_LORA, epilogue=_ep_qkv_norms,
                      out_dtypes=[BF16, BF16], out_widths=[B_Q_LORA, B_KV_LORA],
                      extras=(b_q_norm.reshape(1, -1), b_kv_norm.reshape(1, -1)),
                      extra_specs=(vec(B_Q_LORA), vec(B_KV_LORA)))
    tab64 = lambda tm: pl.BlockSpec((tm, LANES), lambda i, j: (i % tps(tm), 0))
    (kpe,) = _matmul("in_proj_kpe", xn, pw["w_kpe"], tm=1024, tn=LANES, epilogue=_ep_kpe, out_dtypes=[BF16],
                     extras=(tabs["cos64"], tabs["sin64"]), extra_specs=(tab64(1024), tab64(1024)))
    (q,) = _matmul("q_up", qn, pw["w_q"], tm=1024, tn=B_QK_PAD, epilogue=_ep_q_up, out_dtypes=[BF16],
                   extras=(tabs["cos64"], tabs["sin64"]), extra_specs=(tab64(1024), tab64(1024)), b_head_major=True)
    k, vv = _matmul("kv_up", kvn, pw["w_kv"], tm=1024, tn=B_NOPE + B_VDIM, epilogue=_ep_kv_up,
                    out_dtypes=[BF16, BF16], out_widths=[B_QK_PAD, B_VDIM],
                    extras=(kpe,), extra_specs=(pl.BlockSpec((1024, LANES), lambda i, j: (i, 0)),), b_head_major=True)
    ob = _attention(q, k, vv, batch, seq_len, tq=min(512, 2**20 // seq_len))

    tab128 = pl.BlockSpec((1024, LANES), lambda i, j: (i % tps(1024), 0))
    q_tiles = C_WIDTH // 512
    (qk,) = _matmul("in_proj_c_qk", xn, pw["w_cqk"], tm=1024, tn=512,
                    epilogue=functools.partial(_ep_ret_qk, q_tiles=q_tiles), out_dtypes=[BF16],
                    extras=(tabs["cos256"], tabs["sin256"]), extra_specs=(tab128, tab128))
    (vg,) = _matmul("in_proj_c_vg", xn, pw["w_cvg"], tm=1024, tn=512,
                    epilogue=functools.partial(_ep_ret_vg, v_tiles=q_tiles), out_dtypes=[BF16])
    cross_b = _retention_pass(tabs["log_gamma"], qk, vg, None, batch, seq_len, rev=True, final=False)
    oc = _retention_pass(tabs["log_gamma"], qk, vg, cross_b, batch, seq_len, rev=False, final=True)

    (gates,) = _matmul("in_proj_gates", xn, pw["w_g"], tm=1024, tn=1024, epilogue=_ep_sigmoid, out_dtypes=[BF16])
    merged = _merge(oa, ob, oc, pw["w_br"], gates)
    res_spec = lambda tm, tn: pl.BlockSpec((tm, tn), lambda i, j: (i, j))
    (x,) = _matmul("out_proj", merged, pw["w_out"], tm=1024, tn=512, epilogue=_ep_residual, out_dtypes=[F32],
                   extras=(x,), extra_specs=(res_spec(1024, 512),))

    xn2 = _rmsnorm(x, ffn_norm, BF16)
    (gv,) = _matmul("ffn_up", xn2, pw["ffn_up"], tm=1024, tn=256, epilogue=_ep_plain, out_dtypes=[F32])
    act = _ffn_act(gv, ffn_conv, ffn_conv_b, seq_len)
    (x,) = _matmul("ffn_down", act, pw["ffn_down"], tm=512, tn=512, epilogue=_ep_residual, out_dtypes=[F32],
                   extras=(x,), extra_specs=(res_spec(512, 512),))
    return x


def _tables(seq_len):
    c64, s64 = _rope_tables(seq_len, B_ROPE)
    z = jnp.zeros((seq_len, B_ROPE), F32)
    c256, s256 = _rope_tables(seq_len, C_HEAD)
    log_gamma = jnp.log1p(-(2.0 ** (-5.0 - jnp.arange(C_HEADS, dtype=F32))))
    return dict(
        cos64=jnp.concatenate([c64, c64, z], axis=1),
        sin64=jnp.concatenate([-s64, s64, z], axis=1),
        cos256=c256, sin256=s256,
        log_gamma=jnp.broadcast_to(log_gamma[:, None, None], (C_HEADS, SUBLANES, LANES)),
    )


def _encoder(x3, prepped, per_layer, final_norm):
    batch, seq_len, d = x3.shape
    x = x3.reshape(batch * seq_len, d)
    tabs = _tables(seq_len)
    for pw, lw in zip(prepped, per_layer):
        x = _layer(x, batch, seq_len, tabs, pw, *lw)
    return _rmsnorm(x, final_norm, F32).reshape(batch, seq_len, d)


def kernel(x_prompt, x_sample, attn_norm, w_in, a_mu, a_w0, a_w_up, a_a0, a_a_up, a_g_up, a_k_k, a_k_a, a_r_k, a_ln_w, a_ln_b, b_q_norm, b_q_up, b_kv_norm, b_kv_up, w_branch, w_out, ffn_norm, ffn_up, ffn_conv, ffn_conv_b, ffn_down, final_norm):
    depth = w_in.shape[0]
    prepped = [_prep_layer(w_in[l], b_q_up[l], b_kv_up[l], w_branch[l], w_out[l], ffn_up[l], ffn_down[l])
               for l in range(depth)]
    per_layer = [(attn_norm[l], a_mu[l], a_w0[l], a_w_up[l], a_a0[l], a_a_up[l], a_g_up[l], a_k_k[l], a_k_a[l],
                  a_r_k[l], a_ln_w[l], a_ln_b[l], b_q_norm[l], b_kv_norm[l], ffn_norm[l], ffn_conv[l], ffn_conv_b[l])
                 for l in range(depth)]
    return (_encoder(x_prompt, prepped, per_layer, final_norm), _encoder(x_sample, prepped, per_layer, final_norm))
```

```python
import functools
import math

import jax
import jax.numpy as jnp
from jax import lax
from jax.experimental import pallas as pl
from jax.experimental.pallas import tpu as pltpu

F32 = jnp.float32
BF16 = jnp.bfloat16

D_MODEL = 4096
DEPTH = 2
A_HEAD = 64
A_WIDTH = 2048
A_LORA_IN = 384
A_IN = 3 * A_WIDTH + A_LORA_IN
A_GN_EPS = 64e-5
B_NOPE = 128
B_ROPE = 64
B_VDIM = 128
B_HEADS = 16
B_Q_LORA = 1536
B_KV_LORA = 512
B_QK_PAD = 256
C_HEAD = 256
C_HEADS = 8
C_WIDTH = 2048
C_GN_EPS = 1e-5
FFN_HIDDEN = 11008
ROPE_BASE = 10000.0
NORM_EPS = 1e-6

LANES = 128
SUBLANES = 8
VMEM_LIMIT_BYTES = 56 * 2**20

RWKV_CHUNK = 64
RWKV_HEADS_PER_GROUP = 2
RWKV_GROUPS_PER_STEP = 2
RWKV_CHUNKS_PER_STEP = 2
RET_CHUNK = 256


def _params(sem):
    return pltpu.CompilerParams(dimension_semantics=sem, vmem_limit_bytes=VMEM_LIMIT_BYTES)


def _dot(a, b):
    return jnp.dot(a, b, preferred_element_type=F32)


def _dot_nt(a, b):
    return lax.dot_general(a, b, (((1,), (1,)), ((), ())), preferred_element_type=F32)


def _dot_tn(a, b):
    return lax.dot_general(a, b, (((0,), (0,)), ((), ())), preferred_element_type=F32)


def _bf(x):
    return x.astype(BF16)


def _split(x):
    hi = x.astype(BF16)
    lo = (x - hi.astype(F32)).astype(BF16)
    return hi, lo


def _dot_exact_rhs(a, b_bf16):
    ah, al = _split(a)
    return _dot(ah, b_bf16) + _dot(al, b_bf16)


def _sigmoid(x):
    return 1.0 / (1.0 + jnp.exp(-x))


def _silu(x):
    return x * _sigmoid(x)


def _rmsnorm_body(x_ref, g_ref, o_ref):
    x = x_ref[...]
    y = x * lax.rsqrt(jnp.mean(x * x, axis=-1, keepdims=True) + NORM_EPS)
    o_ref[...] = (y * g_ref[...]).astype(o_ref.dtype)


def _rmsnorm(x, g, out_dtype, tm=256):
    m, d = x.shape
    return pl.pallas_call(
        _rmsnorm_body,
        grid=(m // tm,),
        in_specs=[pl.BlockSpec((tm, d), lambda i: (i, 0)), pl.BlockSpec((1, d), lambda i: (0, 0))],
        out_specs=pl.BlockSpec((tm, d), lambda i: (i, 0)),
        out_shape=jax.ShapeDtypeStruct((m, d), out_dtype),
        compiler_params=_params(("parallel",)),
        name="rmsnorm",
    )(x, g.reshape(1, d))


def _mm_body(a_ref, b_ref, *rest, n_extra, epilogue):
    extras = rest[:n_extra]
    outs = rest[n_extra:]
    acc = _dot(a_ref[...], b_ref[...])
    res = epilogue(acc, *extras)
    for o_ref, r in zip(outs, res):
        o_ref[...] = r.astype(o_ref.dtype)


def _matmul(name, a, b, *, tm, tn, epilogue, out_dtypes, out_widths=None, extras=(), extra_specs=(),
            b_head_major=False):
    m, k = a.shape
    if b_head_major:
        nj = b.shape[0]
        b_spec = pl.BlockSpec((None, k, tn), lambda i, j: (j, 0, 0))
    else:
        nj = b.shape[1] // tn
        b_spec = pl.BlockSpec((k, tn), lambda i, j: (0, j))
    out_widths = out_widths or [tn] * len(out_dtypes)
    if b_head_major:
        out_specs = [pl.BlockSpec((None, tm, w), lambda i, j: (j, i, 0)) for w in out_widths]
        out_shape = [jax.ShapeDtypeStruct((nj, m, w), dt) for w, dt in zip(out_widths, out_dtypes)]
    else:
        out_specs = [pl.BlockSpec((tm, w), lambda i, j: (i, j)) for w in out_widths]
        out_shape = [jax.ShapeDtypeStruct((m, nj * w), dt) for w, dt in zip(out_widths, out_dtypes)]
    return pl.pallas_call(
        functools.partial(_mm_body, n_extra=len(extras), epilogue=epilogue),
        grid=(m // tm, nj),
        in_specs=[pl.BlockSpec((tm, k), lambda i, j: (i, 0)), b_spec, *extra_specs],
        out_specs=out_specs,
        out_shape=out_shape,
        compiler_params=_params(("parallel", "arbitrary")),
        name=name,
    )(a, b, *extras)


def _ep_plain(acc):
    return (acc,)


def _ep_sigmoid(acc):
    return (_sigmoid(acc),)


def _ep_residual(acc, res_ref):
    return (acc + res_ref[...],)


def _ep_qkv_norms(acc, gq_ref, gkv_ref):
    q = acc[:, :B_Q_LORA]
    kv = acc[:, B_Q_LORA:]
    qn = q * lax.rsqrt(jnp.mean(q * q, axis=-1, keepdims=True) + NORM_EPS) * gq_ref[...]
    kvn = kv * lax.rsqrt(jnp.mean(kv * kv, axis=-1, keepdims=True) + NORM_EPS) * gkv_ref[...]
    return qn, kvn


def _rope64_paired(x, cos_ref, sin_ref):
    return x * cos_ref[...] + pltpu.roll(x, 64, axis=1) * sin_ref[...]


def _ep_kpe(acc, cos_ref, sin_ref):
    return (_rope64_paired(acc, cos_ref, sin_ref),)


def _ep_q_up(acc, cos_ref, sin_ref):
    scale = (B_NOPE + B_ROPE) ** -0.5
    return (jnp.concatenate([acc[:, :B_NOPE], _rope64_paired(acc[:, B_NOPE:], cos_ref, sin_ref)], axis=1) * scale,)


def _ep_kv_up(acc, kpe_ref):
    k = jnp.concatenate([acc[:, :B_NOPE].astype(BF16), kpe_ref[...]], axis=1)
    return k, acc[:, B_NOPE:]


def _ep_ret_qk(acc, cos_ref, sin_ref, *, q_tiles):
    cos = cos_ref[...]
    sin = sin_ref[...]
    half = C_HEAD // 2
    pieces = []
    for h in range(acc.shape[1] // C_HEAD):
        x1 = acc[:, h * C_HEAD:h * C_HEAD + half]
        x2 = acc[:, h * C_HEAD + half:(h + 1) * C_HEAD]
        pieces += [x1 * cos - x2 * sin, x1 * sin + x2 * cos]
    out = jnp.concatenate(pieces, axis=1)
    scale = jnp.where(pl.program_id(1) < q_tiles, C_HEAD ** -0.5, 1.0)
    return (out * scale,)


def _ep_ret_vg(acc, *, v_tiles):
    return (jnp.where(pl.program_id(1) < v_tiles, acc, _silu(acc)),)


def _merge_body(oa_ref, ob_ref, oc_ref, w_ref, g_ref, out_ref, acc_ref):
    br = pl.program_id(2)
    g = g_ref[...].astype(F32)

    @pl.when(br == 0)
    def _():
        acc_ref[...] = g * _dot(oa_ref[...], w_ref[...])

    @pl.when(br == 1)
    def _():
        acc_ref[...] += g * _dot(ob_ref[...], w_ref[...])

    @pl.when(br == 2)
    def _():
        out_ref[...] = (acc_ref[...] + g * _dot(oc_ref[...], w_ref[...])).astype(out_ref.dtype)


def _merge(oa, ob, oc, w3, gates, tm=1024, tn=512):
    m, kb = oa.shape
    n = w3.shape[2]
    nj = n // tn
    o_spec = pl.BlockSpec((tm, kb), lambda i, j, br: (i, 0))
    return pl.pallas_call(
        _merge_body,
        grid=(m // tm, nj, 3),
        in_specs=[o_spec, o_spec, o_spec,
                  pl.BlockSpec((None, kb, tn), lambda i, j, br: (br, 0, j)),
                  pl.BlockSpec((tm, tn), lambda i, j, br: (i, br * nj + j))],
        out_specs=pl.BlockSpec((tm, tn), lambda i, j, br: (i, j)),
        out_shape=jax.ShapeDtypeStruct((m, n), BF16),
        scratch_shapes=[pltpu.VMEM((tm, tn), F32)],
        compiler_params=_params(("parallel", "arbitrary", "arbitrary")),
        name="branch_merge",
    )(oa, ob, oc, w3, gates)


def _halo_specs(tm, w, col_block, n_rows, halo=SUBLANES):
    per_tile = tm // halo
    last = n_rows // halo - 1
    main = pl.BlockSpec((tm, w), lambda i, j: (i, col_block(j)))
    prev = pl.BlockSpec((halo, w), lambda i, j: (jnp.maximum(i * per_tile - 1, 0), col_block(j)))
    nxt = pl.BlockSpec((halo, w), lambda i, j: (jnp.minimum((i + 1) * per_tile, last), col_block(j)))
    return [main, prev, nxt]


def _shifted(x, prev_ref, next_ref, tiles_per_seq):
    tm = x.shape[0]
    i = pl.program_id(0)
    pos = i % tiles_per_seq
    halo = prev_ref.shape[0]
    prev_row = jnp.where(pos == 0, 0.0, prev_ref[halo - 1:halo, :].astype(F32))
    next_row = jnp.where(pos == tiles_per_seq - 1, 0.0, next_ref[0:1, :].astype(F32))
    row = lax.broadcasted_iota(jnp.int32, x.shape, 0)
    xp = jnp.where(row == 0, prev_row, pltpu.roll(x, 1, axis=0))
    xn = jnp.where(row == tm - 1, next_row, pltpu.roll(x, tm - 1, axis=0))
    return xp, xn


def _head_sum64(x):
    w = x.shape[1]
    r = lax.broadcasted_iota(jnp.int32, (w, w), 0) >> 6
    c = lax.broadcasted_iota(jnp.int32, (w, w), 1) >> 6
    ones = jnp.where(r == c, 1.0, 0.0).astype(BF16)
    return _dot_exact_rhs(x, ones)


def _rwkv_pre_body(r_ref, rp_ref, rn_ref, k_ref, kp_ref, kn_ref, v_ref, vp_ref, vn_ref, l_ref, lp_ref, ln_ref,
                   mur_ref, muk_ref, muv_ref, mul_ref, w0_ref, a0_ref, wup_ref, aup_ref, gup_ref,
                   kk_w_ref, ka_ref, rk_ref,
                   r_out, v_out, kk_out, g_out, bonus_out, lw0_out, lw1_out, kz0_out, kz1_out, b0_out, b1_out,
                   *, tiles_per_seq):
    def mix(x_ref, p_ref, n_ref, mu_ref):
        x = x_ref[...]
        xp, xn = _shifted(x, p_ref, n_ref, tiles_per_seq)
        return x + mu_ref[0:1, :] * (xp - x) + mu_ref[1:2, :] * (xn - x)

    r = mix(r_ref, rp_ref, rn_ref, mur_ref)
    k = mix(k_ref, kp_ref, kn_ref, muk_ref)
    v = mix(v_ref, vp_ref, vn_ref, muv_ref)
    lo = mix(l_ref, lp_ref, ln_ref, mul_ref)
    wd = jnp.tanh(lo[:, 0:LANES])
    ad = lo[:, LANES:2 * LANES]
    gd = _sigmoid(lo[:, 2 * LANES:3 * LANES])
    lane = lax.broadcasted_iota(jnp.int32, (1, LANES), 1)

    g_out[...] = _dot(_bf(gd), gup_ref[...])
    kk = k * kk_w_ref[...]
    kk = kk / jnp.maximum(jnp.sqrt(_head_sum64(kk * kk)), 1e-12)
    r_out[...] = r
    v_out[...] = v
    kk_out[...] = kk
    bonus_out[...] = _head_sum64(r * k * rk_ref[...]) * v
    for z, (lw_out, kz_out, b_out) in enumerate(((lw0_out, kz0_out, b0_out), (lw1_out, kz1_out, b1_out))):
        sel = (lane >> 6) == z
        w_pre = w0_ref[z:z + 1, :] + _dot(_bf(jnp.where(sel, wd, 0.0)), wup_ref[...])
        w_log = -(jnp.maximum(-w_pre, 0.0) + jnp.log1p(jnp.exp(-jnp.abs(w_pre)))) - 0.5
        lw_out[...] = -jnp.exp(w_log)
        a = _sigmoid(a0_ref[z:z + 1, :] + _dot(_bf(jnp.where(sel, ad, 0.0)), aup_ref[...]))
        kz_out[...] = k * (1.0 + (a - 1.0) * ka_ref[...])
        b_out[...] = kk * a


def _rwkv_prologue(ha, a_mu, a_w0, a_w_up, a_a0, a_a_up, a_g_up, a_k_k, a_k_a, a_r_k, seq_len, tm=256, tw=512):
    m = ha.shape[0]
    ncb = A_WIDTH // tw
    lora_block = 3 * A_WIDTH // A_LORA_IN
    in_specs = []
    for s in range(3):
        in_specs += _halo_specs(tm, tw, lambda j, s=s: s * ncb + j, m)
    in_specs += _halo_specs(tm, A_LORA_IN, lambda j: lora_block, m)
    for s in range(3):
        in_specs.append(pl.BlockSpec((2, tw), lambda i, j, s=s: (0, s * ncb + j)))
    in_specs.append(pl.BlockSpec((2, A_LORA_IN), lambda i, j: (0, lora_block)))
    vec2 = pl.BlockSpec((2, tw), lambda i, j: (0, j))
    up = pl.BlockSpec((LANES, tw), lambda i, j: (0, j))
    vec1 = pl.BlockSpec((1, tw), lambda i, j: (0, j))
    in_specs += [vec2, vec2, up, up, up, vec1, vec1, vec1]
    out_spec = pl.BlockSpec((tm, tw), lambda i, j: (i, j))
    out = jax.ShapeDtypeStruct((m, A_WIDTH), F32)
    return pl.pallas_call(
        functools.partial(_rwkv_pre_body, tiles_per_seq=seq_len // tm),
        grid=(m // tm, ncb),
        in_specs=in_specs,
        out_specs=[out_spec] * 11,
        out_shape=[out] * 11,
        compiler_params=_params(("parallel", "arbitrary")),
        name="rwkv_prologue",
    )(ha, ha, ha, ha, ha, ha, ha, ha, ha, ha, ha, ha, a_mu, a_mu, a_mu, a_mu, a_w0, a_a0,
      a_w_up.reshape(LANES, A_WIDTH).astype(BF16), a_a_up.reshape(LANES, A_WIDTH).astype(BF16), a_g_up.astype(BF16),
      a_k_k.reshape(1, A_WIDTH), a_k_a.reshape(1, A_WIDTH), a_r_k.reshape(1, A_WIDTH))


def _rwkv_chunk_terms(insts, *, hp):
    c, w = insts[0][1].shape
    rows = c * hp
    n = len(insts)
    revs = [inst[0] for inst in insts]
    ti = lax.broadcasted_iota(jnp.int32, (c, c), 0)
    si = lax.broadcasted_iota(jnp.int32, (c, c), 1)
    tri = {False: jnp.where(si <= ti, 1.0, 0.0).astype(BF16), True: jnp.where(si >= ti, 1.0, 0.0).astype(BF16)}
    rt = lax.broadcasted_iota(jnp.int32, (rows, rows), 0)
    cs = lax.broadcasted_iota(jnp.int32, (rows, rows), 1)
    eye = jnp.where(rt == cs, 1.0, 0.0)
    rt = rt & (c - 1)
    cs = cs & (c - 1)
    strict = {False: cs < rt, True: cs > rt}
    incl = {False: cs <= rt, True: cs >= rt}
    lane_head = lax.broadcasted_iota(jnp.int32, (c, w), 1) >> 6
    ew = lax.broadcasted_iota(jnp.int32, (w, w), 0) == lax.broadcasted_iota(jnp.int32, (w, w), 1)

    def stack_f32(x):
        return jnp.concatenate([jnp.where(lane_head == h, x, 0.0) for h in range(hp)], axis=0)

    def stack(x):
        return _bf(stack_f32(x))

    cums = []
    for rev, _, _, _, lw, _, _ in insts:
        hi = lw.astype(BF16)
        mid = (lw - hi.astype(F32)).astype(BF16)
        lo = (lw - hi.astype(F32) - mid.astype(F32)).astype(BF16)
        cums.append(_dot(tri[rev], hi) + _dot(tri[rev], mid) + _dot(tri[rev], lo))
    tots = [cum[0:1, :] if rev else cum[c - 1:c, :] for rev, cum in zip(revs, cums)]

    xs, rs_f, bs, ks, bh, kh, vs = [], [], [], [], [], [], []
    for (rev, r, v, kk, lw, kz, b), cum, tot in zip(insts, cums, tots):
        e_neg = jnp.exp(-cum)
        e_rest = jnp.exp(tot - cum)
        xs.append(stack(kk * jnp.exp(cum - lw)))
        rs_f.append(stack_f32(r * jnp.exp(cum)))
        bs.append(stack(b * e_neg))
        ks.append(stack(kz * e_neg))
        bh.append(stack(b * e_rest))
        kh.append(stack(kz * e_rest))
        vs.append(stack(v))

    grams = [_dot_nt(jnp.concatenate([xs[i], _bf(rs_f[i])], axis=0), jnp.concatenate([bs[i], ks[i]], axis=0))
             for i in range(n)]
    l_b = [jnp.where(strict[revs[i]], grams[i][:rows, :rows], 0.0) for i in range(n)]
    lm = [jnp.concatenate([_bf(jnp.where(strict[revs[i]], grams[i][:rows, rows:], 0.0)),
                           _bf(jnp.where(incl[revs[i]], grams[i][rows:, rows:], 0.0))], axis=0) for i in range(n)]
    m_b = [_bf(jnp.where(incl[revs[i]], grams[i][rows:, :rows], 0.0)) for i in range(n)]
    lm_v = [_dot(lm[i], vs[i]) for i in range(n)]
    kt_v = [_dot_tn(kh[i], vs[i]) for i in range(n)]

    t_inv = [eye - l_b[i] for i in range(n)]
    l_pow = l_b
    for _ in range(int(math.log2(c)) - 1):
        lp = [_bf(x) for x in l_pow]
        l_pow = [_dot(x, x) for x in lp]
        t_inv = [t_inv[i] + _dot(_bf(t_inv[i]), _bf(l_pow[i])) for i in range(n)]

    gh = [_bf(_dot(_bf(t_inv[i]), jnp.concatenate([xs[i], _bf(lm_v[i][:rows])], axis=1))) for i in range(n)]
    bt_gh = [_dot_tn(bh[i], gh[i]) for i in range(n)]
    mb_gh = [_dot(m_b[i], gh[i]) for i in range(n)]
    out = []
    for i in range(n):
        phi = jnp.where(ew, jnp.exp(tots[i]), 0.0) - bt_gh[i][:, :w]
        psi = kt_v[i] - bt_gh[i][:, w:]
        q_p = rs_f[i] - mb_gh[i][:, :w]
        y0 = lm_v[i][rows:] - mb_gh[i][:, w:]
        out.append((_bf(q_p), y0, _bf(phi), psi))
    return out


def _rwkv_scan_body(rf_ref, vf_ref, kkf_ref, lwf_ref, kzf_ref, bf_ref, rb_ref, vb_ref, kkb_ref, lwb_ref, kzb_ref,
                    bb_ref, yf_ref, yb_ref, state_ref, *, hp, gp, cps, chunk):
    c = chunk
    w = A_HEAD * hp

    @pl.when(pl.program_id(2) == 0)
    def _():
        state_ref[...] = jnp.zeros_like(state_ref)

    dirs = ((False, (rf_ref, vf_ref, kkf_ref, lwf_ref, kzf_ref, bf_ref), yf_ref),
            (True, (rb_ref, vb_ref, kkb_ref, lwb_ref, kzb_ref, bb_ref), yb_ref))
    keys = [(d, gi, ci) for d in range(2) for gi in range(gp) for ci in range(cps)]
    insts = [(dirs[d][0], *[x[ci * c:(ci + 1) * c, gi * w:(gi + 1) * w] for x in dirs[d][1]]) for d, gi, ci in keys]
    terms = dict(zip(keys, _rwkv_chunk_terms(insts, hp=hp)))
    chains = [(d, gi) for d in range(2) for gi in range(gp)]
    states = {k: state_ref[k[0], k[1]] for k in chains}
    for step in range(cps):
        for d, gi in chains:
            ci = cps - 1 - step if dirs[d][0] else step
            q_p, y0, phi, psi = terms[(d, gi, ci)]
            a_b = _bf(states[(d, gi)])
            y_st = _dot(q_p, a_b) + y0
            states[(d, gi)] = _dot(phi, a_b) + psi
            y = y_st[0:c, :]
            for h in range(1, hp):
                y = y + y_st[h * c:(h + 1) * c, :]
            dirs[d][2][ci * c:(ci + 1) * c, gi * w:(gi + 1) * w] = y
    for d, gi in chains:
        state_ref[d, gi] = states[(d, gi)]


def _rwkv_scan(r, v, kk, lw0, kz0, b0, lw1, kz1, b1, batch, seq_len, hp=RWKV_HEADS_PER_GROUP,
               gp=RWKV_GROUPS_PER_STEP, cps=RWKV_CHUNKS_PER_STEP, chunk=RWKV_CHUNK):
    m = r.shape[0]
    w = A_HEAD * hp
    rows = chunk * cps
    ns = seq_len // rows
    fwd = pl.BlockSpec((rows, w * gp), lambda bi, g, n: (bi * ns + n, g))
    bwd = pl.BlockSpec((rows, w * gp), lambda bi, g, n: (bi * ns + ns - 1 - n, g))
    out = jax.ShapeDtypeStruct((m, A_WIDTH), F32)
    return pl.pallas_call(
        functools.partial(_rwkv_scan_body, hp=hp, gp=gp, cps=cps, chunk=chunk),
        grid=(batch, A_WIDTH // (w * gp), ns),
        in_specs=[fwd] * 6 + [bwd] * 6,
        out_specs=[fwd, bwd],
        out_shape=[out, out],
        scratch_shapes=[pltpu.VMEM((2, gp, w, w), F32)],
        compiler_params=_params(("parallel", "parallel", "arbitrary")),
        name="rwkv_scan",
    )(r, v, kk, lw0, kz0, b0, r, v, kk, lw1, kz1, b1)


def _rwkv_post_body(yf_ref, yb_ref, bonus_ref, g_ref, lnw_ref, lnb_ref, o_ref):
    y = yf_ref[...] + yb_ref[...]
    inv_n = 1.0 / A_HEAD
    mu = _head_sum64(y) * inv_n
    d = y - mu
    var = _head_sum64(d * d) * inv_n
    yn = d * lax.rsqrt(var + A_GN_EPS) * lnw_ref[...] + lnb_ref[...]
    o_ref[...] = ((yn + bonus_ref[...]) * g_ref[...]).astype(o_ref.dtype)


def _rwkv_epilogue(yf, yb, bonus, g, ln_w, ln_b, tm=256, tw=512):
    m = yf.shape[0]
    spec = pl.BlockSpec((tm, tw), lambda i, j: (i, j))
    vec = pl.BlockSpec((1, tw), lambda i, j: (0, j))
    return pl.pallas_call(
        _rwkv_post_body,
        grid=(m // tm, A_WIDTH // tw),
        in_specs=[spec, spec, spec, spec, vec, vec],
        out_specs=spec,
        out_shape=jax.ShapeDtypeStruct((m, A_WIDTH), BF16),
        compiler_params=_params(("parallel", "parallel")),
        name="rwkv_epilogue",
    )(yf, yb, bonus, g, ln_w.reshape(1, A_WIDTH), ln_b.reshape(1, A_WIDTH))


def _attn_body(q_ref, k_ref, v_ref, o_ref):
    s = _dot_nt(q_ref[...], k_ref[...])
    p = jnp.exp(s - jnp.max(s, axis=-1, keepdims=True))
    l = jnp.sum(p, axis=-1, keepdims=True)
    o_ref[...] = (_dot(_bf(p), v_ref[...]) / l).astype(o_ref.dtype)


def _attention(q, k, v, batch, seq_len, tq):
    m = q.shape[1]
    nq = seq_len // tq
    return pl.pallas_call(
        _attn_body,
        grid=(B_HEADS, batch, nq),
        in_specs=[pl.BlockSpec((None, tq, B_QK_PAD), lambda h, bi, qi: (h, bi * nq + qi, 0)),
                  pl.BlockSpec((None, seq_len, B_QK_PAD), lambda h, bi, qi: (h, bi, 0)),
                  pl.BlockSpec((None, seq_len, B_VDIM), lambda h, bi, qi: (h, bi, 0))],
        out_specs=pl.BlockSpec((tq, B_VDIM), lambda h, bi, qi: (bi * nq + qi, h)),
        out_shape=jax.ShapeDtypeStruct((m, B_HEADS * B_VDIM), BF16),
        compiler_params=_params(("parallel", "parallel", "arbitrary")),
        name="mla_attention",
    )(q, k, v)


def _ret_body(lg_ref, q_ref, k_ref, v_ref, *rest, rev, final, chunk):
    if final:
        other_ref, gate_ref, o_ref, state_ref = rest
    else:
        o_ref, state_ref = rest
    c = chunk

    @pl.when(pl.program_id(2) == 0)
    def _():
        state_ref[...] = jnp.zeros_like(state_ref)

    lg = lg_ref[0:1, 0:1]
    q = q_ref[...]
    k = k_ref[...]
    v = v_ref[...]
    idx = lax.broadcasted_iota(jnp.int32, (c, 1), 0).astype(F32)
    if rev:
        q_pow, k_pow = c - idx, idx
    else:
        q_pow, k_pow = idx + 1.0, c - 1.0 - idx
    qd = _bf(q.astype(F32) * jnp.exp(lg * q_pow))
    kd = _bf(k.astype(F32) * jnp.exp(lg * k_pow))
    state = state_ref[...]
    cross = _dot(qd, _bf(state))
    state_ref[...] = state * jnp.exp(lg * float(c)) + _dot_tn(kd, v)
    if final:
        ti = lax.broadcasted_iota(jnp.int32, (c, c), 0)
        si = lax.broadcasted_iota(jnp.int32, (c, c), 1)
        dmask = jnp.exp(lg * jnp.abs(ti - si).astype(F32))
        inner = _dot(_bf(_dot_nt(q, k) * dmask), v)
        o = inner + cross + other_ref[...]
        mu = jnp.mean(o, axis=-1, keepdims=True)
        d = o - mu
        var = jnp.mean(d * d, axis=-1, keepdims=True)
        o_ref[...] = (gate_ref[...].astype(F32) * (d * lax.rsqrt(var + C_GN_EPS))).astype(o_ref.dtype)
    else:
        o_ref[...] = cross


def _retention_pass(log_gamma, qk, vg, other, batch, seq_len, rev, final, chunk=RET_CHUNK):
    m = qk.shape[0]
    nc = seq_len // chunk
    if rev:
        row = lambda bi, n: bi * nc + nc - 1 - n
    else:
        row = lambda bi, n: bi * nc + n
    blk = lambda off: pl.BlockSpec((chunk, C_HEAD), lambda h, bi, n: (row(bi, n), h + off))
    in_specs = [pl.BlockSpec((None, SUBLANES, LANES), lambda h, bi, n: (h, 0, 0)), blk(0), blk(C_HEADS), blk(0)]
    args = [log_gamma, qk, qk, vg]
    if final:
        in_specs += [blk(0), blk(C_HEADS)]
        args += [other, vg]
    return pl.pallas_call(
        functools.partial(_ret_body, rev=rev, final=final, chunk=chunk),
        grid=(C_HEADS, batch, nc),
        in_specs=in_specs,
        out_specs=blk(0),
        out_shape=jax.ShapeDtypeStruct((m, C_WIDTH), BF16 if final else F32),
        scratch_shapes=[pltpu.VMEM((C_HEAD, C_HEAD), F32)],
        compiler_params=_params(("parallel", "parallel", "arbitrary")),
        name="retention_final" if final else "retention_cross",
    )(*args)


def _ffn_act_body(g_ref, gp_ref, gn_ref, val_ref, cw_ref, cb_ref, o_ref, *, tiles_per_seq):
    g = g_ref[...].astype(F32)
    gp, gn = _shifted(g, gp_ref, gn_ref, tiles_per_seq)
    conv = cw_ref[0:1, :] * gp + cw_ref[1:2, :] * g + cw_ref[2:3, :] * gn + cb_ref[...]
    o_ref[...] = (_silu(conv) * val_ref[...].astype(F32)).astype(o_ref.dtype)


def _ffn_act(gv, conv_w, conv_b, seq_len, tm=256, tw=FFN_HIDDEN // 2):
    m = gv.shape[0]
    ncb = FFN_HIDDEN // tw
    return pl.pallas_call(
        functools.partial(_ffn_act_body, tiles_per_seq=seq_len // tm),
        grid=(m // tm, ncb),
        in_specs=[*_halo_specs(tm, tw, lambda j: j, m, halo=2 * SUBLANES),
                  pl.BlockSpec((tm, tw), lambda i, j: (i, ncb + j)),
                  pl.BlockSpec((3, tw), lambda i, j: (0, j)),
                  pl.BlockSpec((1, tw), lambda i, j: (0, j))],
        out_specs=pl.BlockSpec((tm, tw), lambda i, j: (i, j)),
        out_shape=jax.ShapeDtypeStruct((m, FFN_HIDDEN), BF16),
        compiler_params=_params(("parallel", "arbitrary")),
        name="ffn_conv_gate",
    )(gv, gv, gv, gv, conv_w, conv_b.reshape(1, FFN_HIDDEN))


def _rope_tables(seq_len, dim):
    inv = ROPE_BASE ** (-jnp.arange(0, dim, 2, dtype=F32) / dim)
    ang = jnp.arange(seq_len, dtype=F32)[:, None] * inv[None, :]
    return jnp.cos(ang), jnp.sin(ang)


def _swap_halves_cols(w):
    h = w.shape[-1] // 2
    return jnp.concatenate([w[..., h:], w[..., :h]], axis=-1)


def _prep_layer(w_in, b_q_up, b_kv_up, w_branch, w_out, ffn_up, ffn_down):
    o_b = A_IN
    o_kpe = o_b + B_Q_LORA + B_KV_LORA
    o_c = o_kpe + B_ROPE
    o_g = o_c + 4 * C_WIDTH
    w_kpe = w_in[:, o_kpe:o_c]
    qh = b_q_up.reshape(B_Q_LORA, B_HEADS, B_NOPE + B_ROPE)
    q_pe = qh[..., B_NOPE:]
    w_q = jnp.concatenate([qh[..., :B_NOPE], q_pe, _swap_halves_cols(q_pe)], axis=-1)
    return dict(
        w_a=jnp.pad(w_in[:, :A_IN], ((0, 0), (0, LANES))).astype(BF16),
        w_b=w_in[:, o_b:o_kpe].astype(BF16),
        w_kpe=jnp.concatenate([w_kpe, _swap_halves_cols(w_kpe)], axis=-1).astype(BF16),
        w_cqk=w_in[:, o_c:o_c + 2 * C_WIDTH].astype(BF16),
        w_cvg=w_in[:, o_c + 2 * C_WIDTH:o_g].astype(BF16),
        w_g=w_in[:, o_g:].astype(BF16),
        w_q=jnp.transpose(w_q, (1, 0, 2)).astype(BF16),
        w_kv=jnp.transpose(b_kv_up.reshape(B_KV_LORA, B_HEADS, B_NOPE + B_VDIM), (1, 0, 2)).astype(BF16),
        w_br=w_branch.reshape(3, A_WIDTH, D_MODEL).astype(BF16),
        w_out=w_out.astype(BF16),
        ffn_up=ffn_up.astype(BF16),
        ffn_down=ffn_down.astype(BF16),
    )


def _layer(x, batch, seq_len, tabs, pw, attn_norm, a_mu, a_w0, a_w_up, a_a0, a_a_up, a_g_up, a_k_k, a_k_a, a_r_k,
           a_ln_w, a_ln_b, b_q_norm, b_kv_norm, ffn_norm, ffn_conv, ffn_conv_b):
    m = x.shape[0]
    tps = lambda tm: seq_len // tm
    xn = _rmsnorm(x, attn_norm, BF16)

    (ha,) = _matmul("in_proj_a", xn, pw["w_a"], tm=1024, tn=512, epilogue=_ep_plain, out_dtypes=[F32])
    r, v, kk, g, bonus, lw0, lw1, kz0, kz1, b0, b1 = _rwkv_prologue(
        ha, a_mu, a_w0, a_w_up, a_a0, a_a_up, a_g_up, a_k_k, a_k_a, a_r_k, seq_len)
    yf, yb = _rwkv_scan(r, v, kk, lw0, kz0, b0, lw1, kz1, b1, batch, seq_len)
    oa = _rwkv_epilogue(yf, yb, bonus, g, a_ln_w, a_ln_b)

    tm_b = 512
    vec = lambda n: pl.BlockSpec((1, n), lambda i, j: (0, 0))
    qn, kvn = _matmul("in_proj_b", xn, pw["w_b"], tm=tm_b, tn=B_Q_LORA + B_KV_LORA, epilogue=_ep_qkv_norms,
                      out_dtypes=[BF16, BF16], out_widths=[B_Q_LORA, B_KV_LORA],
                      extras=(b_q_norm.reshape(1, -1), b_kv_norm.reshape(1, -1)),
                      extra_specs=(vec(B_Q_LORA), vec(B_KV_LORA)))
    tab64 = lambda tm: pl.BlockSpec((tm, LANES), lambda i, j: (i % tps(tm), 0))
    (kpe,) = _matmul("in_proj_kpe", xn, pw["w_kpe"], tm=1024, tn=LANES, epilogue=_ep_kpe, out_dtypes=[BF16],
                     extras=(tabs["cos64"], tabs["sin64"]), extra_specs=(tab64(1024), tab64(1024)))
    (q,) = _matmul("q_up", qn, pw["w_q"], tm=1024, tn=B_QK_PAD, epilogue=_ep_q_up, out_dtypes=[BF16],
                   extras=(tabs["cos64"], tabs["sin64"]), extra_specs=(tab64(1024), tab64(1024)), b_head_major=True)
    k, vv = _matmul("kv_up", kvn, pw["w_kv"], tm=1024, tn=B_NOPE + B_VDIM, epilogue=_ep_kv_up,
                    out_dtypes=[BF16, BF16], out_widths=[B_QK_PAD, B_VDIM],
                    extras=(kpe,), extra_specs=(pl.BlockSpec((1024, LANES), lambda i, j: (i, 0)),), b_head_major=True)
    ob = _attention(q, k, vv, batch, seq_len, tq=min(512, 2**20 // seq_len))

    tab128 = pl.BlockSpec((1024, LANES), lambda i, j: (i % tps(1024), 0))
    q_tiles = C_WIDTH // 512
    (qk,) = _matmul("in_proj_c_qk", xn, pw["w_cqk"], tm=1024, tn=512,
                    epilogue=functools.partial(_ep_ret_qk, q_tiles=q_tiles), out_dtypes=[BF16],
                    extras=(tabs["cos256"], tabs["sin256"]), extra_specs=(tab128, tab128))
    (vg,) = _matmul("in_proj_c_vg", xn, pw["w_cvg"], tm=1024, tn=512,
                    epilogue=functools.partial(_ep_ret_vg, v_tiles=q_tiles), out_dtypes=[BF16])
    cross_b = _retention_pass(tabs["log_gamma"], qk, vg, None, batch, seq_len, rev=True, final=False)
    oc = _retention_pass(tabs["log_gamma"], qk, vg, cross_b, batch, seq_len, rev=False, final=True)

    (gates,) = _matmul("in_proj_gates", xn, pw["w_g"], tm=1024, tn=1024, epilogue=_ep_sigmoid, out_dtypes=[BF16])
    merged = _merge(oa, ob, oc, pw["w_br"], gates)
    res_spec = lambda tm, tn: pl.BlockSpec((tm, tn), lambda i, j: (i, j))
    (x,) = _matmul("out_proj", merged, pw["w_out"], tm=1024, tn=512, epilogue=_ep_residual, out_dtypes=[F32],
                   extras=(x,), extra_specs=(res_spec(1024, 512),))

    xn2 = _rmsnorm(x, ffn_norm, BF16)
    (gv,) = _matmul("ffn_up", xn2, pw["ffn_up"], tm=1024, tn=256, epilogue=_ep_plain, out_dtypes=[BF16])
    act = _ffn_act(gv, ffn_conv, ffn_conv_b, seq_len)
    (x,) = _matmul("ffn_down", act, pw["ffn_down"], tm=512, tn=512, epilogue=_ep_residual, out_dtypes=[F32],
                   extras=(x,), extra_specs=(res_spec(512, 512),))
    return x


def _tables(seq_len):
    c64, s64 = _rope_tables(seq_len, B_ROPE)
    z = jnp.zeros((seq_len, B_ROPE), F32)
    c256, s256 = _rope_tables(seq_len, C_HEAD)
    log_gamma = jnp.log1p(-(2.0 ** (-5.0 - jnp.arange(C_HEADS, dtype=F32))))
    return dict(
        cos64=jnp.concatenate([c64, c64, z], axis=1),
        sin64=jnp.concatenate([-s64, s64, z], axis=1),
        cos256=c256, sin256=s256,
        log_gamma=jnp.broadcast_to(log_gamma[:, None, None], (C_HEADS, SUBLANES, LANES)),
    )


def _encoder(x3, prepped, per_layer, final_norm):
    batch, seq_len, d = x3.shape
    x = x3.reshape(batch * seq_len, d)
    tabs = _tables(seq_len)
    for pw, lw in zip(prepped, per_layer):
        x = _layer(x, batch, seq_len, tabs, pw, *lw)
    return _rmsnorm(x, final_norm, F32).reshape(batch, seq_len, d)


def kernel(x_prompt, x_sample, attn_norm, w_in, a_mu, a_w0, a_w_up, a_a0, a_a_up, a_g_up, a_k_k, a_k_a, a_r_k, a_ln_w, a_ln_b, b_q_norm, b_q_up, b_kv_norm, b_kv_up, w_branch, w_out, ffn_norm, ffn_up, ffn_conv, ffn_conv_b, ffn_down, final_norm):
    depth = w_in.shape[0]
    prepped = [_prep_layer(w_in[l], b_q_up[l], b_kv_up[l], w_branch[l], w_out[l], ffn_up[l], ffn_down[l])
               for l in range(depth)]
    per_layer = [(attn_norm[l], a_mu[l], a_w0[l], a_w_up[l], a_a0[l], a_a_up[l], a_g_up[l], a_k_k[l], a_k_a[l],
                  a_r_k[l], a_ln_w[l], a_ln_b[l], b_q_norm[l], b_kv_norm[l], ffn_norm[l], ffn_conv[l], ffn_conv_b[l])
                 for l in range(depth)]
    return (_encoder(x_prompt, prepped, per_layer, final_norm), _encoder(x_sample, prepped, per_layer, final_norm))
```

```python
import functools
import math

import jax
import jax.numpy as jnp
from jax import lax
from jax.experimental import pallas as pl
from jax.experimental.pallas import tpu as pltpu

F32 = jnp.float32
BF16 = jnp.bfloat16

D_MODEL = 4096
DEPTH = 2
A_HEAD = 64
A_WIDTH = 2048
A_LORA_IN = 384
A_IN = 3 * A_WIDTH + A_LORA_IN
A_GN_EPS = 64e-5
B_NOPE = 128
B_ROPE = 64
B_VDIM = 128
B_HEADS = 16
B_Q_LORA = 1536
B_KV_LORA = 512
B_QK_PAD = 256
C_HEAD = 256
C_HEADS = 8
C_WIDTH = 2048
C_GN_EPS = 1e-5
FFN_HIDDEN = 11008
ROPE_BASE = 10000.0
NORM_EPS = 1e-6

LANES = 128
SUBLANES = 8
VMEM_LIMIT_BYTES = 56 * 2**20
MM_ROWS = 2048

RWKV_CHUNK = 64
RWKV_HEADS_PER_GROUP = 2
RWKV_GROUPS_PER_STEP = 2
RWKV_CHUNKS_PER_STEP = 2
RET_CHUNK = 256
RET_HEADS_PER_STEP = 4


def _params(sem):
    return pltpu.CompilerParams(dimension_semantics=sem, vmem_limit_bytes=VMEM_LIMIT_BYTES)


def _dot(a, b):
    return jnp.dot(a, b, preferred_element_type=F32)


def _dot_nt(a, b):
    return lax.dot_general(a, b, (((1,), (1,)), ((), ())), preferred_element_type=F32)


def _dot_tn(a, b):
    return lax.dot_general(a, b, (((0,), (0,)), ((), ())), preferred_element_type=F32)


def _bf(x):
    return x.astype(BF16)


def _split(x):
    hi = x.astype(BF16)
    lo = (x - hi.astype(F32)).astype(BF16)
    return hi, lo


def _dot_exact_rhs(a, b_bf16):
    ah, al = _split(a)
    return _dot(ah, b_bf16) + _dot(al, b_bf16)


def _sigmoid(x):
    return 1.0 / (1.0 + jnp.exp(-x))


def _silu(x):
    return x * _sigmoid(x)


def _rmsnorm_body(x_ref, g_ref, o_ref):
    x = x_ref[...]
    y = x * lax.rsqrt(jnp.mean(x * x, axis=-1, keepdims=True) + NORM_EPS)
    o_ref[...] = (y * g_ref[...]).astype(o_ref.dtype)


def _rmsnorm(x, g, out_dtype, tm=256):
    m, d = x.shape
    return pl.pallas_call(
        _rmsnorm_body,
        grid=(m // tm,),
        in_specs=[pl.BlockSpec((tm, d), lambda i: (i, 0)), pl.BlockSpec((1, d), lambda i: (0, 0))],
        out_specs=pl.BlockSpec((tm, d), lambda i: (i, 0)),
        out_shape=jax.ShapeDtypeStruct((m, d), out_dtype),
        compiler_params=_params(("parallel",)),
        name="rmsnorm",
    )(x, g.reshape(1, d))


def _mm_body(a_ref, b_ref, *rest, n_extra, epilogue):
    extras = rest[:n_extra]
    outs = rest[n_extra:]
    acc = _dot(a_ref[...], b_ref[...])
    res = epilogue(acc, *extras)
    for o_ref, r in zip(outs, res):
        o_ref[...] = r.astype(o_ref.dtype)


def _matmul(name, a, b, *, tm, tn, epilogue, out_dtypes, out_widths=None, extras=(), extra_specs=(),
            b_head_major=False):
    m, k = a.shape
    if b_head_major:
        nj = b.shape[0]
        b_spec = pl.BlockSpec((None, k, tn), lambda i, j: (j, 0, 0))
    else:
        nj = b.shape[1] // tn
        b_spec = pl.BlockSpec((k, tn), lambda i, j: (0, j))
    out_widths = out_widths or [tn] * len(out_dtypes)
    if b_head_major:
        out_specs = [pl.BlockSpec((None, tm, w), lambda i, j: (j, i, 0)) for w in out_widths]
        out_shape = [jax.ShapeDtypeStruct((nj, m, w), dt) for w, dt in zip(out_widths, out_dtypes)]
    else:
        out_specs = [pl.BlockSpec((tm, w), lambda i, j: (i, j)) for w in out_widths]
        out_shape = [jax.ShapeDtypeStruct((m, nj * w), dt) for w, dt in zip(out_widths, out_dtypes)]
    return pl.pallas_call(
        functools.partial(_mm_body, n_extra=len(extras), epilogue=epilogue),
        grid=(m // tm, nj),
        in_specs=[pl.BlockSpec((tm, k), lambda i, j: (i, 0), pipeline_mode=pl.Buffered(1)), b_spec, *extra_specs],
        out_specs=out_specs,
        out_shape=out_shape,
        compiler_params=_params(("parallel", "arbitrary")),
        name=name,
    )(a, b, *extras)


def _ep_plain(acc):
    return (acc,)


def _ep_sigmoid(acc):
    return (_sigmoid(acc),)


def _ep_residual(acc, res_ref):
    return (acc + res_ref[...],)


def _ep_qkv_norms(acc, gq_ref, gkv_ref):
    q = acc[:, :B_Q_LORA]
    kv = acc[:, B_Q_LORA:]
    qn = q * lax.rsqrt(jnp.mean(q * q, axis=-1, keepdims=True) + NORM_EPS) * gq_ref[...]
    kvn = kv * lax.rsqrt(jnp.mean(kv * kv, axis=-1, keepdims=True) + NORM_EPS) * gkv_ref[...]
    return qn, kvn


def _rope64_paired(x, cos_ref, sin_ref):
    return x * cos_ref[...] + pltpu.roll(x, 64, axis=1) * sin_ref[...]


def _ep_kpe(acc, cos_ref, sin_ref):
    return (_rope64_paired(acc, cos_ref, sin_ref),)


def _ep_q_up(acc, cos_ref, sin_ref):
    scale = (B_NOPE + B_ROPE) ** -0.5
    return (jnp.concatenate([acc[:, :B_NOPE], _rope64_paired(acc[:, B_NOPE:], cos_ref, sin_ref)], axis=1) * scale,)


def _ep_kv_up(acc, kpe_ref):
    k = jnp.concatenate([acc[:, :B_NOPE].astype(BF16), kpe_ref[...]], axis=1)
    return k, acc[:, B_NOPE:]


def _ep_ret_qk(acc, cos_ref, sin_ref, *, q_tiles):
    cos = cos_ref[...]
    sin = sin_ref[...]
    half = C_HEAD // 2
    pieces = []
    for h in range(acc.shape[1] // C_HEAD):
        x1 = acc[:, h * C_HEAD:h * C_HEAD + half]
        x2 = acc[:, h * C_HEAD + half:(h + 1) * C_HEAD]
        pieces += [x1 * cos - x2 * sin, x1 * sin + x2 * cos]
    out = jnp.concatenate(pieces, axis=1)
    scale = jnp.where(pl.program_id(1) < q_tiles, C_HEAD ** -0.5, 1.0)
    return (out * scale,)


def _ep_ret_vg(acc, *, v_tiles):
    return (jnp.where(pl.program_id(1) < v_tiles, acc, _silu(acc)),)


def _merge_body(oa_ref, ob_ref, oc_ref, w_ref, g_ref, out_ref, acc_ref):
    br = pl.program_id(2)
    g = g_ref[...].astype(F32)

    @pl.when(br == 0)
    def _():
        acc_ref[...] = g * _dot(oa_ref[...], w_ref[...])

    @pl.when(br == 1)
    def _():
        acc_ref[...] += g * _dot(ob_ref[...], w_ref[...])

    @pl.when(br == 2)
    def _():
        out_ref[...] = (acc_ref[...] + g * _dot(oc_ref[...], w_ref[...])).astype(out_ref.dtype)


def _merge(oa, ob, oc, w3, gates, tm=MM_ROWS, tn=512):
    m, kb = oa.shape
    n = w3.shape[2]
    nj = n // tn
    o_spec = pl.BlockSpec((tm, kb), lambda i, j, br: (i, 0), pipeline_mode=pl.Buffered(1))
    return pl.pallas_call(
        _merge_body,
        grid=(m // tm, nj, 3),
        in_specs=[o_spec, o_spec, o_spec,
                  pl.BlockSpec((None, kb, tn), lambda i, j, br: (br, 0, j)),
                  pl.BlockSpec((tm, tn), lambda i, j, br: (i, br * nj + j))],
        out_specs=pl.BlockSpec((tm, tn), lambda i, j, br: (i, j)),
        out_shape=jax.ShapeDtypeStruct((m, n), BF16),
        scratch_shapes=[pltpu.VMEM((tm, tn), F32)],
        compiler_params=_params(("parallel", "arbitrary", "arbitrary")),
        name="branch_merge",
    )(oa, ob, oc, w3, gates)


def _halo_specs(tm, w, col_block, n_rows, halo=SUBLANES):
    per_tile = tm // halo
    last = n_rows // halo - 1
    main = pl.BlockSpec((tm, w), lambda i, j: (i, col_block(j)))
    prev = pl.BlockSpec((halo, w), lambda i, j: (jnp.maximum(i * per_tile - 1, 0), col_block(j)))
    nxt = pl.BlockSpec((halo, w), lambda i, j: (jnp.minimum((i + 1) * per_tile, last), col_block(j)))
    return [main, prev, nxt]


def _shifted(x, prev_ref, next_ref, tiles_per_seq):
    tm = x.shape[0]
    i = pl.program_id(0)
    pos = i % tiles_per_seq
    halo = prev_ref.shape[0]
    prev_row = jnp.where(pos == 0, 0.0, prev_ref[halo - 1:halo, :].astype(F32))
    next_row = jnp.where(pos == tiles_per_seq - 1, 0.0, next_ref[0:1, :].astype(F32))
    row = lax.broadcasted_iota(jnp.int32, x.shape, 0)
    xp = jnp.where(row == 0, prev_row, pltpu.roll(x, 1, axis=0))
    xn = jnp.where(row == tm - 1, next_row, pltpu.roll(x, tm - 1, axis=0))
    return xp, xn


def _head_sum64(x):
    w = x.shape[1]
    r = lax.broadcasted_iota(jnp.int32, (w, w), 0) >> 6
    c = lax.broadcasted_iota(jnp.int32, (w, w), 1) >> 6
    ones = jnp.where(r == c, 1.0, 0.0).astype(BF16)
    return _dot_exact_rhs(x, ones)


def _rwkv_pre_body(r_ref, rp_ref, rn_ref, k_ref, kp_ref, kn_ref, v_ref, vp_ref, vn_ref, l_ref, lp_ref, ln_ref,
                   mur_ref, muk_ref, muv_ref, mul_ref, w0_ref, a0_ref, wup_ref, aup_ref, gup_ref,
                   kk_w_ref, ka_ref, rk_ref,
                   r_out, v_out, kk_out, g_out, bonus_out, lw0_out, lw1_out, kz0_out, kz1_out, b0_out, b1_out,
                   *, tiles_per_seq):
    def mix(x_ref, p_ref, n_ref, mu_ref):
        x = x_ref[...]
        xp, xn = _shifted(x, p_ref, n_ref, tiles_per_seq)
        return x + mu_ref[0:1, :] * (xp - x) + mu_ref[1:2, :] * (xn - x)

    r = mix(r_ref, rp_ref, rn_ref, mur_ref)
    k = mix(k_ref, kp_ref, kn_ref, muk_ref)
    v = mix(v_ref, vp_ref, vn_ref, muv_ref)
    lo = mix(l_ref, lp_ref, ln_ref, mul_ref)
    wd = jnp.tanh(lo[:, 0:LANES])
    ad = lo[:, LANES:2 * LANES]
    gd = _sigmoid(lo[:, 2 * LANES:3 * LANES])
    lane = lax.broadcasted_iota(jnp.int32, (1, LANES), 1)

    g_out[...] = _dot(_bf(gd), gup_ref[...])
    kk = k * kk_w_ref[...]
    kk = kk / jnp.maximum(jnp.sqrt(_head_sum64(kk * kk)), 1e-12)
    r_out[...] = r.astype(r_out.dtype)
    v_out[...] = v.astype(v_out.dtype)
    kk_out[...] = kk.astype(kk_out.dtype)
    bonus_out[...] = _head_sum64(r * k * rk_ref[...]) * v
    for z, (lw_out, kz_out, b_out) in enumerate(((lw0_out, kz0_out, b0_out), (lw1_out, kz1_out, b1_out))):
        sel = (lane >> 6) == z
        w_pre = w0_ref[z:z + 1, :] + _dot(_bf(jnp.where(sel, wd, 0.0)), wup_ref[...])
        w_log = -(jnp.maximum(-w_pre, 0.0) + jnp.log1p(jnp.exp(-jnp.abs(w_pre)))) - 0.5
        lw_out[...] = -jnp.exp(w_log)
        a = _sigmoid(a0_ref[z:z + 1, :] + _dot(_bf(jnp.where(sel, ad, 0.0)), aup_ref[...]))
        kz_out[...] = (k * (1.0 + (a - 1.0) * ka_ref[...])).astype(kz_out.dtype)
        b_out[...] = (kk * a).astype(b_out.dtype)


def _rwkv_prologue(ha, a_mu, a_w0, a_w_up, a_a0, a_a_up, a_g_up, a_k_k, a_k_a, a_r_k, seq_len, tm=256, tw=512):
    m = ha.shape[0]
    ncb = A_WIDTH // tw
    lora_block = 3 * A_WIDTH // A_LORA_IN
    in_specs = []
    for s in range(3):
        in_specs += _halo_specs(tm, tw, lambda j, s=s: s * ncb + j, m)
    in_specs += _halo_specs(tm, A_LORA_IN, lambda j: lora_block, m)
    for s in range(3):
        in_specs.append(pl.BlockSpec((2, tw), lambda i, j, s=s: (0, s * ncb + j)))
    in_specs.append(pl.BlockSpec((2, A_LORA_IN), lambda i, j: (0, lora_block)))
    vec2 = pl.BlockSpec((2, tw), lambda i, j: (0, j))
    up = pl.BlockSpec((LANES, tw), lambda i, j: (0, j))
    vec1 = pl.BlockSpec((1, tw), lambda i, j: (0, j))
    in_specs += [vec2, vec2, up, up, up, vec1, vec1, vec1]
    out_spec = pl.BlockSpec((tm, tw), lambda i, j: (i, j))
    out_dtypes = [BF16, BF16, BF16, F32, F32, F32, F32, BF16, BF16, BF16, BF16]
    return pl.pallas_call(
        functools.partial(_rwkv_pre_body, tiles_per_seq=seq_len // tm),
        grid=(m // tm, ncb),
        in_specs=in_specs,
        out_specs=[out_spec] * 11,
        out_shape=[jax.ShapeDtypeStruct((m, A_WIDTH), dt) for dt in out_dtypes],
        compiler_params=_params(("parallel", "arbitrary")),
        name="rwkv_prologue",
    )(ha, ha, ha, ha, ha, ha, ha, ha, ha, ha, ha, ha, a_mu, a_mu, a_mu, a_mu, a_w0, a_a0,
      a_w_up.reshape(LANES, A_WIDTH).astype(BF16), a_a_up.reshape(LANES, A_WIDTH).astype(BF16), a_g_up.astype(BF16),
      a_k_k.reshape(1, A_WIDTH), a_k_a.reshape(1, A_WIDTH), a_r_k.reshape(1, A_WIDTH))


def _rwkv_chunk_terms(insts, *, hp):
    c, w = insts[0][1].shape
    rows = c * hp
    n = len(insts)
    revs = [inst[0] for inst in insts]
    ti = lax.broadcasted_iota(jnp.int32, (c, c), 0)
    si = lax.broadcasted_iota(jnp.int32, (c, c), 1)
    tri = {False: jnp.where(si <= ti, 1.0, 0.0).astype(BF16), True: jnp.where(si >= ti, 1.0, 0.0).astype(BF16)}
    rt = lax.broadcasted_iota(jnp.int32, (rows, rows), 0)
    cs = lax.broadcasted_iota(jnp.int32, (rows, rows), 1)
    eye = jnp.where(rt == cs, 1.0, 0.0)
    rt = rt & (c - 1)
    cs = cs & (c - 1)
    strict = {False: cs < rt, True: cs > rt}
    incl = {False: cs <= rt, True: cs >= rt}
    lane_head = lax.broadcasted_iota(jnp.int32, (c, w), 1) >> 6
    ew = lax.broadcasted_iota(jnp.int32, (w, w), 0) == lax.broadcasted_iota(jnp.int32, (w, w), 1)

    def stack_f32(x):
        return jnp.concatenate([jnp.where(lane_head == h, x, 0.0) for h in range(hp)], axis=0)

    def stack(x):
        return _bf(stack_f32(x))

    cums = []
    for rev, _, _, _, lw, _, _ in insts:
        hi = lw.astype(BF16)
        mid = (lw - hi.astype(F32)).astype(BF16)
        lo = (lw - hi.astype(F32) - mid.astype(F32)).astype(BF16)
        parts = _dot(tri[rev], jnp.concatenate([hi, mid, lo], axis=1))
        cums.append(parts[:, :w] + parts[:, w:2 * w] + parts[:, 2 * w:])
    tots = [cum[0:1, :] if rev else cum[c - 1:c, :] for rev, cum in zip(revs, cums)]

    xs, rs_f, bs, ks, bh, kh, vs = [], [], [], [], [], [], []
    for (rev, r, v, kk, lw, kz, b), cum, tot in zip(insts, cums, tots):
        r, kk, kz, b = (x.astype(F32) for x in (r, kk, kz, b))
        e_neg = jnp.exp(-cum)
        e_rest = jnp.exp(tot - cum)
        xs.append(stack(kk * jnp.exp(cum - lw)))
        rs_f.append(stack_f32(r * jnp.exp(cum)))
        bs.append(stack(b * e_neg))
        ks.append(stack(kz * e_neg))
        bh.append(stack(b * e_rest))
        kh.append(stack(kz * e_rest))
        vs.append(stack(v))

    grams = [_dot_nt(jnp.concatenate([xs[i], _bf(rs_f[i])], axis=0), jnp.concatenate([bs[i], ks[i]], axis=0))
             for i in range(n)]
    l_b = [jnp.where(strict[revs[i]], grams[i][:rows, :rows], 0.0) for i in range(n)]
    lm = [jnp.concatenate([_bf(jnp.where(strict[revs[i]], grams[i][:rows, rows:], 0.0)),
                           _bf(jnp.where(incl[revs[i]], grams[i][rows:, rows:], 0.0))], axis=0) for i in range(n)]
    m_b = [_bf(jnp.where(incl[revs[i]], grams[i][rows:, :rows], 0.0)) for i in range(n)]
    lm_v = [_dot(lm[i], vs[i]) for i in range(n)]
    kt_v = [_dot_tn(kh[i], vs[i]) for i in range(n)]

    t_inv = [eye - l_b[i] for i in range(n)]
    lb = [_bf(x) for x in l_b]
    l_pow = [_dot(x, x) for x in lb]
    for _ in range(int(math.log2(c)) - 2):
        lp = [_bf(x) for x in l_pow]
        both = [_dot(lp[i], jnp.concatenate([_bf(t_inv[i]), lp[i]], axis=1)) for i in range(n)]
        t_inv = [t_inv[i] + both[i][:, :rows] for i in range(n)]
        l_pow = [both[i][:, rows:] for i in range(n)]
    t_inv = [t_inv[i] + _dot(_bf(l_pow[i]), _bf(t_inv[i])) for i in range(n)]

    gh = [_bf(_dot(_bf(t_inv[i]), jnp.concatenate([xs[i], _bf(lm_v[i][:rows])], axis=1))) for i in range(n)]
    bt_gh = [_dot_tn(bh[i], gh[i]) for i in range(n)]
    mb_gh = [_dot(m_b[i], gh[i]) for i in range(n)]
    out = []
    for i in range(n):
        phi = jnp.where(ew, jnp.exp(tots[i]), 0.0) - bt_gh[i][:, :w]
        psi = kt_v[i] - bt_gh[i][:, w:]
        q_p = rs_f[i] - mb_gh[i][:, :w]
        y0 = lm_v[i][rows:] - mb_gh[i][:, w:]
        out.append((_bf(q_p), y0, _bf(phi), psi))
    return out


def _rwkv_scan_body(rf_ref, vf_ref, kkf_ref, lwf_ref, kzf_ref, bf_ref, rb_ref, vb_ref, kkb_ref, lwb_ref, kzb_ref,
                    bb_ref, yf_ref, yb_ref, state_ref, *, hp, gp, cps, chunk):
    c = chunk
    w = A_HEAD * hp

    @pl.when(pl.program_id(2) == 0)
    def _():
        state_ref[...] = jnp.zeros_like(state_ref)

    dirs = ((False, (rf_ref, vf_ref, kkf_ref, lwf_ref, kzf_ref, bf_ref), yf_ref),
            (True, (rb_ref, vb_ref, kkb_ref, lwb_ref, kzb_ref, bb_ref), yb_ref))
    keys = [(d, gi, ci) for d in range(2) for gi in range(gp) for ci in range(cps)]
    insts = [(dirs[d][0], *[x[ci * c:(ci + 1) * c, gi * w:(gi + 1) * w] for x in dirs[d][1]]) for d, gi, ci in keys]
    terms = dict(zip(keys, _rwkv_chunk_terms(insts, hp=hp)))
    chains = [(d, gi) for d in range(2) for gi in range(gp)]
    states = {k: state_ref[k[0], k[1]] for k in chains}
    for step in range(cps):
        for d, gi in chains:
            ci = cps - 1 - step if dirs[d][0] else step
            q_p, y0, phi, psi = terms[(d, gi, ci)]
            a_b = _bf(states[(d, gi)])
            y_st = _dot(q_p, a_b) + y0
            states[(d, gi)] = _dot(phi, a_b) + psi
            y = y_st[0:c, :]
            for h in range(1, hp):
                y = y + y_st[h * c:(h + 1) * c, :]
            dirs[d][2][ci * c:(ci + 1) * c, gi * w:(gi + 1) * w] = y
    for d, gi in chains:
        state_ref[d, gi] = states[(d, gi)]


def _rwkv_scan(r, v, kk, lw0, kz0, b0, lw1, kz1, b1, batch, seq_len, hp=RWKV_HEADS_PER_GROUP,
               gp=RWKV_GROUPS_PER_STEP, cps=RWKV_CHUNKS_PER_STEP, chunk=RWKV_CHUNK):
    m = r.shape[0]
    w = A_HEAD * hp
    rows = chunk * cps
    ns = seq_len // rows
    fwd = pl.BlockSpec((rows, w * gp), lambda bi, g, n: (bi * ns + n, g))
    bwd = pl.BlockSpec((rows, w * gp), lambda bi, g, n: (bi * ns + ns - 1 - n, g))
    out = jax.ShapeDtypeStruct((m, A_WIDTH), F32)
    return pl.pallas_call(
        functools.partial(_rwkv_scan_body, hp=hp, gp=gp, cps=cps, chunk=chunk),
        grid=(batch, A_WIDTH // (w * gp), ns),
        in_specs=[fwd] * 6 + [bwd] * 6,
        out_specs=[fwd, bwd],
        out_shape=[out, out],
        scratch_shapes=[pltpu.VMEM((2, gp, w, w), F32)],
        compiler_params=_params(("parallel", "parallel", "arbitrary")),
        name="rwkv_scan",
    )(r, v, kk, lw0, kz0, b0, r, v, kk, lw1, kz1, b1)


def _rwkv_post_body(yf_ref, yb_ref, bonus_ref, g_ref, lnw_ref, lnb_ref, o_ref):
    y = yf_ref[...] + yb_ref[...]
    inv_n = 1.0 / A_HEAD
    mu = _head_sum64(y) * inv_n
    d = y - mu
    var = _head_sum64(d * d) * inv_n
    yn = d * lax.rsqrt(var + A_GN_EPS) * lnw_ref[...] + lnb_ref[...]
    o_ref[...] = ((yn + bonus_ref[...]) * g_ref[...]).astype(o_ref.dtype)


def _rwkv_epilogue(yf, yb, bonus, g, ln_w, ln_b, tm=256, tw=512):
    m = yf.shape[0]
    spec = pl.BlockSpec((tm, tw), lambda i, j: (i, j))
    vec = pl.BlockSpec((1, tw), lambda i, j: (0, j))
    return pl.pallas_call(
        _rwkv_post_body,
        grid=(m // tm, A_WIDTH // tw),
        in_specs=[spec, spec, spec, spec, vec, vec],
        out_specs=spec,
        out_shape=jax.ShapeDtypeStruct((m, A_WIDTH), BF16),
        compiler_params=_params(("parallel", "parallel")),
        name="rwkv_epilogue",
    )(yf, yb, bonus, g, ln_w.reshape(1, A_WIDTH), ln_b.reshape(1, A_WIDTH))


ATTN_SUBTILES = 2


def _attn_body(q_ref, k_ref, v_ref, o_ref):
    sub = q_ref.shape[0] // ATTN_SUBTILES
    k = k_ref[...]
    v = v_ref[...]
    s = [_dot_nt(q_ref[i * sub:(i + 1) * sub, :], k) for i in range(ATTN_SUBTILES)]
    for i in range(ATTN_SUBTILES):
        p = jnp.exp(s[i] - jnp.max(s[i], axis=-1, keepdims=True))
        l = jnp.sum(p, axis=-1, keepdims=True)
        o_ref[i * sub:(i + 1) * sub, :] = (_dot(_bf(p), v) / l).astype(o_ref.dtype)


def _attention(q, k, v, batch, seq_len, tq):
    m = q.shape[1]
    nq = seq_len // tq
    return pl.pallas_call(
        _attn_body,
        grid=(B_HEADS, batch, nq),
        in_specs=[pl.BlockSpec((None, tq, B_QK_PAD), lambda h, bi, qi: (h, bi * nq + qi, 0)),
                  pl.BlockSpec((None, seq_len, B_QK_PAD), lambda h, bi, qi: (h, bi, 0)),
                  pl.BlockSpec((None, seq_len, B_VDIM), lambda h, bi, qi: (h, bi, 0))],
        out_specs=pl.BlockSpec((tq, B_VDIM), lambda h, bi, qi: (bi * nq + qi, h)),
        out_shape=jax.ShapeDtypeStruct((m, B_HEADS * B_VDIM), BF16),
        compiler_params=_params(("parallel", "parallel", "arbitrary")),
        name="mla_attention",
    )(q, k, v)


def _ret_body(lg_ref, q_ref, k_ref, v_ref, *rest, rev, final, chunk, hps):
    if final:
        other_ref, gate_ref, o_ref, state_ref = rest
    else:
        o_ref, state_ref = rest
    c = chunk
    heads = range(hps)
    cols = [slice(h * C_HEAD, (h + 1) * C_HEAD) for h in heads]

    @pl.when(pl.program_id(2) == 0)
    def _():
        state_ref[...] = jnp.zeros_like(state_ref)

    lg = [lg_ref[h, 0:1, 0:1] for h in heads]
    q = [q_ref[:, cols[h]] for h in heads]
    k = [k_ref[:, cols[h]] for h in heads]
    v = [v_ref[:, cols[h]] for h in heads]
    idx = lax.broadcasted_iota(jnp.int32, (c, 1), 0).astype(F32)
    if rev:
        q_pow, k_pow = c - idx, idx
    else:
        q_pow, k_pow = idx + 1.0, c - 1.0 - idx
    qd = [_bf(q[h].astype(F32) * jnp.exp(lg[h] * q_pow)) for h in heads]
    kd = [_bf(k[h].astype(F32) * jnp.exp(lg[h] * k_pow)) for h in heads]
    state = [state_ref[h] for h in heads]
    cross = [_dot(qd[h], _bf(state[h])) for h in heads]
    kv = [_dot_tn(kd[h], v[h]) for h in heads]
    for h in heads:
        state_ref[h] = state[h] * jnp.exp(lg[h] * float(c)) + kv[h]
    if final:
        ti = lax.broadcasted_iota(jnp.int32, (c, c), 0)
        si = lax.broadcasted_iota(jnp.int32, (c, c), 1)
        dist = jnp.abs(ti - si).astype(F32)
        s = [_dot_nt(q[h], k[h]) for h in heads]
        inner = [_dot(_bf(s[h] * jnp.exp(lg[h] * dist)), v[h]) for h in heads]
        for h in heads:
            o = inner[h] + cross[h] + other_ref[:, cols[h]]
            mu = jnp.mean(o, axis=-1, keepdims=True)
            d = o - mu
            var = jnp.mean(d * d, axis=-1, keepdims=True)
            o_ref[:, cols[h]] = (gate_ref[:, cols[h]].astype(F32) * (d * lax.rsqrt(var + C_GN_EPS))).astype(o_ref.dtype)
    else:
        for h in heads:
            o_ref[:, cols[h]] = cross[h]


def _retention_pass(log_gamma, qk, vg, other, batch, seq_len, rev, final, chunk=RET_CHUNK, hps=RET_HEADS_PER_STEP):
    m = qk.shape[0]
    nc = seq_len // chunk
    ng = C_HEADS // hps
    if rev:
        row = lambda bi, n: bi * nc + nc - 1 - n
    else:
        row = lambda bi, n: bi * nc + n
    blk = lambda off: pl.BlockSpec((chunk, C_HEAD * hps), lambda g, bi, n: (row(bi, n), g + off))
    in_specs = [pl.BlockSpec((hps, SUBLANES, LANES), lambda g, bi, n: (g, 0, 0)), blk(0), blk(ng), blk(0)]
    args = [log_gamma, qk, qk, vg]
    if final:
        in_specs += [blk(0), blk(ng)]
        args += [other, vg]
    return pl.pallas_call(
        functools.partial(_ret_body, rev=rev, final=final, chunk=chunk, hps=hps),
        grid=(ng, batch, nc),
        in_specs=in_specs,
        out_specs=blk(0),
        out_shape=jax.ShapeDtypeStruct((m, C_WIDTH), BF16 if final else F32),
        scratch_shapes=[pltpu.VMEM((hps, C_HEAD, C_HEAD), F32)],
        compiler_params=_params(("parallel", "parallel", "arbitrary")),
        name="retention_final" if final else "retention_cross",
    )(*args)


def _ffn_act_body(g_ref, gp_ref, gn_ref, val_ref, cw_ref, cb_ref, o_ref, *, tiles_per_seq):
    g = g_ref[...].astype(F32)
    gp, gn = _shifted(g, gp_ref, gn_ref, tiles_per_seq)
    conv = cw_ref[0:1, :] * gp + cw_ref[1:2, :] * g + cw_ref[2:3, :] * gn + cb_ref[...]
    o_ref[...] = (_silu(conv) * val_ref[...].astype(F32)).astype(o_ref.dtype)


def _ffn_act(gv, conv_w, conv_b, seq_len, tm=256, tw=FFN_HIDDEN // 2):
    m = gv.shape[0]
    ncb = FFN_HIDDEN // tw
    return pl.pallas_call(
        functools.partial(_ffn_act_body, tiles_per_seq=seq_len // tm),
        grid=(m // tm, ncb),
        in_specs=[*_halo_specs(tm, tw, lambda j: j, m, halo=2 * SUBLANES),
                  pl.BlockSpec((tm, tw), lambda i, j: (i, ncb + j)),
                  pl.BlockSpec((3, tw), lambda i, j: (0, j)),
                  pl.BlockSpec((1, tw), lambda i, j: (0, j))],
        out_specs=pl.BlockSpec((tm, tw), lambda i, j: (i, j)),
        out_shape=jax.ShapeDtypeStruct((m, FFN_HIDDEN), BF16),
        compiler_params=_params(("parallel", "arbitrary")),
        name="ffn_conv_gate",
    )(gv, gv, gv, gv, conv_w, conv_b.reshape(1, FFN_HIDDEN))


def _rope_tables(seq_len, dim):
    inv = ROPE_BASE ** (-jnp.arange(0, dim, 2, dtype=F32) / dim)
    ang = jnp.arange(seq_len, dtype=F32)[:, None] * inv[None, :]
    return jnp.cos(ang), jnp.sin(ang)


def _swap_halves_cols(w):
    h = w.shape[-1] // 2
    return jnp.concatenate([w[..., h:], w[..., :h]], axis=-1)


def _prep_layer(w_in, b_q_up, b_kv_up, w_branch, w_out, ffn_up, ffn_down):
    o_b = A_IN
    o_kpe = o_b + B_Q_LORA + B_KV_LORA
    o_c = o_kpe + B_ROPE
    o_g = o_c + 4 * C_WIDTH
    w_kpe = w_in[:, o_kpe:o_c]
    qh = b_q_up.reshape(B_Q_LORA, B_HEADS, B_NOPE + B_ROPE)
    q_pe = qh[..., B_NOPE:]
    w_q = jnp.concatenate([qh[..., :B_NOPE], q_pe, _swap_halves_cols(q_pe)], axis=-1)
    return dict(
        w_a=jnp.pad(w_in[:, :A_IN], ((0, 0), (0, LANES))).astype(BF16),
        w_b=w_in[:, o_b:o_kpe].astype(BF16),
        w_kpe=jnp.concatenate([w_kpe, _swap_halves_cols(w_kpe)], axis=-1).astype(BF16),
        w_cqk=w_in[:, o_c:o_c + 2 * C_WIDTH].astype(BF16),
        w_cvg=w_in[:, o_c + 2 * C_WIDTH:o_g].astype(BF16),
        w_g=w_in[:, o_g:].astype(BF16),
        w_q=jnp.transpose(w_q, (1, 0, 2)).astype(BF16),
        w_kv=jnp.transpose(b_kv_up.reshape(B_KV_LORA, B_HEADS, B_NOPE + B_VDIM), (1, 0, 2)).astype(BF16),
        w_br=w_branch.reshape(3, A_WIDTH, D_MODEL).astype(BF16),
        w_out=w_out.astype(BF16),
        ffn_up=ffn_up.astype(BF16),
        ffn_down=ffn_down.astype(BF16),
    )


def _layer(x, batch, seq_len, tabs, pw, attn_norm, a_mu, a_w0, a_w_up, a_a0, a_a_up, a_g_up, a_k_k, a_k_a, a_r_k,
           a_ln_w, a_ln_b, b_q_norm, b_kv_norm, ffn_norm, ffn_conv, ffn_conv_b):
    m = x.shape[0]
    tps = lambda tm: seq_len // tm
    xn = _rmsnorm(x, attn_norm, BF16)

    (ha,) = _matmul("in_proj_a", xn, pw["w_a"], tm=MM_ROWS, tn=512, epilogue=_ep_plain, out_dtypes=[F32])
    r, v, kk, g, bonus, lw0, lw1, kz0, kz1, b0, b1 = _rwkv_prologue(
        ha, a_mu, a_w0, a_w_up, a_a0, a_a_up, a_g_up, a_k_k, a_k_a, a_r_k, seq_len)
    yf, yb = _rwkv_scan(r, v, kk, lw0, kz0, b0, lw1, kz1, b1, batch, seq_len)
    oa = _rwkv_epilogue(yf, yb, bonus, g, a_ln_w, a_ln_b)

    tm_b = 512
    vec = lambda n: pl.BlockSpec((1, n), lambda i, j: (0, 0))
    qn, kvn = _matmul("in_proj_b", xn, pw["w_b"], tm=tm_b, tn=B_Q_LORA + B_KV_LORA, epilogue=_ep_qkv_norms,
                      out_dtypes=[BF16, BF16], out_widths=[B_Q_LORA, B_KV_LORA],
                      extras=(b_q_norm.reshape(1, -1), b_kv_norm.reshape(1, -1)),
                      extra_specs=(vec(B_Q_LORA), vec(B_KV_LORA)))
    tab64 = lambda tm: pl.BlockSpec((tm, LANES), lambda i, j: (i % tps(tm), 0))
    (kpe,) = _matmul("in_proj_kpe", xn, pw["w_kpe"], tm=MM_ROWS, tn=LANES, epilogue=_ep_kpe, out_dtypes=[BF16],
                     extras=(tabs["cos64"], tabs["sin64"]), extra_specs=(tab64(MM_ROWS), tab64(MM_ROWS)))
    (q,) = _matmul("q_up", qn, pw["w_q"], tm=MM_ROWS, tn=B_QK_PAD, epilogue=_ep_q_up, out_dtypes=[BF16],
                   extras=(tabs["cos64"], tabs["sin64"]), extra_specs=(tab64(MM_ROWS), tab64(MM_ROWS)),
                   b_head_major=True)
    k, vv = _matmul("kv_up", kvn, pw["w_kv"], tm=MM_ROWS, tn=B_NOPE + B_VDIM, epilogue=_ep_kv_up,
                    out_dtypes=[BF16, BF16], out_widths=[B_QK_PAD, B_VDIM],
                    extras=(kpe,), extra_specs=(pl.BlockSpec((MM_ROWS, LANES), lambda i, j: (i, 0)),),
                    b_head_major=True)
    ob = _attention(q, k, vv, batch, seq_len, tq=2**21 // seq_len)

    tab128 = pl.BlockSpec((MM_ROWS, LANES), lambda i, j: (i % tps(MM_ROWS), 0))
    q_tiles = C_WIDTH // 512
    (qk,) = _matmul("in_proj_c_qk", xn, pw["w_cqk"], tm=MM_ROWS, tn=512,
                    epilogue=functools.partial(_ep_ret_qk, q_tiles=q_tiles), out_dtypes=[BF16],
                    extras=(tabs["cos256"], tabs["sin256"]), extra_specs=(tab128, tab128))
    (vg,) = _matmul("in_proj_c_vg", xn, pw["w_cvg"], tm=MM_ROWS, tn=512,
                    epilogue=functools.partial(_ep_ret_vg, v_tiles=q_tiles), out_dtypes=[BF16])
    cross_b = _retention_pass(tabs["log_gamma"], qk, vg, None, batch, seq_len, rev=True, final=False)
    oc = _retention_pass(tabs["log_gamma"], qk, vg, cross_b, batch, seq_len, rev=False, final=True)

    (gates,) = _matmul("in_proj_gates", xn, pw["w_g"], tm=MM_ROWS, tn=512, epilogue=_ep_sigmoid, out_dtypes=[BF16])
    merged = _merge(oa, ob, oc, pw["w_br"], gates)
    res_spec = lambda tm, tn: pl.BlockSpec((tm, tn), lambda i, j: (i, j))
    (x,) = _matmul("out_proj", merged, pw["w_out"], tm=MM_ROWS, tn=512, epilogue=_ep_residual, out_dtypes=[F32],
                   extras=(x,), extra_specs=(res_spec(MM_ROWS, 512),))

    xn2 = _rmsnorm(x, ffn_norm, BF16)
    (gv,) = _matmul("ffn_up", xn2, pw["ffn_up"], tm=MM_ROWS, tn=512, epilogue=_ep_plain, out_dtypes=[BF16])
    act = _ffn_act(gv, ffn_conv, ffn_conv_b, seq_len)
    (x,) = _matmul("ffn_down", act, pw["ffn_down"], tm=1024, tn=256, epilogue=_ep_residual, out_dtypes=[F32],
                   extras=(x,), extra_specs=(res_spec(1024, 256),))
    return x


def _tables(seq_len):
    c64, s64 = _rope_tables(seq_len, B_ROPE)
    z = jnp.zeros((seq_len, B_ROPE), F32)
    c256, s256 = _rope_tables(seq_len, C_HEAD)
    log_gamma = jnp.log1p(-(2.0 ** (-5.0 - jnp.arange(C_HEADS, dtype=F32))))
    return dict(
        cos64=jnp.concatenate([c64, c64, z], axis=1),
        sin64=jnp.concatenate([-s64, s64, z], axis=1),
        cos256=c256, sin256=s256,
        log_gamma=jnp.broadcast_to(log_gamma[:, None, None], (C_HEADS, SUBLANES, LANES)),
    )


def _encoder(x3, prepped, per_layer, final_norm):
    batch, seq_len, d = x3.shape
    x = x3.reshape(batch * seq_len, d)
    tabs = _tables(seq_len)
    for pw, lw in zip(prepped, per_layer):
        x = _layer(x, batch, seq_len, tabs, pw, *lw)
    return _rmsnorm(x, final_norm, F32).reshape(batch, seq_len, d)


def kernel(x_prompt, x_sample, attn_norm, w_in, a_mu, a_w0, a_w_up, a_a0, a_a_up, a_g_up, a_k_k, a_k_a, a_r_k, a_ln_w, a_ln_b, b_q_norm, b_q_up, b_kv_norm, b_kv_up, w_branch, w_out, ffn_norm, ffn_up, ffn_conv, ffn_conv_b, ffn_down, final_norm):
    depth = w_in.shape[0]
    prepped = [_prep_layer(w_in[l], b_q_up[l], b_kv_up[l], w_branch[l], w_out[l], ffn_up[l], ffn_down[l])
               for l in range(depth)]
    per_layer = [(attn_norm[l], a_mu[l], a_w0[l], a_w_up[l], a_a0[l], a_a_up[l], a_g_up[l], a_k_k[l], a_k_a[l],
                  a_r_k[l], a_ln_w[l], a_ln_b[l], b_q_norm[l], b_kv_norm[l], ffn_norm[l], ffn_conv[l], ffn_conv_b[l])
                 for l in range(depth)]
    return (_encoder(x_prompt, prepped, per_layer, final_norm), _encoder(x_sample, prepped, per_layer, final_norm))
```

```python
import functools
import math

import jax
import jax.numpy as jnp
from jax import lax
from jax.experimental import pallas as pl
from jax.experimental.pallas import tpu as pltpu

F32 = jnp.float32
BF16 = jnp.bfloat16

D_MODEL = 4096
DEPTH = 2
A_HEAD = 64
A_WIDTH = 2048
A_LORA_IN = 384
A_IN = 3 * A_WIDTH + A_LORA_IN
A_GN_EPS = 64e-5
B_NOPE = 128
B_ROPE = 64
B_VDIM = 128
B_HEADS = 16
B_Q_LORA = 1536
B_KV_LORA = 512
B_QK_PAD = 256
C_HEAD = 256
C_HEADS = 8
C_WIDTH = 2048
C_GN_EPS = 1e-5
FFN_HIDDEN = 11008
ROPE_BASE = 10000.0
NORM_EPS = 1e-6

LANES = 128
SUBLANES = 8
VMEM_LIMIT_BYTES = 56 * 2**20
MM_ROWS = 1024

RWKV_CHUNK = 64
RWKV_HEADS_PER_GROUP = 2
RWKV_GROUPS_PER_STEP = 2
RWKV_CHUNKS_PER_STEP = 2
RET_CHUNK = 256
RET_HEADS_PER_STEP = 4


def _params(sem):
    return pltpu.CompilerParams(dimension_semantics=sem, vmem_limit_bytes=VMEM_LIMIT_BYTES)


def _dot(a, b):
    return jnp.dot(a, b, preferred_element_type=F32)


def _dot_nt(a, b):
    return lax.dot_general(a, b, (((1,), (1,)), ((), ())), preferred_element_type=F32)


def _dot_tn(a, b):
    return lax.dot_general(a, b, (((0,), (0,)), ((), ())), preferred_element_type=F32)


def _bf(x):
    return x.astype(BF16)


def _sigmoid(x):
    return 1.0 / (1.0 + jnp.exp(-x))


def _silu(x):
    return x * _sigmoid(x)


def _rmsnorm_body(x_ref, g_ref, o_ref):
    x = x_ref[...]
    y = x * lax.rsqrt(jnp.mean(x * x, axis=-1, keepdims=True) + NORM_EPS)
    o_ref[...] = (y * g_ref[...]).astype(o_ref.dtype)


def _rmsnorm(x, g, out_dtype, tm=256):
    m, d = x.shape
    return pl.pallas_call(
        _rmsnorm_body,
        grid=(m // tm,),
        in_specs=[pl.BlockSpec((tm, d), lambda i: (i, 0)), pl.BlockSpec((1, d), lambda i: (0, 0))],
        out_specs=pl.BlockSpec((tm, d), lambda i: (i, 0)),
        out_shape=jax.ShapeDtypeStruct((m, d), out_dtype),
        compiler_params=_params(("parallel",)),
        name="rmsnorm",
    )(x, g.reshape(1, d))


def _mm_body(a_ref, b_ref, *rest, n_extra, epilogue):
    extras = rest[:n_extra]
    outs = rest[n_extra:]
    acc = _dot(a_ref[...], b_ref[...])
    res = epilogue(acc, *extras)
    for o_ref, r in zip(outs, res):
        o_ref[...] = r.astype(o_ref.dtype)


def _matmul(name, a, b, *, tm, tn, epilogue, out_dtypes, out_widths=None, extras=(), extra_specs=(),
            b_head_major=False):
    m, k = a.shape
    if b_head_major:
        nj = b.shape[0]
        b_spec = pl.BlockSpec((None, k, tn), lambda i, j: (j, 0, 0))
    else:
        nj = b.shape[1] // tn
        b_spec = pl.BlockSpec((k, tn), lambda i, j: (0, j))
    out_widths = out_widths or [tn] * len(out_dtypes)
    if b_head_major:
        out_specs = [pl.BlockSpec((None, tm, w), lambda i, j: (j, i, 0)) for w in out_widths]
        out_shape = [jax.ShapeDtypeStruct((nj, m, w), dt) for w, dt in zip(out_widths, out_dtypes)]
    else:
        out_specs = [pl.BlockSpec((tm, w), lambda i, j: (i, j)) for w in out_widths]
        out_shape = [jax.ShapeDtypeStruct((m, nj * w), dt) for w, dt in zip(out_widths, out_dtypes)]
    return pl.pallas_call(
        functools.partial(_mm_body, n_extra=len(extras), epilogue=epilogue),
        grid=(m // tm, nj),
        in_specs=[pl.BlockSpec((tm, k), lambda i, j: (i, 0)), b_spec, *extra_specs],
        out_specs=out_specs,
        out_shape=out_shape,
        compiler_params=_params(("parallel", "arbitrary")),
        name=name,
    )(a, b, *extras)


def _ep_plain(acc):
    return (acc,)


def _ep_sigmoid(acc):
    return (_sigmoid(acc),)


def _ep_residual(acc, res_ref):
    return (acc + res_ref[...],)


def _ep_qkv_norms(acc, gq_ref, gkv_ref):
    q = acc[:, :B_Q_LORA]
    kv = acc[:, B_Q_LORA:]
    qn = q * lax.rsqrt(jnp.mean(q * q, axis=-1, keepdims=True) + NORM_EPS) * gq_ref[...]
    kvn = kv * lax.rsqrt(jnp.mean(kv * kv, axis=-1, keepdims=True) + NORM_EPS) * gkv_ref[...]
    return qn, kvn


def _rope64_paired(x, cos_ref, sin_ref):
    return x * cos_ref[...] + pltpu.roll(x, 64, axis=1) * sin_ref[...]


def _ep_kpe(acc, cos_ref, sin_ref):
    return (_rope64_paired(acc, cos_ref, sin_ref),)


def _ep_q_up(acc, cos_ref, sin_ref):
    scale = (B_NOPE + B_ROPE) ** -0.5
    return (jnp.concatenate([acc[:, :B_NOPE], _rope64_paired(acc[:, B_NOPE:], cos_ref, sin_ref)], axis=1) * scale,)


def _ep_kv_up(acc, kpe_ref):
    k = jnp.concatenate([acc[:, :B_NOPE].astype(BF16), kpe_ref[...]], axis=1)
    return k, acc[:, B_NOPE:]


def _ep_ret_qk(acc, cos_ref, sin_ref, *, q_tiles):
    cos = cos_ref[...]
    sin = sin_ref[...]
    half = C_HEAD // 2
    pieces = []
    for h in range(acc.shape[1] // C_HEAD):
        x1 = acc[:, h * C_HEAD:h * C_HEAD + half]
        x2 = acc[:, h * C_HEAD + half:(h + 1) * C_HEAD]
        pieces += [x1 * cos - x2 * sin, x1 * sin + x2 * cos]
    out = jnp.concatenate(pieces, axis=1)
    scale = jnp.where(pl.program_id(1) < q_tiles, C_HEAD ** -0.5, 1.0)
    return (out * scale,)


def _ep_ret_vg(acc, *, v_tiles):
    return (jnp.where(pl.program_id(1) < v_tiles, acc, _silu(acc)),)


def _merge_body(oa_ref, ob_ref, oc_ref, w_ref, g_ref, out_ref, acc_ref):
    br = pl.program_id(2)
    g = g_ref[...].astype(F32)

    @pl.when(br == 0)
    def _():
        acc_ref[...] = g * _dot(oa_ref[...], w_ref[...])

    @pl.when(br == 1)
    def _():
        acc_ref[...] += g * _dot(ob_ref[...], w_ref[...])

    @pl.when(br == 2)
    def _():
        out_ref[...] = (acc_ref[...] + g * _dot(oc_ref[...], w_ref[...])).astype(out_ref.dtype)


def _merge(oa, ob, oc, w3, gates, tm=MM_ROWS, tn=512):
    m, kb = oa.shape
    n = w3.shape[2]
    nj = n // tn
    o_spec = pl.BlockSpec((tm, kb), lambda i, j, br: (i, 0))
    return pl.pallas_call(
        _merge_body,
        grid=(m // tm, nj, 3),
        in_specs=[o_spec, o_spec, o_spec,
                  pl.BlockSpec((None, kb, tn), lambda i, j, br: (br, 0, j)),
                  pl.BlockSpec((tm, tn), lambda i, j, br: (i, br * nj + j))],
        out_specs=pl.BlockSpec((tm, tn), lambda i, j, br: (i, j)),
        out_shape=jax.ShapeDtypeStruct((m, n), BF16),
        scratch_shapes=[pltpu.VMEM((tm, tn), F32)],
        compiler_params=_params(("parallel", "arbitrary", "arbitrary")),
        name="branch_merge",
    )(oa, ob, oc, w3, gates)


def _halo_specs(tm, w, col_block, n_rows, halo=SUBLANES):
    per_tile = tm // halo
    last = n_rows // halo - 1
    main = pl.BlockSpec((tm, w), lambda i, j: (i, col_block(j)))
    prev = pl.BlockSpec((halo, w), lambda i, j: (jnp.maximum(i * per_tile - 1, 0), col_block(j)))
    nxt = pl.BlockSpec((halo, w), lambda i, j: (jnp.minimum((i + 1) * per_tile, last), col_block(j)))
    return [main, prev, nxt]


def _shifted(x, prev_ref, next_ref, tiles_per_seq):
    tm = x.shape[0]
    i = pl.program_id(0)
    pos = i % tiles_per_seq
    halo = prev_ref.shape[0]
    prev_row = jnp.where(pos == 0, 0.0, prev_ref[halo - 1:halo, :].astype(F32))
    next_row = jnp.where(pos == tiles_per_seq - 1, 0.0, next_ref[0:1, :].astype(F32))
    row = lax.broadcasted_iota(jnp.int32, x.shape, 0)
    xp = jnp.where(row == 0, prev_row, pltpu.roll(x, 1, axis=0))
    xn = jnp.where(row == tm - 1, next_row, pltpu.roll(x, tm - 1, axis=0))
    return xp, xn


def _head_sum64(x):
    w = x.shape[1]
    r = lax.broadcasted_iota(jnp.int32, (w, w), 0) >> 6
    c = lax.broadcasted_iota(jnp.int32, (w, w), 1) >> 6
    ones = jnp.where(r == c, 1.0, 0.0).astype(BF16)
    return _dot(_bf(x), ones)


def _rwkv_pre_body(r_ref, rp_ref, rn_ref, k_ref, kp_ref, kn_ref, v_ref, vp_ref, vn_ref, l_ref, lp_ref, ln_ref,
                   mur_ref, muk_ref, muv_ref, mul_ref, w0_ref, a0_ref, wup_ref, aup_ref, gup_ref,
                   kk_w_ref, ka_ref, rk_ref,
                   r_out, v_out, kk_out, g_out, bonus_out, lw0_out, lw1_out, kz0_out, kz1_out, b0_out, b1_out,
                   *, tiles_per_seq):
    def mix(x_ref, p_ref, n_ref, mu_ref):
        x = x_ref[...]
        xp, xn = _shifted(x, p_ref, n_ref, tiles_per_seq)
        return x + mu_ref[0:1, :] * (xp - x) + mu_ref[1:2, :] * (xn - x)

    r = mix(r_ref, rp_ref, rn_ref, mur_ref)
    k = mix(k_ref, kp_ref, kn_ref, muk_ref)
    v = mix(v_ref, vp_ref, vn_ref, muv_ref)
    lo = mix(l_ref, lp_ref, ln_ref, mul_ref)
    wd = jnp.tanh(lo[:, 0:LANES])
    ad = lo[:, LANES:2 * LANES]
    gd = _sigmoid(lo[:, 2 * LANES:3 * LANES])
    lane = lax.broadcasted_iota(jnp.int32, (1, LANES), 1)

    g_out[...] = _dot(_bf(gd), gup_ref[...])
    kk = k * kk_w_ref[...]
    kk = kk / jnp.maximum(jnp.sqrt(_head_sum64(kk * kk)), 1e-12)
    r_out[...] = r.astype(r_out.dtype)
    v_out[...] = v.astype(v_out.dtype)
    kk_out[...] = kk.astype(kk_out.dtype)
    bonus_out[...] = _head_sum64(r * k * rk_ref[...]) * v
    for z, (lw_out, kz_out, b_out) in enumerate(((lw0_out, kz0_out, b0_out), (lw1_out, kz1_out, b1_out))):
        sel = (lane >> 6) == z
        w_pre = w0_ref[z:z + 1, :] + _dot(_bf(jnp.where(sel, wd, 0.0)), wup_ref[...])
        w_log = -(jnp.maximum(-w_pre, 0.0) + jnp.log(1.0 + jnp.exp(-jnp.abs(w_pre)))) - 0.5
        lw_out[...] = -jnp.exp(w_log)
        a = _sigmoid(a0_ref[z:z + 1, :] + _dot(_bf(jnp.where(sel, ad, 0.0)), aup_ref[...]))
        kz_out[...] = (k * (1.0 + (a - 1.0) * ka_ref[...])).astype(kz_out.dtype)
        b_out[...] = (kk * a).astype(b_out.dtype)


def _rwkv_prologue(ha, a_mu, a_w0, a_w_up, a_a0, a_a_up, a_g_up, a_k_k, a_k_a, a_r_k, seq_len, tm=256, tw=512):
    m = ha.shape[0]
    ncb = A_WIDTH // tw
    lora_block = 3 * A_WIDTH // A_LORA_IN
    in_specs = []
    for s in range(3):
        in_specs += _halo_specs(tm, tw, lambda j, s=s: s * ncb + j, m)
    in_specs += _halo_specs(tm, A_LORA_IN, lambda j: lora_block, m)
    for s in range(3):
        in_specs.append(pl.BlockSpec((2, tw), lambda i, j, s=s: (0, s * ncb + j)))
    in_specs.append(pl.BlockSpec((2, A_LORA_IN), lambda i, j: (0, lora_block)))
    vec2 = pl.BlockSpec((2, tw), lambda i, j: (0, j))
    up = pl.BlockSpec((LANES, tw), lambda i, j: (0, j))
    vec1 = pl.BlockSpec((1, tw), lambda i, j: (0, j))
    in_specs += [vec2, vec2, up, up, up, vec1, vec1, vec1]
    out_spec = pl.BlockSpec((tm, tw), lambda i, j: (i, j))
    out_dtypes = [BF16, BF16, BF16, F32, F32, F32, F32, BF16, BF16, BF16, BF16]
    return pl.pallas_call(
        functools.partial(_rwkv_pre_body, tiles_per_seq=seq_len // tm),
        grid=(m // tm, ncb),
        in_specs=in_specs,
        out_specs=[out_spec] * 11,
        out_shape=[jax.ShapeDtypeStruct((m, A_WIDTH), dt) for dt in out_dtypes],
        compiler_params=_params(("parallel", "arbitrary")),
        name="rwkv_prologue",
    )(ha, ha, ha, ha, ha, ha, ha, ha, ha, ha, ha, ha, a_mu, a_mu, a_mu, a_mu, a_w0, a_a0,
      a_w_up.reshape(LANES, A_WIDTH).astype(BF16), a_a_up.reshape(LANES, A_WIDTH).astype(BF16), a_g_up.astype(BF16),
      a_k_k.reshape(1, A_WIDTH), a_k_a.reshape(1, A_WIDTH), a_r_k.reshape(1, A_WIDTH))


def _rwkv_chunk_terms(insts, *, hp):
    c, w = insts[0][1].shape
    rows = c * hp
    n = len(insts)
    revs = [inst[0] for inst in insts]
    ti = lax.broadcasted_iota(jnp.int32, (c, c), 0)
    si = lax.broadcasted_iota(jnp.int32, (c, c), 1)
    tri = {False: jnp.where(si <= ti, 1.0, 0.0).astype(BF16), True: jnp.where(si >= ti, 1.0, 0.0).astype(BF16)}
    rt = lax.broadcasted_iota(jnp.int32, (rows, rows), 0)
    cs = lax.broadcasted_iota(jnp.int32, (rows, rows), 1)
    eye = jnp.where(rt == cs, 1.0, 0.0)
    rt = rt & (c - 1)
    cs = cs & (c - 1)
    strict = {False: cs < rt, True: cs > rt}
    incl = {False: cs <= rt, True: cs >= rt}
    lane_head = lax.broadcasted_iota(jnp.int32, (c, w), 1) >> 6
    ew = lax.broadcasted_iota(jnp.int32, (w, w), 0) == lax.broadcasted_iota(jnp.int32, (w, w), 1)

    def stack_f32(x):
        return jnp.concatenate([jnp.where(lane_head == h, x, 0.0) for h in range(hp)], axis=0)

    def stack(x):
        return _bf(stack_f32(x))

    cums = []
    for rev, _, _, _, lw, _, _ in insts:
        hi = lw.astype(BF16)
        mid = (lw - hi.astype(F32)).astype(BF16)
        lo = (lw - hi.astype(F32) - mid.astype(F32)).astype(BF16)
        parts = _dot(tri[rev], jnp.concatenate([hi, mid, lo], axis=1))
        cums.append(parts[:, :w] + parts[:, w:2 * w] + parts[:, 2 * w:])
    tots = [cum[0:1, :] if rev else cum[c - 1:c, :] for rev, cum in zip(revs, cums)]

    xs, rs_f, bs, ks, bh, kh, vs = [], [], [], [], [], [], []
    for (rev, r, v, kk, lw, kz, b), cum, tot in zip(insts, cums, tots):
        r, kk, kz, b = (x.astype(F32) for x in (r, kk, kz, b))
        e_neg = jnp.exp(-cum)
        e_rest = jnp.exp(tot - cum)
        xs.append(stack(kk * jnp.exp(cum - lw)))
        rs_f.append(stack_f32(r * jnp.exp(cum)))
        bs.append(stack(b * e_neg))
        ks.append(stack(kz * e_neg))
        bh.append(stack(b * e_rest))
        kh.append(stack(kz * e_rest))
        vs.append(stack(v))

    grams = [_dot_nt(jnp.concatenate([xs[i], _bf(rs_f[i])], axis=0), jnp.concatenate([bs[i], ks[i]], axis=0))
             for i in range(n)]
    l_b = [jnp.where(strict[revs[i]], grams[i][:rows, :rows], 0.0) for i in range(n)]
    lm = [jnp.concatenate([_bf(jnp.where(strict[revs[i]], grams[i][:rows, rows:], 0.0)),
                           _bf(jnp.where(incl[revs[i]], grams[i][rows:, rows:], 0.0))], axis=0) for i in range(n)]
    m_b = [_bf(jnp.where(incl[revs[i]], grams[i][rows:, :rows], 0.0)) for i in range(n)]
    lm_v = [_dot(lm[i], vs[i]) for i in range(n)]
    kt_v = [_dot_tn(kh[i], vs[i]) for i in range(n)]

    t_inv = [eye - l_b[i] for i in range(n)]
    lb = [_bf(x) for x in l_b]
    l_pow = [_dot(x, x) for x in lb]
    for _ in range(int(math.log2(c)) - 2):
        lp = [_bf(x) for x in l_pow]
        both = [_dot(lp[i], jnp.concatenate([_bf(t_inv[i]), lp[i]], axis=1)) for i in range(n)]
        t_inv = [t_inv[i] + both[i][:, :rows] for i in range(n)]
        l_pow = [both[i][:, rows:] for i in range(n)]
    t_inv = [t_inv[i] + _dot(_bf(l_pow[i]), _bf(t_inv[i])) for i in range(n)]

    gh = [_bf(_dot(_bf(t_inv[i]), jnp.concatenate([xs[i], _bf(lm_v[i][:rows])], axis=1))) for i in range(n)]
    bt_gh = [_dot_tn(bh[i], gh[i]) for i in range(n)]
    mb_gh = [_dot(m_b[i], gh[i]) for i in range(n)]
    out = []
    for i in range(n):
        phi = jnp.where(ew, jnp.exp(tots[i]), 0.0) - bt_gh[i][:, :w]
        psi = kt_v[i] - bt_gh[i][:, w:]
        q_p = rs_f[i] - mb_gh[i][:, :w]
        y0 = lm_v[i][rows:] - mb_gh[i][:, w:]
        out.append((_bf(q_p), y0, _bf(phi), psi))
    return out


def _rwkv_scan_body(rf_ref, vf_ref, kkf_ref, lwf_ref, kzf_ref, bf_ref, rb_ref, vb_ref, kkb_ref, lwb_ref, kzb_ref,
                    bb_ref, yf_ref, yb_ref, state_ref, *, hp, gp, cps, chunk):
    c = chunk
    w = A_HEAD * hp

    @pl.when(pl.program_id(2) == 0)
    def _():
        state_ref[...] = jnp.zeros_like(state_ref)

    dirs = ((False, (rf_ref, vf_ref, kkf_ref, lwf_ref, kzf_ref, bf_ref), yf_ref),
            (True, (rb_ref, vb_ref, kkb_ref, lwb_ref, kzb_ref, bb_ref), yb_ref))
    keys = [(d, gi, ci) for d in range(2) for gi in range(gp) for ci in range(cps)]
    insts = [(dirs[d][0], *[x[ci * c:(ci + 1) * c, gi * w:(gi + 1) * w] for x in dirs[d][1]]) for d, gi, ci in keys]
    terms = dict(zip(keys, _rwkv_chunk_terms(insts, hp=hp)))
    chains = [(d, gi) for d in range(2) for gi in range(gp)]
    states = {k: state_ref[k[0], k[1]] for k in chains}
    for step in range(cps):
        for d, gi in chains:
            ci = cps - 1 - step if dirs[d][0] else step
            q_p, y0, phi, psi = terms[(d, gi, ci)]
            a_b = _bf(states[(d, gi)])
            y_st = _dot(q_p, a_b) + y0
            states[(d, gi)] = _dot(phi, a_b) + psi
            y = y_st[0:c, :]
            for h in range(1, hp):
                y = y + y_st[h * c:(h + 1) * c, :]
            dirs[d][2][ci * c:(ci + 1) * c, gi * w:(gi + 1) * w] = y
    for d, gi in chains:
        state_ref[d, gi] = states[(d, gi)]


def _rwkv_scan(r, v, kk, lw0, kz0, b0, lw1, kz1, b1, batch, seq_len, hp=RWKV_HEADS_PER_GROUP,
               gp=RWKV_GROUPS_PER_STEP, cps=RWKV_CHUNKS_PER_STEP, chunk=RWKV_CHUNK):
    m = r.shape[0]
    w = A_HEAD * hp
    rows = chunk * cps
    ns = seq_len // rows
    fwd = pl.BlockSpec((rows, w * gp), lambda bi, g, n: (bi * ns + n, g))
    bwd = pl.BlockSpec((rows, w * gp), lambda bi, g, n: (bi * ns + ns - 1 - n, g))
    out = jax.ShapeDtypeStruct((m, A_WIDTH), F32)
    return pl.pallas_call(
        functools.partial(_rwkv_scan_body, hp=hp, gp=gp, cps=cps, chunk=chunk),
        grid=(batch, A_WIDTH // (w * gp), ns),
        in_specs=[fwd] * 6 + [bwd] * 6,
        out_specs=[fwd, bwd],
        out_shape=[out, out],
        scratch_shapes=[pltpu.VMEM((2, gp, w, w), F32)],
        compiler_params=_params(("parallel", "parallel", "arbitrary")),
        name="rwkv_scan",
    )(r, v, kk, lw0, kz0, b0, r, v, kk, lw1, kz1, b1)


def _rwkv_post_body(yf_ref, yb_ref, bonus_ref, g_ref, lnw_ref, lnb_ref, o_ref):
    y = yf_ref[...] + yb_ref[...]
    inv_n = 1.0 / A_HEAD
    mu = _head_sum64(y) * inv_n
    d = y - mu
    var = _head_sum64(d * d) * inv_n
    yn = d * lax.rsqrt(var + A_GN_EPS) * lnw_ref[...] + lnb_ref[...]
    o_ref[...] = ((yn + bonus_ref[...]) * g_ref[...]).astype(o_ref.dtype)


def _rwkv_epilogue(yf, yb, bonus, g, ln_w, ln_b, tm=256, tw=512):
    m = yf.shape[0]
    spec = pl.BlockSpec((tm, tw), lambda i, j: (i, j))
    vec = pl.BlockSpec((1, tw), lambda i, j: (0, j))
    return pl.pallas_call(
        _rwkv_post_body,
        grid=(m // tm, A_WIDTH // tw),
        in_specs=[spec, spec, spec, spec, vec, vec],
        out_specs=spec,
        out_shape=jax.ShapeDtypeStruct((m, A_WIDTH), BF16),
        compiler_params=_params(("parallel", "parallel")),
        name="rwkv_epilogue",
    )(yf, yb, bonus, g, ln_w.reshape(1, A_WIDTH), ln_b.reshape(1, A_WIDTH))


ATTN_QUERY_ROWS = 1024
ATTN_SUBTILES = 4
ATTN_LOOKAHEAD = 2


def _attn_body(q_ref, k_ref, v_ref, o_ref):
    sub = q_ref.shape[0] // ATTN_SUBTILES
    k = k_ref[...]
    v = v_ref[...]

    def scores(i):
        return _dot_nt(q_ref[i * sub:(i + 1) * sub, :], k)

    s = {i: scores(i) for i in range(ATTN_LOOKAHEAD)}
    for i in range(ATTN_SUBTILES):
        si = s.pop(i)
        p = jnp.exp(si - jnp.max(si, axis=-1, keepdims=True))
        l = jnp.sum(p, axis=-1, keepdims=True)
        if i + ATTN_LOOKAHEAD < ATTN_SUBTILES:
            s[i + ATTN_LOOKAHEAD] = scores(i + ATTN_LOOKAHEAD)
        o_ref[i * sub:(i + 1) * sub, :] = (_dot(_bf(p), v) / l).astype(o_ref.dtype)


def _attention(q, k, v, batch, seq_len, tq):
    m = q.shape[1]
    nq = seq_len // tq
    return pl.pallas_call(
        _attn_body,
        grid=(B_HEADS, batch, nq),
        in_specs=[pl.BlockSpec((None, tq, B_QK_PAD), lambda h, bi, qi: (h, bi * nq + qi, 0)),
                  pl.BlockSpec((None, seq_len, B_QK_PAD), lambda h, bi, qi: (h, bi, 0)),
                  pl.BlockSpec((None, seq_len, B_VDIM), lambda h, bi, qi: (h, bi, 0))],
        out_specs=pl.BlockSpec((tq, B_VDIM), lambda h, bi, qi: (bi * nq + qi, h)),
        out_shape=jax.ShapeDtypeStruct((m, B_HEADS * B_VDIM), BF16),
        compiler_params=_params(("parallel", "parallel", "arbitrary")),
        name="mla_attention",
    )(q, k, v)


def _ret_body(lg_ref, q_ref, k_ref, v_ref, *rest, rev, final, chunk, hps):
    if final:
        other_ref, gate_ref, o_ref, state_ref = rest
    else:
        o_ref, state_ref = rest
    c = chunk
    heads = range(hps)
    cols = [slice(h * C_HEAD, (h + 1) * C_HEAD) for h in heads]

    @pl.when(pl.program_id(2) == 0)
    def _():
        state_ref[...] = jnp.zeros_like(state_ref)

    lg = [lg_ref[h, 0:1, 0:1] for h in heads]
    q = [q_ref[:, cols[h]] for h in heads]
    k = [k_ref[:, cols[h]] for h in heads]
    v = [v_ref[:, cols[h]] for h in heads]
    idx = lax.broadcasted_iota(jnp.int32, (c, 1), 0).astype(F32)
    if rev:
        q_pow, k_pow = c - idx, idx
    else:
        q_pow, k_pow = idx + 1.0, c - 1.0 - idx
    qd = [_bf(q[h].astype(F32) * jnp.exp(lg[h] * q_pow)) for h in heads]
    kd = [_bf(k[h].astype(F32) * jnp.exp(lg[h] * k_pow)) for h in heads]
    state = [state_ref[h] for h in heads]
    cross = [_dot(qd[h], _bf(state[h])) for h in heads]
    kv = [_dot_tn(kd[h], v[h]) for h in heads]
    for h in heads:
        state_ref[h] = state[h] * jnp.exp(lg[h] * float(c)) + kv[h]
    if final:
        ti = lax.broadcasted_iota(jnp.int32, (c, c), 0)
        si = lax.broadcasted_iota(jnp.int32, (c, c), 1)
        dist = jnp.abs(ti - si).astype(F32)
        s = [_dot_nt(q[h], k[h]) for h in heads]
        inner = [_dot(_bf(s[h] * jnp.exp(lg[h] * dist)), v[h]) for h in heads]
        for h in heads:
            o = inner[h] + cross[h] + other_ref[:, cols[h]]
            mu = jnp.mean(o, axis=-1, keepdims=True)
            d = o - mu
            var = jnp.mean(d * d, axis=-1, keepdims=True)
            o_ref[:, cols[h]] = (gate_ref[:, cols[h]].astype(F32) * (d * lax.rsqrt(var + C_GN_EPS))).astype(o_ref.dtype)
    else:
        for h in heads:
            o_ref[:, cols[h]] = cross[h]


def _retention_pass(log_gamma, qk, vg, other, batch, seq_len, rev, final, chunk=RET_CHUNK, hps=RET_HEADS_PER_STEP):
    m = qk.shape[0]
    nc = seq_len // chunk
    ng = C_HEADS // hps
    if rev:
        row = lambda bi, n: bi * nc + nc - 1 - n
    else:
        row = lambda bi, n: bi * nc + n
    blk = lambda off: pl.BlockSpec((chunk, C_HEAD * hps), lambda g, bi, n: (row(bi, n), g + off))
    in_specs = [pl.BlockSpec((hps, SUBLANES, LANES), lambda g, bi, n: (g, 0, 0)), blk(0), blk(ng), blk(0)]
    args = [log_gamma, qk, qk, vg]
    if final:
        in_specs += [blk(0), blk(ng)]
        args += [other, vg]
    return pl.pallas_call(
        functools.partial(_ret_body, rev=rev, final=final, chunk=chunk, hps=hps),
        grid=(ng, batch, nc),
        in_specs=in_specs,
        out_specs=blk(0),
        out_shape=jax.ShapeDtypeStruct((m, C_WIDTH), BF16 if final else F32),
        scratch_shapes=[pltpu.VMEM((hps, C_HEAD, C_HEAD), F32)],
        compiler_params=_params(("parallel", "parallel", "arbitrary")),
        name="retention_final" if final else "retention_cross",
    )(*args)


def _ffn_act_body(g_ref, gp_ref, gn_ref, val_ref, cw_ref, cb_ref, o_ref, *, tiles_per_seq):
    g = g_ref[...].astype(F32)
    gp, gn = _shifted(g, gp_ref, gn_ref, tiles_per_seq)
    conv = cw_ref[0:1, :] * gp + cw_ref[1:2, :] * g + cw_ref[2:3, :] * gn + cb_ref[...]
    o_ref[...] = (_silu(conv) * val_ref[...].astype(F32)).astype(o_ref.dtype)


def _ffn_act(gv, conv_w, conv_b, seq_len, tm=256, tw=FFN_HIDDEN // 2):
    m = gv.shape[0]
    ncb = FFN_HIDDEN // tw
    return pl.pallas_call(
        functools.partial(_ffn_act_body, tiles_per_seq=seq_len // tm),
        grid=(m // tm, ncb),
        in_specs=[*_halo_specs(tm, tw, lambda j: j, m, halo=2 * SUBLANES),
                  pl.BlockSpec((tm, tw), lambda i, j: (i, ncb + j)),
                  pl.BlockSpec((3, tw), lambda i, j: (0, j)),
                  pl.BlockSpec((1, tw), lambda i, j: (0, j))],
        out_specs=pl.BlockSpec((tm, tw), lambda i, j: (i, j)),
        out_shape=jax.ShapeDtypeStruct((m, FFN_HIDDEN), BF16),
        compiler_params=_params(("parallel", "arbitrary")),
        name="ffn_conv_gate",
    )(gv, gv, gv, gv, conv_w, conv_b.reshape(1, FFN_HIDDEN))


def _rope_tables(seq_len, dim):
    inv = ROPE_BASE ** (-jnp.arange(0, dim, 2, dtype=F32) / dim)
    ang = jnp.arange(seq_len, dtype=F32)[:, None] * inv[None, :]
    return jnp.cos(ang), jnp.sin(ang)


def _swap_halves_cols(w):
    h = w.shape[-1] // 2
    return jnp.concatenate([w[..., h:], w[..., :h]], axis=-1)


def _prep_layer(w_in, b_q_up, b_kv_up, w_branch, w_out, ffn_up, ffn_down):
    o_b = A_IN
    o_kpe = o_b + B_Q_LORA + B_KV_LORA
    o_c = o_kpe + B_ROPE
    o_g = o_c + 4 * C_WIDTH
    w_kpe = w_in[:, o_kpe:o_c]
    qh = b_q_up.reshape(B_Q_LORA, B_HEADS, B_NOPE + B_ROPE)
    q_pe = qh[..., B_NOPE:]
    w_q = jnp.concatenate([qh[..., :B_NOPE], q_pe, _swap_halves_cols(q_pe)], axis=-1)
    return dict(
        w_a=jnp.pad(w_in[:, :A_IN], ((0, 0), (0, LANES))).astype(BF16),
        w_b=w_in[:, o_b:o_kpe].astype(BF16),
        w_kpe=jnp.concatenate([w_kpe, _swap_halves_cols(w_kpe)], axis=-1).astype(BF16),
        w_cqk=w_in[:, o_c:o_c + 2 * C_WIDTH].astype(BF16),
        w_cvg=w_in[:, o_c + 2 * C_WIDTH:o_g].astype(BF16),
        w_g=w_in[:, o_g:].astype(BF16),
        w_q=jnp.transpose(w_q, (1, 0, 2)).astype(BF16),
        w_kv=jnp.transpose(b_kv_up.reshape(B_KV_LORA, B_HEADS, B_NOPE + B_VDIM), (1, 0, 2)).astype(BF16),
        w_br=w_branch.reshape(3, A_WIDTH, D_MODEL).astype(BF16),
        w_out=w_out.astype(BF16),
        ffn_up=ffn_up.astype(BF16),
        ffn_down=ffn_down.astype(BF16),
    )


def _layer(x, batch, seq_len, tabs, pw, attn_norm, a_mu, a_w0, a_w_up, a_a0, a_a_up, a_g_up, a_k_k, a_k_a, a_r_k,
           a_ln_w, a_ln_b, b_q_norm, b_kv_norm, ffn_norm, ffn_conv, ffn_conv_b):
    m = x.shape[0]
    tps = lambda tm: seq_len // tm
    xn = _rmsnorm(x, attn_norm, BF16)

    (ha,) = _matmul("in_proj_a", xn, pw["w_a"], tm=MM_ROWS, tn=512, epilogue=_ep_plain, out_dtypes=[F32])
    r, v, kk, g, bonus, lw0, lw1, kz0, kz1, b0, b1 = _rwkv_prologue(
        ha, a_mu, a_w0, a_w_up, a_a0, a_a_up, a_g_up, a_k_k, a_k_a, a_r_k, seq_len)
    yf, yb = _rwkv_scan(r, v, kk, lw0, kz0, b0, lw1, kz1, b1, batch, seq_len)
    oa = _rwkv_epilogue(yf, yb, bonus, g, a_ln_w, a_ln_b)

    tm_b = 512
    vec = lambda n: pl.BlockSpec((1, n), lambda i, j: (0, 0))
    qn, kvn = _matmul("in_proj_b", xn, pw["w_b"], tm=tm_b, tn=B_Q_LORA + B_KV_LORA, epilogue=_ep_qkv_norms,
                      out_dtypes=[BF16, BF16], out_widths=[B_Q_LORA, B_KV_LORA],
                      extras=(b_q_norm.reshape(1, -1), b_kv_norm.reshape(1, -1)),
                      extra_specs=(vec(B_Q_LORA), vec(B_KV_LORA)))
    tab64 = lambda tm: pl.BlockSpec((tm, LANES), lambda i, j: (i % tps(tm), 0))
    (kpe,) = _matmul("in_proj_kpe", xn, pw["w_kpe"], tm=MM_ROWS, tn=LANES, epilogue=_ep_kpe, out_dtypes=[BF16],
                     extras=(tabs["cos64"], tabs["sin64"]), extra_specs=(tab64(MM_ROWS), tab64(MM_ROWS)))
    (q,) = _matmul("q_up", qn, pw["w_q"], tm=MM_ROWS, tn=B_QK_PAD, epilogue=_ep_q_up, out_dtypes=[BF16],
                   extras=(tabs["cos64"], tabs["sin64"]), extra_specs=(tab64(MM_ROWS), tab64(MM_ROWS)),
                   b_head_major=True)
    k, vv = _matmul("kv_up", kvn, pw["w_kv"], tm=MM_ROWS, tn=B_NOPE + B_VDIM, epilogue=_ep_kv_up,
                    out_dtypes=[BF16, BF16], out_widths=[B_QK_PAD, B_VDIM],
                    extras=(kpe,), extra_specs=(pl.BlockSpec((MM_ROWS, LANES), lambda i, j: (i, 0)),),
                    b_head_major=True)
    ob = _attention(q, k, vv, batch, seq_len, tq=ATTN_QUERY_ROWS)

    tab128 = pl.BlockSpec((MM_ROWS, LANES), lambda i, j: (i % tps(MM_ROWS), 0))
    q_tiles = C_WIDTH // 512
    (qk,) = _matmul("in_proj_c_qk", xn, pw["w_cqk"], tm=MM_ROWS, tn=512,
                    epilogue=functools.partial(_ep_ret_qk, q_tiles=q_tiles), out_dtypes=[BF16],
                    extras=(tabs["cos256"], tabs["sin256"]), extra_specs=(tab128, tab128))
    (vg,) = _matmul("in_proj_c_vg", xn, pw["w_cvg"], tm=MM_ROWS, tn=512,
                    epilogue=functools.partial(_ep_ret_vg, v_tiles=q_tiles), out_dtypes=[BF16])
    cross_b = _retention_pass(tabs["log_gamma"], qk, vg, None, batch, seq_len, rev=True, final=False)
    oc = _retention_pass(tabs["log_gamma"], qk, vg, cross_b, batch, seq_len, rev=False, final=True)

    (gates,) = _matmul("in_proj_gates", xn, pw["w_g"], tm=MM_ROWS, tn=512, epilogue=_ep_sigmoid, out_dtypes=[BF16])
    merged = _merge(oa, ob, oc, pw["w_br"], gates)
    res_spec = lambda tm, tn: pl.BlockSpec((tm, tn), lambda i, j: (i, j))
    (x,) = _matmul("out_proj", merged, pw["w_out"], tm=MM_ROWS, tn=512, epilogue=_ep_residual, out_dtypes=[F32],
                   extras=(x,), extra_specs=(res_spec(MM_ROWS, 512),))

    xn2 = _rmsnorm(x, ffn_norm, BF16)
    (gv,) = _matmul("ffn_up", xn2, pw["ffn_up"], tm=MM_ROWS, tn=512, epilogue=_ep_plain, out_dtypes=[BF16])
    act = _ffn_act(gv, ffn_conv, ffn_conv_b, seq_len)
    (x,) = _matmul("ffn_down", act, pw["ffn_down"], tm=512, tn=512, epilogue=_ep_residual, out_dtypes=[F32],
                   extras=(x,), extra_specs=(res_spec(512, 512),))
    return x


def _tables(seq_len):
    c64, s64 = _rope_tables(seq_len, B_ROPE)
    z = jnp.zeros((seq_len, B_ROPE), F32)
    c256, s256 = _rope_tables(seq_len, C_HEAD)
    log_gamma = jnp.log1p(-(2.0 ** (-5.0 - jnp.arange(C_HEADS, dtype=F32))))
    return dict(
        cos64=jnp.concatenate([c64, c64, z], axis=1),
        sin64=jnp.concatenate([-s64, s64, z], axis=1),
        cos256=c256, sin256=s256,
        log_gamma=jnp.broadcast_to(log_gamma[:, None, None], (C_HEADS, SUBLANES, LANES)),
    )


def _encoder(x3, prepped, per_layer, final_norm):
    batch, seq_len, d = x3.shape
    x = x3.reshape(batch * seq_len, d)
    tabs = _tables(seq_len)
    for pw, lw in zip(prepped, per_layer):
        x = _layer(x, batch, seq_len, tabs, pw, *lw)
    return _rmsnorm(x, final_norm, F32).reshape(batch, seq_len, d)


def kernel(x_prompt, x_sample, attn_norm, w_in, a_mu, a_w0, a_w_up, a_a0, a_a_up, a_g_up, a_k_k, a_k_a, a_r_k, a_ln_w, a_ln_b, b_q_norm, b_q_up, b_kv_norm, b_kv_up, w_branch, w_out, ffn_norm, ffn_up, ffn_conv, ffn_conv_b, ffn_down, final_norm):
    depth = w_in.shape[0]
    prepped = [_prep_layer(w_in[l], b_q_up[l], b_kv_up[l], w_branch[l], w_out[l], ffn_up[l], ffn_down[l])
               for l in range(depth)]
    per_layer = [(attn_norm[l], a_mu[l], a_w0[l], a_w_up[l], a_a0[l], a_a_up[l], a_g_up[l], a_k_k[l], a_k_a[l],
                  a_r_k[l], a_ln_w[l], a_ln_b[l], b_q_norm[l], b_kv_norm[l], ffn_norm[l], ffn_conv[l], ffn_conv_b[l])
                 for l in range(depth)]
    return (_encoder(x_prompt, prepped, per_layer, final_norm), _encoder(x_sample, prepped, per_layer, final_norm))
```

```python
import functools
import math

import jax
import jax.numpy as jnp
from jax import lax
from jax.experimental import pallas as pl
from jax.experimental.pallas import tpu as pltpu

F32 = jnp.float32
BF16 = jnp.bfloat16

D_MODEL = 4096
DEPTH = 2
A_HEAD = 64
A_WIDTH = 2048
A_LORA_IN = 384
A_IN = 3 * A_WIDTH + A_LORA_IN
A_GN_EPS = 64e-5
B_NOPE = 128
B_ROPE = 64
B_VDIM = 128
B_HEADS = 16
B_Q_LORA = 1536
B_KV_LORA = 512
B_QK_PAD = 256
C_HEAD = 256
C_HEADS = 8
C_WIDTH = 2048
C_GN_EPS = 1e-5
FFN_HIDDEN = 11008
ROPE_BASE = 10000.0
NORM_EPS = 1e-6

LANES = 128
SUBLANES = 8
VMEM_LIMIT_BYTES = 56 * 2**20
MM_ROWS = 1024

RWKV_CHUNK = 64
RWKV_HEADS_PER_GROUP = 2
RWKV_GROUPS_PER_STEP = 2
RWKV_CHUNKS_PER_STEP = 2
RET_CHUNK = 256
RET_HEADS_PER_STEP = 4


def _params(sem):
    return pltpu.CompilerParams(dimension_semantics=sem, vmem_limit_bytes=VMEM_LIMIT_BYTES)


def _dot(a, b):
    return jnp.dot(a, b, preferred_element_type=F32)


def _dot_nt(a, b):
    return lax.dot_general(a, b, (((1,), (1,)), ((), ())), preferred_element_type=F32)


def _dot_tn(a, b):
    return lax.dot_general(a, b, (((0,), (0,)), ((), ())), preferred_element_type=F32)


def _bf(x):
    return x.astype(BF16)


def _sigmoid(x):
    return 1.0 / (1.0 + jnp.exp(-x))


def _silu(x):
    return x * _sigmoid(x)


def _rmsnorm_body(x_ref, g_ref, o_ref):
    x = x_ref[...]
    y = x * lax.rsqrt(jnp.mean(x * x, axis=-1, keepdims=True) + NORM_EPS)
    o_ref[...] = (y * g_ref[...]).astype(o_ref.dtype)


def _rmsnorm(x, g, out_dtype, tm=256):
    m, d = x.shape
    return pl.pallas_call(
        _rmsnorm_body,
        grid=(m // tm,),
        in_specs=[pl.BlockSpec((tm, d), lambda i: (i, 0)), pl.BlockSpec((1, d), lambda i: (0, 0))],
        out_specs=pl.BlockSpec((tm, d), lambda i: (i, 0)),
        out_shape=jax.ShapeDtypeStruct((m, d), out_dtype),
        compiler_params=_params(("parallel",)),
        name="rmsnorm",
    )(x, g.reshape(1, d))


def _mm_body(a_ref, b_ref, *rest, n_extra, epilogue):
    extras = rest[:n_extra]
    outs = rest[n_extra:]
    acc = _dot(a_ref[...], b_ref[...])
    res = epilogue(acc, *extras)
    for o_ref, r in zip(outs, res):
        o_ref[...] = r.astype(o_ref.dtype)


def _matmul(name, a, b, *, tm, tn, epilogue, out_dtypes, out_widths=None, extras=(), extra_specs=(),
            b_head_major=False):
    m, k = a.shape
    if b_head_major:
        nj = b.shape[0]
        b_spec = pl.BlockSpec((None, k, tn), lambda i, j: (j, 0, 0))
    else:
        nj = b.shape[1] // tn
        b_spec = pl.BlockSpec((k, tn), lambda i, j: (0, j))
    out_widths = out_widths or [tn] * len(out_dtypes)
    if b_head_major:
        out_specs = [pl.BlockSpec((None, tm, w), lambda i, j: (j, i, 0)) for w in out_widths]
        out_shape = [jax.ShapeDtypeStruct((nj, m, w), dt) for w, dt in zip(out_widths, out_dtypes)]
    else:
        out_specs = [pl.BlockSpec((tm, w), lambda i, j: (i, j)) for w in out_widths]
        out_shape = [jax.ShapeDtypeStruct((m, nj * w), dt) for w, dt in zip(out_widths, out_dtypes)]
    return pl.pallas_call(
        functools.partial(_mm_body, n_extra=len(extras), epilogue=epilogue),
        grid=(m // tm, nj),
        in_specs=[pl.BlockSpec((tm, k), lambda i, j: (i, 0)), b_spec, *extra_specs],
        out_specs=out_specs,
        out_shape=out_shape,
        compiler_params=_params(("parallel", "arbitrary")),
        name=name,
    )(a, b, *extras)


def _ep_plain(acc):
    return (acc,)


def _ep_sigmoid(acc):
    return (_sigmoid(acc),)


def _ep_residual(acc, res_ref):
    return (acc + res_ref[...],)


def _ep_qkv_norms(acc, gq_ref, gkv_ref):
    q = acc[:, :B_Q_LORA]
    kv = acc[:, B_Q_LORA:]
    qn = q * lax.rsqrt(jnp.mean(q * q, axis=-1, keepdims=True) + NORM_EPS) * gq_ref[...]
    kvn = kv * lax.rsqrt(jnp.mean(kv * kv, axis=-1, keepdims=True) + NORM_EPS) * gkv_ref[...]
    return qn, kvn


def _rope64_paired(x, cos_ref, sin_ref):
    return x * cos_ref[...] + pltpu.roll(x, 64, axis=1) * sin_ref[...]


def _ep_kpe(acc, cos_ref, sin_ref):
    return (_rope64_paired(acc, cos_ref, sin_ref),)


def _ep_q_up(acc, cos_ref, sin_ref):
    scale = (B_NOPE + B_ROPE) ** -0.5
    return (jnp.concatenate([acc[:, :B_NOPE], _rope64_paired(acc[:, B_NOPE:], cos_ref, sin_ref)], axis=1) * scale,)


def _ep_kv_up(acc, kpe_ref):
    k = jnp.concatenate([acc[:, :B_NOPE].astype(BF16), kpe_ref[...]], axis=1)
    v = acc[:, B_NOPE:]
    return k, jnp.concatenate([v, jnp.ones_like(v)], axis=1)


def _ep_ret_qk(acc, cos_ref, sin_ref, *, q_tiles):
    cos = cos_ref[...]
    sin = sin_ref[...]
    half = C_HEAD // 2
    pieces = []
    for h in range(acc.shape[1] // C_HEAD):
        x1 = acc[:, h * C_HEAD:h * C_HEAD + half]
        x2 = acc[:, h * C_HEAD + half:(h + 1) * C_HEAD]
        pieces += [x1 * cos - x2 * sin, x1 * sin + x2 * cos]
    out = jnp.concatenate(pieces, axis=1)
    scale = jnp.where(pl.program_id(1) < q_tiles, C_HEAD ** -0.5, 1.0)
    return (out * scale,)


def _ep_ret_vg(acc, *, v_tiles):
    return (jnp.where(pl.program_id(1) < v_tiles, acc, _silu(acc)),)


def _merge_body(oa_ref, ob_ref, oc_ref, w_ref, g_ref, out_ref, acc_ref):
    br = pl.program_id(2)
    g = g_ref[...].astype(F32)

    @pl.when(br == 0)
    def _():
        acc_ref[...] = g * _dot(oa_ref[...], w_ref[...])

    @pl.when(br == 1)
    def _():
        acc_ref[...] += g * _dot(ob_ref[...], w_ref[...])

    @pl.when(br == 2)
    def _():
        out_ref[...] = (acc_ref[...] + g * _dot(oc_ref[...], w_ref[...])).astype(out_ref.dtype)


def _merge(oa, ob, oc, w3, gates, tm=MM_ROWS, tn=512):
    m, kb = oa.shape
    n = w3.shape[2]
    nj = n // tn
    o_spec = pl.BlockSpec((tm, kb), lambda i, j, br: (i, 0))
    return pl.pallas_call(
        _merge_body,
        grid=(m // tm, nj, 3),
        in_specs=[o_spec, o_spec, o_spec,
                  pl.BlockSpec((None, kb, tn), lambda i, j, br: (br, 0, j)),
                  pl.BlockSpec((tm, tn), lambda i, j, br: (i, br * nj + j))],
        out_specs=pl.BlockSpec((tm, tn), lambda i, j, br: (i, j)),
        out_shape=jax.ShapeDtypeStruct((m, n), BF16),
        scratch_shapes=[pltpu.VMEM((tm, tn), F32)],
        compiler_params=_params(("parallel", "arbitrary", "arbitrary")),
        name="branch_merge",
    )(oa, ob, oc, w3, gates)


def _halo_specs(tm, w, col_block, n_rows, halo=SUBLANES):
    per_tile = tm // halo
    last = n_rows // halo - 1
    main = pl.BlockSpec((tm, w), lambda i, j: (i, col_block(j)))
    prev = pl.BlockSpec((halo, w), lambda i, j: (jnp.maximum(i * per_tile - 1, 0), col_block(j)))
    nxt = pl.BlockSpec((halo, w), lambda i, j: (jnp.minimum((i + 1) * per_tile, last), col_block(j)))
    return [main, prev, nxt]


def _shifted(x, prev_ref, next_ref, tiles_per_seq):
    tm = x.shape[0]
    i = pl.program_id(0)
    pos = i % tiles_per_seq
    halo = prev_ref.shape[0]
    prev_row = jnp.where(pos == 0, 0.0, prev_ref[halo - 1:halo, :].astype(F32))
    next_row = jnp.where(pos == tiles_per_seq - 1, 0.0, next_ref[0:1, :].astype(F32))
    row = lax.broadcasted_iota(jnp.int32, x.shape, 0)
    xp = jnp.where(row == 0, prev_row, pltpu.roll(x, 1, axis=0))
    xn = jnp.where(row == tm - 1, next_row, pltpu.roll(x, tm - 1, axis=0))
    return xp, xn


def _head_sum64(x):
    w = x.shape[1]
    r = lax.broadcasted_iota(jnp.int32, (w, w), 0) >> 6
    c = lax.broadcasted_iota(jnp.int32, (w, w), 1) >> 6
    ones = jnp.where(r == c, 1.0, 0.0).astype(BF16)
    return _dot(_bf(x), ones)


def _rwkv_pre_body(r_ref, rp_ref, rn_ref, k_ref, kp_ref, kn_ref, v_ref, vp_ref, vn_ref, l_ref, lp_ref, ln_ref,
                   mur_ref, muk_ref, muv_ref, mul_ref, w0_ref, a0_ref, wup_ref, aup_ref, gup_ref,
                   kk_w_ref, ka_ref, rk_ref,
                   r_out, v_out, kk_out, g_out, bonus_out, lw0_out, lw1_out, kz0_out, kz1_out, b0_out, b1_out,
                   *, tiles_per_seq):
    def mix(x_ref, p_ref, n_ref, mu_ref):
        x = x_ref[...]
        xp, xn = _shifted(x, p_ref, n_ref, tiles_per_seq)
        return x + mu_ref[0:1, :] * (xp - x) + mu_ref[1:2, :] * (xn - x)

    r = mix(r_ref, rp_ref, rn_ref, mur_ref)
    k = mix(k_ref, kp_ref, kn_ref, muk_ref)
    v = mix(v_ref, vp_ref, vn_ref, muv_ref)
    lo = mix(l_ref, lp_ref, ln_ref, mul_ref)
    wd = jnp.tanh(lo[:, 0:LANES])
    ad = lo[:, LANES:2 * LANES]
    gd = _sigmoid(lo[:, 2 * LANES:3 * LANES])
    lane = lax.broadcasted_iota(jnp.int32, (1, LANES), 1)

    g_out[...] = _dot(_bf(gd), gup_ref[...])
    kk = k * kk_w_ref[...]
    kk = kk / jnp.maximum(jnp.sqrt(_head_sum64(kk * kk)), 1e-12)
    r_out[...] = r.astype(r_out.dtype)
    v_out[...] = v.astype(v_out.dtype)
    kk_out[...] = kk.astype(kk_out.dtype)
    bonus_out[...] = _head_sum64(r * k * rk_ref[...]) * v
    for z, (lw_out, kz_out, b_out) in enumerate(((lw0_out, kz0_out, b0_out), (lw1_out, kz1_out, b1_out))):
        sel = (lane >> 6) == z
        w_pre = w0_ref[z:z + 1, :] + _dot(_bf(jnp.where(sel, wd, 0.0)), wup_ref[...])
        w_log = -(jnp.maximum(-w_pre, 0.0) + jnp.log(1.0 + jnp.exp(-jnp.abs(w_pre)))) - 0.5
        lw_out[...] = -jnp.exp(w_log)
        a = _sigmoid(a0_ref[z:z + 1, :] + _dot(_bf(jnp.where(sel, ad, 0.0)), aup_ref[...]))
        kz_out[...] = (k * (1.0 + (a - 1.0) * ka_ref[...])).astype(kz_out.dtype)
        b_out[...] = (kk * a).astype(b_out.dtype)


def _rwkv_prologue(ha, a_mu, a_w0, a_w_up, a_a0, a_a_up, a_g_up, a_k_k, a_k_a, a_r_k, seq_len, tm=256, tw=512):
    m = ha.shape[0]
    ncb = A_WIDTH // tw
    lora_block = 3 * A_WIDTH // A_LORA_IN
    in_specs = []
    for s in range(3):
        in_specs += _halo_specs(tm, tw, lambda j, s=s: s * ncb + j, m)
    in_specs += _halo_specs(tm, A_LORA_IN, lambda j: lora_block, m)
    for s in range(3):
        in_specs.append(pl.BlockSpec((2, tw), lambda i, j, s=s: (0, s * ncb + j)))
    in_specs.append(pl.BlockSpec((2, A_LORA_IN), lambda i, j: (0, lora_block)))
    vec2 = pl.BlockSpec((2, tw), lambda i, j: (0, j))
    up = pl.BlockSpec((LANES, tw), lambda i, j: (0, j))
    vec1 = pl.BlockSpec((1, tw), lambda i, j: (0, j))
    in_specs += [vec2, vec2, up, up, up, vec1, vec1, vec1]
    out_spec = pl.BlockSpec((tm, tw), lambda i, j: (i, j))
    out_dtypes = [BF16, BF16, BF16, F32, F32, F32, F32, BF16, BF16, BF16, BF16]
    return pl.pallas_call(
        functools.partial(_rwkv_pre_body, tiles_per_seq=seq_len // tm),
        grid=(m // tm, ncb),
        in_specs=in_specs,
        out_specs=[out_spec] * 11,
        out_shape=[jax.ShapeDtypeStruct((m, A_WIDTH), dt) for dt in out_dtypes],
        compiler_params=_params(("parallel", "arbitrary")),
        name="rwkv_prologue",
    )(ha, ha, ha, ha, ha, ha, ha, ha, ha, ha, ha, ha, a_mu, a_mu, a_mu, a_mu, a_w0, a_a0,
      a_w_up.reshape(LANES, A_WIDTH).astype(BF16), a_a_up.reshape(LANES, A_WIDTH).astype(BF16), a_g_up.astype(BF16),
      a_k_k.reshape(1, A_WIDTH), a_k_a.reshape(1, A_WIDTH), a_r_k.reshape(1, A_WIDTH))


def _rwkv_chunk_terms(insts, *, hp):
    c, w = insts[0][1].shape
    rows = c * hp
    n = len(insts)
    revs = [inst[0] for inst in insts]
    ti = lax.broadcasted_iota(jnp.int32, (c, c), 0)
    si = lax.broadcasted_iota(jnp.int32, (c, c), 1)
    tri = {False: jnp.where(si <= ti, 1.0, 0.0).astype(BF16), True: jnp.where(si >= ti, 1.0, 0.0).astype(BF16)}
    rt = lax.broadcasted_iota(jnp.int32, (rows, rows), 0)
    cs = lax.broadcasted_iota(jnp.int32, (rows, rows), 1)
    eye = jnp.where(rt == cs, 1.0, 0.0)
    rt = rt & (c - 1)
    cs = cs & (c - 1)
    strict = {False: cs < rt, True: cs > rt}
    incl = {False: cs <= rt, True: cs >= rt}
    lane_head = lax.broadcasted_iota(jnp.int32, (c, w), 1) >> 6
    ew = lax.broadcasted_iota(jnp.int32, (w, w), 0) == lax.broadcasted_iota(jnp.int32, (w, w), 1)

    def stack_f32(x):
        return jnp.concatenate([jnp.where(lane_head == h, x, 0.0) for h in range(hp)], axis=0)

    def stack(x):
        return _bf(stack_f32(x))

    cums = []
    for rev, _, _, _, lw, _, _ in insts:
        hi = lw.astype(BF16)
        mid = (lw - hi.astype(F32)).astype(BF16)
        lo = (lw - hi.astype(F32) - mid.astype(F32)).astype(BF16)
        parts = _dot(tri[rev], jnp.concatenate([hi, mid, lo], axis=1))
        cums.append(parts[:, :w] + parts[:, w:2 * w] + parts[:, 2 * w:])
    tots = [cum[0:1, :] if rev else cum[c - 1:c, :] for rev, cum in zip(revs, cums)]

    xs, rs_f, bs, ks, bh, kh, vs = [], [], [], [], [], [], []
    for (rev, r, v, kk, lw, kz, b), cum, tot in zip(insts, cums, tots):
        r, kk, kz, b = (x.astype(F32) for x in (r, kk, kz, b))
        e_neg = jnp.exp(-cum)
        e_rest = jnp.exp(tot - cum)
        xs.append(stack(kk * jnp.exp(cum - lw)))
        rs_f.append(stack_f32(r * jnp.exp(cum)))
        bs.append(stack(b * e_neg))
        ks.append(stack(kz * e_neg))
        bh.append(stack(b * e_rest))
        kh.append(stack(kz * e_rest))
        vs.append(stack(v))

    grams = [_dot_nt(jnp.concatenate([xs[i], _bf(rs_f[i])], axis=0), jnp.concatenate([bs[i], ks[i]], axis=0))
             for i in range(n)]
    l_b = [jnp.where(strict[revs[i]], grams[i][:rows, :rows], 0.0) for i in range(n)]
    lm = [jnp.concatenate([_bf(jnp.where(strict[revs[i]], grams[i][:rows, rows:], 0.0)),
                           _bf(jnp.where(incl[revs[i]], grams[i][rows:, rows:], 0.0))], axis=0) for i in range(n)]
    m_b = [_bf(jnp.where(incl[revs[i]], grams[i][rows:, :rows], 0.0)) for i in range(n)]
    lm_v = [_dot(lm[i], vs[i]) for i in range(n)]
    kt_v = [_dot_tn(kh[i], vs[i]) for i in range(n)]

    t_inv = [eye - l_b[i] for i in range(n)]
    lb = [_bf(x) for x in l_b]
    l_pow = [_dot(x, x) for x in lb]
    for _ in range(int(math.log2(c)) - 2):
        lp = [_bf(x) for x in l_pow]
        both = [_dot(lp[i], jnp.concatenate([_bf(t_inv[i]), lp[i]], axis=1)) for i in range(n)]
        t_inv = [t_inv[i] + both[i][:, :rows] for i in range(n)]
        l_pow = [both[i][:, rows:] for i in range(n)]
    t_inv = [t_inv[i] + _dot(_bf(l_pow[i]), _bf(t_inv[i])) for i in range(n)]

    gh = [_bf(_dot(_bf(t_inv[i]), jnp.concatenate([xs[i], _bf(lm_v[i][:rows])], axis=1))) for i in range(n)]
    bt_gh = [_dot_tn(bh[i], gh[i]) for i in range(n)]
    mb_gh = [_dot(m_b[i], gh[i]) for i in range(n)]
    out = []
    for i in range(n):
        phi = jnp.where(ew, jnp.exp(tots[i]), 0.0) - bt_gh[i][:, :w]
        psi = kt_v[i] - bt_gh[i][:, w:]
        q_p = rs_f[i] - mb_gh[i][:, :w]
        y0 = lm_v[i][rows:] - mb_gh[i][:, w:]
        out.append((_bf(q_p), y0, _bf(phi), psi))
    return out


def _rwkv_scan_body(rf_ref, vf_ref, kkf_ref, lwf_ref, kzf_ref, bf_ref, rb_ref, vb_ref, kkb_ref, lwb_ref, kzb_ref,
                    bb_ref, yf_ref, yb_ref, state_ref, *, hp, gp, cps, chunk):
    c = chunk
    w = A_HEAD * hp

    @pl.when(pl.program_id(2) == 0)
    def _():
        state_ref[...] = jnp.zeros_like(state_ref)

    dirs = ((False, (rf_ref, vf_ref, kkf_ref, lwf_ref, kzf_ref, bf_ref), yf_ref),
            (True, (rb_ref, vb_ref, kkb_ref, lwb_ref, kzb_ref, bb_ref), yb_ref))
    keys = [(d, gi, ci) for d in range(2) for gi in range(gp) for ci in range(cps)]
    insts = [(dirs[d][0], *[x[ci * c:(ci + 1) * c, gi * w:(gi + 1) * w] for x in dirs[d][1]]) for d, gi, ci in keys]
    terms = dict(zip(keys, _rwkv_chunk_terms(insts, hp=hp)))
    chains = [(d, gi) for d in range(2) for gi in range(gp)]
    states = {k: state_ref[k[0], k[1]] for k in chains}
    for step in range(cps):
        for d, gi in chains:
            ci = cps - 1 - step if dirs[d][0] else step
            q_p, y0, phi, psi = terms[(d, gi, ci)]
            a_b = _bf(states[(d, gi)])
            y_st = _dot(q_p, a_b) + y0
            states[(d, gi)] = _dot(phi, a_b) + psi
            y = y_st[0:c, :]
            for h in range(1, hp):
                y = y + y_st[h * c:(h + 1) * c, :]
            dirs[d][2][ci * c:(ci + 1) * c, gi * w:(gi + 1) * w] = y
    for d, gi in chains:
        state_ref[d, gi] = states[(d, gi)]


def _rwkv_scan(r, v, kk, lw0, kz0, b0, lw1, kz1, b1, batch, seq_len, hp=RWKV_HEADS_PER_GROUP,
               gp=RWKV_GROUPS_PER_STEP, cps=RWKV_CHUNKS_PER_STEP, chunk=RWKV_CHUNK):
    m = r.shape[0]
    w = A_HEAD * hp
    rows = chunk * cps
    ns = seq_len // rows
    fwd = pl.BlockSpec((rows, w * gp), lambda bi, g, n: (bi * ns + n, g))
    bwd = pl.BlockSpec((rows, w * gp), lambda bi, g, n: (bi * ns + ns - 1 - n, g))
    out = jax.ShapeDtypeStruct((m, A_WIDTH), F32)
    return pl.pallas_call(
        functools.partial(_rwkv_scan_body, hp=hp, gp=gp, cps=cps, chunk=chunk),
        grid=(batch, A_WIDTH // (w * gp), ns),
        in_specs=[fwd] * 6 + [bwd] * 6,
        out_specs=[fwd, bwd],
        out_shape=[out, out],
        scratch_shapes=[pltpu.VMEM((2, gp, w, w), F32)],
        compiler_params=_params(("parallel", "parallel", "arbitrary")),
        name="rwkv_scan",
    )(r, v, kk, lw0, kz0, b0, r, v, kk, lw1, kz1, b1)


def _rwkv_post_body(yf_ref, yb_ref, bonus_ref, g_ref, lnw_ref, lnb_ref, o_ref):
    y = yf_ref[...] + yb_ref[...]
    inv_n = 1.0 / A_HEAD
    mu = _head_sum64(y) * inv_n
    d = y - mu
    var = _head_sum64(d * d) * inv_n
    yn = d * lax.rsqrt(var + A_GN_EPS) * lnw_ref[...] + lnb_ref[...]
    o_ref[...] = ((yn + bonus_ref[...]) * g_ref[...]).astype(o_ref.dtype)


def _rwkv_epilogue(yf, yb, bonus, g, ln_w, ln_b, tm=256, tw=512):
    m = yf.shape[0]
    spec = pl.BlockSpec((tm, tw), lambda i, j: (i, j))
    vec = pl.BlockSpec((1, tw), lambda i, j: (0, j))
    return pl.pallas_call(
        _rwkv_post_body,
        grid=(m // tm, A_WIDTH // tw),
        in_specs=[spec, spec, spec, spec, vec, vec],
        out_specs=spec,
        out_shape=jax.ShapeDtypeStruct((m, A_WIDTH), BF16),
        compiler_params=_params(("parallel", "parallel")),
        name="rwkv_epilogue",
    )(yf, yb, bonus, g, ln_w.reshape(1, A_WIDTH), ln_b.reshape(1, A_WIDTH))


ATTN_QUERY_ROWS = 1024
ATTN_SUBTILES = 4
ATTN_LOOKAHEAD = 2


def _attn_body(q_ref, k_ref, v_ref, o_ref):
    sub = q_ref.shape[0] // ATTN_SUBTILES
    k = k_ref[...]
    v = v_ref[...]

    def scores(i):
        return _dot_nt(q_ref[i * sub:(i + 1) * sub, :], k)

    s = {i: scores(i) for i in range(ATTN_LOOKAHEAD)}
    for i in range(ATTN_SUBTILES):
        si = s.pop(i)
        p = jnp.exp(_bf(si - jnp.max(si, axis=-1, keepdims=True)))
        if i + ATTN_LOOKAHEAD < ATTN_SUBTILES:
            s[i + ATTN_LOOKAHEAD] = scores(i + ATTN_LOOKAHEAD)
        o = _dot(p, v)
        o_ref[i * sub:(i + 1) * sub, :] = (o[:, :B_VDIM] / o[:, B_VDIM:]).astype(o_ref.dtype)


def _attention(q, k, v, batch, seq_len, tq):
    m = q.shape[1]
    nq = seq_len // tq
    return pl.pallas_call(
        _attn_body,
        grid=(B_HEADS, batch, nq),
        in_specs=[pl.BlockSpec((None, tq, B_QK_PAD), lambda h, bi, qi: (h, bi * nq + qi, 0)),
                  pl.BlockSpec((None, seq_len, B_QK_PAD), lambda h, bi, qi: (h, bi, 0)),
                  pl.BlockSpec((None, seq_len, 2 * B_VDIM), lambda h, bi, qi: (h, bi, 0))],
        out_specs=pl.BlockSpec((tq, B_VDIM), lambda h, bi, qi: (bi * nq + qi, h)),
        out_shape=jax.ShapeDtypeStruct((m, B_HEADS * B_VDIM), BF16),
        compiler_params=_params(("parallel", "parallel", "arbitrary")),
        name="mla_attention",
    )(q, k, v)


def _ret_body(lg_ref, q_ref, k_ref, v_ref, *rest, rev, final, chunk, hps):
    if final:
        other_ref, gate_ref, o_ref, state_ref = rest
    else:
        o_ref, state_ref = rest
    c = chunk
    heads = range(hps)
    cols = [slice(h * C_HEAD, (h + 1) * C_HEAD) for h in heads]

    @pl.when(pl.program_id(2) == 0)
    def _():
        state_ref[...] = jnp.zeros_like(state_ref)

    lg = [lg_ref[h, 0:1, 0:1] for h in heads]
    q = [q_ref[:, cols[h]] for h in heads]
    k = [k_ref[:, cols[h]] for h in heads]
    v = [v_ref[:, cols[h]] for h in heads]
    idx = lax.broadcasted_iota(jnp.int32, (c, 1), 0).astype(F32)
    if rev:
        q_pow, k_pow = c - idx, idx
    else:
        q_pow, k_pow = idx + 1.0, c - 1.0 - idx
    qd = [_bf(q[h].astype(F32) * jnp.exp(lg[h] * q_pow)) for h in heads]
    kd = [_bf(k[h].astype(F32) * jnp.exp(lg[h] * k_pow)) for h in heads]
    state = [state_ref[h] for h in heads]
    cross = [_dot(qd[h], _bf(state[h])) for h in heads]
    kv = [_dot_tn(kd[h], v[h]) for h in heads]
    for h in heads:
        state_ref[h] = state[h] * jnp.exp(lg[h] * float(c)) + kv[h]
    if final:
        ti = lax.broadcasted_iota(jnp.int32, (c, c), 0)
        si = lax.broadcasted_iota(jnp.int32, (c, c), 1)
        dist = jnp.abs(ti - si).astype(F32)
        s = [_dot_nt(q[h], k[h]) for h in heads]
        inner = [_dot(_bf(s[h] * jnp.exp(lg[h] * dist)), v[h]) for h in heads]
        for h in heads:
            o = inner[h] + cross[h] + other_ref[:, cols[h]]
            mu = jnp.mean(o, axis=-1, keepdims=True)
            d = o - mu
            var = jnp.mean(d * d, axis=-1, keepdims=True)
            o_ref[:, cols[h]] = (gate_ref[:, cols[h]].astype(F32) * (d * lax.rsqrt(var + C_GN_EPS))).astype(o_ref.dtype)
    else:
        for h in heads:
            o_ref[:, cols[h]] = cross[h]


def _retention_pass(log_gamma, qk, vg, other, batch, seq_len, rev, final, chunk=RET_CHUNK, hps=RET_HEADS_PER_STEP):
    m = qk.shape[0]
    nc = seq_len // chunk
    ng = C_HEADS // hps
    if rev:
        row = lambda bi, n: bi * nc + nc - 1 - n
    else:
        row = lambda bi, n: bi * nc + n
    blk = lambda off: pl.BlockSpec((chunk, C_HEAD * hps), lambda g, bi, n: (row(bi, n), g + off))
    in_specs = [pl.BlockSpec((hps, SUBLANES, LANES), lambda g, bi, n: (g, 0, 0)), blk(0), blk(ng), blk(0)]
    args = [log_gamma, qk, qk, vg]
    if final:
        in_specs += [blk(0), blk(ng)]
        args += [other, vg]
    return pl.pallas_call(
        functools.partial(_ret_body, rev=rev, final=final, chunk=chunk, hps=hps),
        grid=(ng, batch, nc),
        in_specs=in_specs,
        out_specs=blk(0),
        out_shape=jax.ShapeDtypeStruct((m, C_WIDTH), BF16 if final else F32),
        scratch_shapes=[pltpu.VMEM((hps, C_HEAD, C_HEAD), F32)],
        compiler_params=_params(("parallel", "parallel", "arbitrary")),
        name="retention_final" if final else "retention_cross",
    )(*args)


def _ffn_up_body(x_ref, xp_ref, xn_ref, wg_ref, wv_ref, cw_ref, cb_ref, o_ref, xext_ref, *, tiles_per_seq):
    tm = x_ref.shape[0]
    halo = xp_ref.shape[0]

    @pl.when(pl.program_id(1) == 0)
    def _():
        pos = pl.program_id(0) % tiles_per_seq
        xext_ref[0:halo, :] = jnp.where(pos == 0, jnp.zeros_like(xp_ref[...]), xp_ref[...])
        xext_ref[halo:halo + tm, :] = x_ref[...]
        xext_ref[halo + tm:, :] = jnp.where(pos == tiles_per_seq - 1, jnp.zeros_like(xn_ref[...]), xn_ref[...])

    g_ext = _dot(xext_ref[...], wg_ref[...])
    val = _dot(xext_ref[halo:halo + tm, :], wv_ref[...])
    rows = tm + 2 * halo
    g = g_ext[halo:halo + tm]
    gp = pltpu.roll(g_ext, 1, axis=0)[halo:halo + tm]
    gn = pltpu.roll(g_ext, rows - 1, axis=0)[halo:halo + tm]
    conv = cw_ref[0:1, :] * gp + cw_ref[1:2, :] * g + cw_ref[2:3, :] * gn + cb_ref[...]
    o_ref[...] = (_silu(conv) * val).astype(o_ref.dtype)


def _ffn_up_act(xn, w_up, conv_w, conv_b, seq_len, tm=MM_ROWS, tn=256):
    m, k = xn.shape
    nj = FFN_HIDDEN // tn
    halo = 2 * SUBLANES
    return pl.pallas_call(
        functools.partial(_ffn_up_body, tiles_per_seq=seq_len // tm),
        grid=(m // tm, nj),
        in_specs=[*_halo_specs(tm, k, lambda j: 0, m, halo=halo),
                  pl.BlockSpec((k, tn), lambda i, j: (0, j)),
                  pl.BlockSpec((k, tn), lambda i, j: (0, nj + j)),
                  pl.BlockSpec((3, tn), lambda i, j: (0, j)),
                  pl.BlockSpec((1, tn), lambda i, j: (0, j))],
        out_specs=pl.BlockSpec((tm, tn), lambda i, j: (i, j)),
        out_shape=jax.ShapeDtypeStruct((m, FFN_HIDDEN), BF16),
        scratch_shapes=[pltpu.VMEM((tm + 2 * halo, k), BF16)],
        compiler_params=_params(("parallel", "arbitrary")),
        name="ffn_up_conv_gate",
    )(xn, xn, xn, w_up, w_up, conv_w, conv_b.reshape(1, FFN_HIDDEN))


def _rope_tables(seq_len, dim):
    inv = ROPE_BASE ** (-jnp.arange(0, dim, 2, dtype=F32) / dim)
    ang = jnp.arange(seq_len, dtype=F32)[:, None] * inv[None, :]
    return jnp.cos(ang), jnp.sin(ang)


def _swap_halves_cols(w):
    h = w.shape[-1] // 2
    return jnp.concatenate([w[..., h:], w[..., :h]], axis=-1)


def _prep_layer(w_in, b_q_up, b_kv_up, w_branch, w_out, ffn_up, ffn_down):
    o_b = A_IN
    o_kpe = o_b + B_Q_LORA + B_KV_LORA
    o_c = o_kpe + B_ROPE
    o_g = o_c + 4 * C_WIDTH
    w_kpe = w_in[:, o_kpe:o_c]
    qh = b_q_up.reshape(B_Q_LORA, B_HEADS, B_NOPE + B_ROPE)
    q_pe = qh[..., B_NOPE:]
    w_q = jnp.concatenate([qh[..., :B_NOPE], q_pe, _swap_halves_cols(q_pe)], axis=-1)
    return dict(
        w_a=jnp.pad(w_in[:, :A_IN], ((0, 0), (0, LANES))).astype(BF16),
        w_b=w_in[:, o_b:o_kpe].astype(BF16),
        w_kpe=jnp.concatenate([w_kpe, _swap_halves_cols(w_kpe)], axis=-1).astype(BF16),
        w_cqk=w_in[:, o_c:o_c + 2 * C_WIDTH].astype(BF16),
        w_cvg=w_in[:, o_c + 2 * C_WIDTH:o_g].astype(BF16),
        w_g=w_in[:, o_g:].astype(BF16),
        w_q=jnp.transpose(w_q, (1, 0, 2)).astype(BF16),
        w_kv=jnp.transpose(b_kv_up.reshape(B_KV_LORA, B_HEADS, B_NOPE + B_VDIM), (1, 0, 2)).astype(BF16),
        w_br=w_branch.reshape(3, A_WIDTH, D_MODEL).astype(BF16),
        w_out=w_out.astype(BF16),
        ffn_up=ffn_up.astype(BF16),
        ffn_down=ffn_down.astype(BF16),
    )


def _layer(x, batch, seq_len, tabs, pw, attn_norm, a_mu, a_w0, a_w_up, a_a0, a_a_up, a_g_up, a_k_k, a_k_a, a_r_k,
           a_ln_w, a_ln_b, b_q_norm, b_kv_norm, ffn_norm, ffn_conv, ffn_conv_b):
    m = x.shape[0]
    tps = lambda tm: seq_len // tm
    xn = _rmsnorm(x, attn_norm, BF16)

    (ha,) = _matmul("in_proj_a", xn, pw["w_a"], tm=MM_ROWS, tn=512, epilogue=_ep_plain, out_dtypes=[F32])
    r, v, kk, g, bonus, lw0, lw1, kz0, kz1, b0, b1 = _rwkv_prologue(
        ha, a_mu, a_w0, a_w_up, a_a0, a_a_up, a_g_up, a_k_k, a_k_a, a_r_k, seq_len)
    yf, yb = _rwkv_scan(r, v, kk, lw0, kz0, b0, lw1, kz1, b1, batch, seq_len)
    oa = _rwkv_epilogue(yf, yb, bonus, g, a_ln_w, a_ln_b)

    tm_b = 512
    vec = lambda n: pl.BlockSpec((1, n), lambda i, j: (0, 0))
    qn, kvn = _matmul("in_proj_b", xn, pw["w_b"], tm=tm_b, tn=B_Q_LORA + B_KV_LORA, epilogue=_ep_qkv_norms,
                      out_dtypes=[BF16, BF16], out_widths=[B_Q_LORA, B_KV_LORA],
                      extras=(b_q_norm.reshape(1, -1), b_kv_norm.reshape(1, -1)),
                      extra_specs=(vec(B_Q_LORA), vec(B_KV_LORA)))
    tab64 = lambda tm: pl.BlockSpec((tm, LANES), lambda i, j: (i % tps(tm), 0))
    (kpe,) = _matmul("in_proj_kpe", xn, pw["w_kpe"], tm=MM_ROWS, tn=LANES, epilogue=_ep_kpe, out_dtypes=[BF16],
                     extras=(tabs["cos64"], tabs["sin64"]), extra_specs=(tab64(MM_ROWS), tab64(MM_ROWS)))
    (q,) = _matmul("q_up", qn, pw["w_q"], tm=MM_ROWS, tn=B_QK_PAD, epilogue=_ep_q_up, out_dtypes=[BF16],
                   extras=(tabs["cos64"], tabs["sin64"]), extra_specs=(tab64(MM_ROWS), tab64(MM_ROWS)),
                   b_head_major=True)
    k, vv = _matmul("kv_up", kvn, pw["w_kv"], tm=MM_ROWS, tn=B_NOPE + B_VDIM, epilogue=_ep_kv_up,
                    out_dtypes=[BF16, BF16], out_widths=[B_QK_PAD, 2 * B_VDIM],
                    extras=(kpe,), extra_specs=(pl.BlockSpec((MM_ROWS, LANES), lambda i, j: (i, 0)),),
                    b_head_major=True)
    ob = _attention(q, k, vv, batch, seq_len, tq=ATTN_QUERY_ROWS)

    tab128 = pl.BlockSpec((MM_ROWS, LANES), lambda i, j: (i % tps(MM_ROWS), 0))
    q_tiles = C_WIDTH // 512
    (qk,) = _matmul("in_proj_c_qk", xn, pw["w_cqk"], tm=MM_ROWS, tn=512,
                    epilogue=functools.partial(_ep_ret_qk, q_tiles=q_tiles), out_dtypes=[BF16],
                    extras=(tabs["cos256"], tabs["sin256"]), extra_specs=(tab128, tab128))
    (vg,) = _matmul("in_proj_c_vg", xn, pw["w_cvg"], tm=MM_ROWS, tn=512,
                    epilogue=functools.partial(_ep_ret_vg, v_tiles=q_tiles), out_dtypes=[BF16])
    cross_b = _retention_pass(tabs["log_gamma"], qk, vg, None, batch, seq_len, rev=True, final=False)
    oc = _retention_pass(tabs["log_gamma"], qk, vg, cross_b, batch, seq_len, rev=False, final=True)

    (gates,) = _matmul("in_proj_gates", xn, pw["w_g"], tm=MM_ROWS, tn=1024, epilogue=_ep_sigmoid, out_dtypes=[BF16])
    merged = _merge(oa, ob, oc, pw["w_br"], gates)
    res_spec = lambda tm, tn: pl.BlockSpec((tm, tn), lambda i, j: (i, j))
    (x,) = _matmul("out_proj", merged, pw["w_out"], tm=MM_ROWS, tn=512, epilogue=_ep_residual, out_dtypes=[F32],
                   extras=(x,), extra_specs=(res_spec(MM_ROWS, 512),))

    xn2 = _rmsnorm(x, ffn_norm, BF16)
    act = _ffn_up_act(xn2, pw["ffn_up"], ffn_conv, ffn_conv_b, seq_len)
    (x,) = _matmul("ffn_down", act, pw["ffn_down"], tm=512, tn=512, epilogue=_ep_residual, out_dtypes=[F32],
                   extras=(x,), extra_specs=(res_spec(512, 512),))
    return x


def _tables(seq_len):
    c64, s64 = _rope_tables(seq_len, B_ROPE)
    z = jnp.zeros((seq_len, B_ROPE), F32)
    c256, s256 = _rope_tables(seq_len, C_HEAD)
    log_gamma = jnp.log1p(-(2.0 ** (-5.0 - jnp.arange(C_HEADS, dtype=F32))))
    return dict(
        cos64=jnp.concatenate([c64, c64, z], axis=1),
        sin64=jnp.concatenate([-s64, s64, z], axis=1),
        cos256=c256, sin256=s256,
        log_gamma=jnp.broadcast_to(log_gamma[:, None, None], (C_HEADS, SUBLANES, LANES)),
    )


def _encoder(x3, prepped, per_layer, final_norm):
    batch, seq_len, d = x3.shape
    x = x3.reshape(batch * seq_len, d)
    tabs = _tables(seq_len)
    for pw, lw in zip(prepped, per_layer):
        x = _layer(x, batch, seq_len, tabs, pw, *lw)
    return _rmsnorm(x, final_norm, F32).reshape(batch, seq_len, d)


def kernel(x_prompt, x_sample, attn_norm, w_in, a_mu, a_w0, a_w_up, a_a0, a_a_up, a_g_up, a_k_k, a_k_a, a_r_k, a_ln_w, a_ln_b, b_q_norm, b_q_up, b_kv_norm, b_kv_up, w_branch, w_out, ffn_norm, ffn_up, ffn_conv, ffn_conv_b, ffn_down, final_norm):
    depth = w_in.shape[0]
    prepped = [_prep_layer(w_in[l], b_q_up[l], b_kv_up[l], w_branch[l], w_out[l], ffn_up[l], ffn_down[l])
               for l in range(depth)]
    per_layer = [(attn_norm[l], a_mu[l], a_w0[l], a_w_up[l], a_a0[l], a_a_up[l], a_g_up[l], a_k_k[l], a_k_a[l],
                  a_r_k[l], a_ln_w[l], a_ln_b[l], b_q_norm[l], b_kv_norm[l], ffn_norm[l], ffn_conv[l], ffn_conv_b[l])
                 for l in range(depth)]
    return (_encoder(x_prompt, prepped, per_layer, final_norm), _encoder(x_sample, prepped, per_layer, final_norm))
```

```python
import functools
import math

import jax
import jax.numpy as jnp
from jax import lax
from jax.experimental import pallas as pl
from jax.experimental.pallas import tpu as pltpu

F32 = jnp.float32
BF16 = jnp.bfloat16

D_MODEL = 4096
DEPTH = 2
A_HEAD = 64
A_WIDTH = 2048
A_LORA_IN = 384
A_IN = 3 * A_WIDTH + A_LORA_IN
A_GN_EPS = 64e-5
B_NOPE = 128
B_ROPE = 64
B_VDIM = 128
B_HEADS = 16
B_Q_LORA = 1536
B_KV_LORA = 512
B_QK_PAD = 256
C_HEAD = 256
C_HEADS = 8
C_WIDTH = 2048
C_GN_EPS = 1e-5
FFN_HIDDEN = 11008
ROPE_BASE = 10000.0
NORM_EPS = 1e-6

LANES = 128
SUBLANES = 8
VMEM_LIMIT_BYTES = 56 * 2**20
MM_ROWS = 1024

RWKV_CHUNK = 64
RWKV_HEADS_PER_GROUP = 2
RWKV_GROUPS_PER_STEP = 2
RWKV_CHUNKS_PER_STEP = 4
RET_CHUNK = 256
RET_HEADS_PER_STEP = 4


def _params(sem):
    return pltpu.CompilerParams(dimension_semantics=sem, vmem_limit_bytes=VMEM_LIMIT_BYTES)


def _dot(a, b):
    return jnp.dot(a, b, preferred_element_type=F32)


def _dot_nt(a, b):
    return lax.dot_general(a, b, (((1,), (1,)), ((), ())), preferred_element_type=F32)


def _dot_tn(a, b):
    return lax.dot_general(a, b, (((0,), (0,)), ((), ())), preferred_element_type=F32)


def _bf(x):
    return x.astype(BF16)


def _sigmoid(x):
    return 1.0 / (1.0 + jnp.exp(-x))


def _silu(x):
    return x * _sigmoid(x)


def _rmsnorm_body(x_ref, g_ref, o_ref):
    x = x_ref[...]
    y = x * lax.rsqrt(jnp.mean(x * x, axis=-1, keepdims=True) + NORM_EPS)
    o_ref[...] = (y * g_ref[...]).astype(o_ref.dtype)


def _rmsnorm(x, g, out_dtype, tm=256):
    m, d = x.shape
    return pl.pallas_call(
        _rmsnorm_body,
        grid=(m // tm,),
        in_specs=[pl.BlockSpec((tm, d), lambda i: (i, 0)), pl.BlockSpec((1, d), lambda i: (0, 0))],
        out_specs=pl.BlockSpec((tm, d), lambda i: (i, 0)),
        out_shape=jax.ShapeDtypeStruct((m, d), out_dtype),
        compiler_params=_params(("parallel",)),
        name="rmsnorm",
    )(x, g.reshape(1, d))


def _mm_body(a_ref, b_ref, *rest, n_extra, epilogue):
    extras = rest[:n_extra]
    outs = rest[n_extra:]
    acc = _dot(a_ref[...], b_ref[...])
    res = epilogue(acc, *extras)
    for o_ref, r in zip(outs, res):
        o_ref[...] = r.astype(o_ref.dtype)


def _matmul(name, a, b, *, tm, tn, epilogue, out_dtypes, out_widths=None, extras=(), extra_specs=(),
            b_head_major=False):
    m, k = a.shape
    if b_head_major:
        nj = b.shape[0]
        b_spec = pl.BlockSpec((None, k, tn), lambda i, j: (j, 0, 0))
    else:
        nj = b.shape[1] // tn
        b_spec = pl.BlockSpec((k, tn), lambda i, j: (0, j))
    out_widths = out_widths or [tn] * len(out_dtypes)
    if b_head_major:
        out_specs = [pl.BlockSpec((None, tm, w), lambda i, j: (j, i, 0)) for w in out_widths]
        out_shape = [jax.ShapeDtypeStruct((nj, m, w), dt) for w, dt in zip(out_widths, out_dtypes)]
    else:
        out_specs = [pl.BlockSpec((tm, w), lambda i, j: (i, j)) for w in out_widths]
        out_shape = [jax.ShapeDtypeStruct((m, nj * w), dt) for w, dt in zip(out_widths, out_dtypes)]
    return pl.pallas_call(
        functools.partial(_mm_body, n_extra=len(extras), epilogue=epilogue),
        grid=(m // tm, nj),
        in_specs=[pl.BlockSpec((tm, k), lambda i, j: (i, 0)), b_spec, *extra_specs],
        out_specs=out_specs,
        out_shape=out_shape,
        compiler_params=_params(("parallel", "arbitrary")),
        name=name,
    )(a, b, *extras)


def _ep_plain(acc):
    return (acc,)


def _ep_sigmoid(acc):
    return (_sigmoid(acc),)


def _ep_residual(acc, res_ref):
    return (acc + res_ref[...],)


def _ep_qkv_norms(acc, gq_ref, gkv_ref):
    q = acc[:, :B_Q_LORA]
    kv = acc[:, B_Q_LORA:]
    qn = q * lax.rsqrt(jnp.mean(q * q, axis=-1, keepdims=True) + NORM_EPS) * gq_ref[...]
    kvn = kv * lax.rsqrt(jnp.mean(kv * kv, axis=-1, keepdims=True) + NORM_EPS) * gkv_ref[...]
    return qn, kvn


def _rope64_paired(x, cos_ref, sin_ref):
    return x * cos_ref[...] + pltpu.roll(x, 64, axis=1) * sin_ref[...]


def _ep_kpe(acc, cos_ref, sin_ref):
    return (_rope64_paired(acc, cos_ref, sin_ref),)


def _ep_q_up(acc, cos_ref, sin_ref):
    scale = (B_NOPE + B_ROPE) ** -0.5
    return (jnp.concatenate([acc[:, :B_NOPE], _rope64_paired(acc[:, B_NOPE:], cos_ref, sin_ref)], axis=1) * scale,)


def _ep_kv_up(acc, kpe_ref):
    k = jnp.concatenate([acc[:, :B_NOPE].astype(BF16), kpe_ref[...]], axis=1)
    v = acc[:, B_NOPE:]
    return k, jnp.concatenate([v, jnp.ones_like(v)], axis=1)


def _ep_ret_qk(acc, cos_ref, sin_ref, *, q_tiles):
    cos = cos_ref[...]
    sin = sin_ref[...]
    half = C_HEAD // 2
    pieces = []
    for h in range(acc.shape[1] // C_HEAD):
        x1 = acc[:, h * C_HEAD:h * C_HEAD + half]
        x2 = acc[:, h * C_HEAD + half:(h + 1) * C_HEAD]
        pieces += [x1 * cos - x2 * sin, x1 * sin + x2 * cos]
    out = jnp.concatenate(pieces, axis=1)
    scale = jnp.where(pl.program_id(1) < q_tiles, C_HEAD ** -0.5, 1.0)
    return (out * scale,)


def _ep_ret_vg(acc, *, v_tiles):
    return (jnp.where(pl.program_id(1) < v_tiles, acc, _silu(acc)),)


def _merge_body(oa_ref, ob_ref, oc_ref, w_ref, g_ref, out_ref, acc_ref):
    br = pl.program_id(2)
    g = g_ref[...].astype(F32)

    @pl.when(br == 0)
    def _():
        acc_ref[...] = g * _dot(oa_ref[...], w_ref[...])

    @pl.when(br == 1)
    def _():
        acc_ref[...] += g * _dot(ob_ref[...], w_ref[...])

    @pl.when(br == 2)
    def _():
        out_ref[...] = (acc_ref[...] + g * _dot(oc_ref[...], w_ref[...])).astype(out_ref.dtype)


def _merge(oa, ob, oc, w3, gates, tm=MM_ROWS, tn=512):
    m, kb = oa.shape
    n = w3.shape[2]
    nj = n // tn
    o_spec = pl.BlockSpec((tm, kb), lambda i, j, br: (i, 0))
    return pl.pallas_call(
        _merge_body,
        grid=(m // tm, nj, 3),
        in_specs=[o_spec, o_spec, o_spec,
                  pl.BlockSpec((None, kb, tn), lambda i, j, br: (br, 0, j)),
                  pl.BlockSpec((tm, tn), lambda i, j, br: (i, br * nj + j))],
        out_specs=pl.BlockSpec((tm, tn), lambda i, j, br: (i, j)),
        out_shape=jax.ShapeDtypeStruct((m, n), BF16),
        scratch_shapes=[pltpu.VMEM((tm, tn), F32)],
        compiler_params=_params(("parallel", "arbitrary", "arbitrary")),
        name="branch_merge",
    )(oa, ob, oc, w3, gates)


def _halo_specs(tm, w, col_block, n_rows, halo=SUBLANES):
    per_tile = tm // halo
    last = n_rows // halo - 1
    main = pl.BlockSpec((tm, w), lambda i, j: (i, col_block(j)))
    prev = pl.BlockSpec((halo, w), lambda i, j: (jnp.maximum(i * per_tile - 1, 0), col_block(j)))
    nxt = pl.BlockSpec((halo, w), lambda i, j: (jnp.minimum((i + 1) * per_tile, last), col_block(j)))
    return [main, prev, nxt]


def _shifted(x, prev_ref, next_ref, tiles_per_seq):
    tm = x.shape[0]
    i = pl.program_id(0)
    pos = i % tiles_per_seq
    halo = prev_ref.shape[0]
    prev_row = jnp.where(pos == 0, 0.0, prev_ref[halo - 1:halo, :].astype(F32))
    next_row = jnp.where(pos == tiles_per_seq - 1, 0.0, next_ref[0:1, :].astype(F32))
    row = lax.broadcasted_iota(jnp.int32, x.shape, 0)
    xp = jnp.where(row == 0, prev_row, pltpu.roll(x, 1, axis=0))
    xn = jnp.where(row == tm - 1, next_row, pltpu.roll(x, tm - 1, axis=0))
    return xp, xn


def _head_sum64(x):
    w = x.shape[1]
    r = lax.broadcasted_iota(jnp.int32, (w, w), 0) >> 6
    c = lax.broadcasted_iota(jnp.int32, (w, w), 1) >> 6
    ones = jnp.where(r == c, 1.0, 0.0).astype(BF16)
    return _dot(_bf(x), ones)


def _rwkv_pre_body(r_ref, rp_ref, rn_ref, k_ref, kp_ref, kn_ref, v_ref, vp_ref, vn_ref, l_ref, lp_ref, ln_ref,
                   mur_ref, muk_ref, muv_ref, mul_ref, w0_ref, a0_ref, wup_ref, aup_ref, gup_ref,
                   kk_w_ref, ka_ref, rk_ref,
                   r_out, v_out, kk_out, g_out, bonus_out, lw0_out, lw1_out, kz0_out, kz1_out, b0_out, b1_out,
                   *, tiles_per_seq):
    def mix(x_ref, p_ref, n_ref, mu_ref):
        x = x_ref[...]
        xp, xn = _shifted(x, p_ref, n_ref, tiles_per_seq)
        return x + mu_ref[0:1, :] * (xp - x) + mu_ref[1:2, :] * (xn - x)

    r = mix(r_ref, rp_ref, rn_ref, mur_ref)
    k = mix(k_ref, kp_ref, kn_ref, muk_ref)
    v = mix(v_ref, vp_ref, vn_ref, muv_ref)
    lo = mix(l_ref, lp_ref, ln_ref, mul_ref)
    wd = jnp.tanh(lo[:, 0:LANES])
    ad = lo[:, LANES:2 * LANES]
    gd = _sigmoid(lo[:, 2 * LANES:3 * LANES])
    lane = lax.broadcasted_iota(jnp.int32, (1, LANES), 1)

    g_out[...] = _dot(_bf(gd), gup_ref[...])
    kk = k * kk_w_ref[...]
    kk = kk * lax.rsqrt(jnp.maximum(_head_sum64(kk * kk), 1e-24))
    r_out[...] = r.astype(r_out.dtype)
    v_out[...] = v.astype(v_out.dtype)
    kk_out[...] = kk.astype(kk_out.dtype)
    bonus_out[...] = _head_sum64(r * k * rk_ref[...]) * v
    for z, (lw_out, kz_out, b_out) in enumerate(((lw0_out, kz0_out, b0_out), (lw1_out, kz1_out, b1_out))):
        sel = (lane >> 6) == z
        w_pre = w0_ref[z:z + 1, :] + _dot(_bf(jnp.where(sel, wd, 0.0)), wup_ref[...])
        lw_out[...] = -math.exp(-0.5) * _sigmoid(w_pre)
        a = _sigmoid(a0_ref[z:z + 1, :] + _dot(_bf(jnp.where(sel, ad, 0.0)), aup_ref[...]))
        kz_out[...] = (k * (1.0 + (a - 1.0) * ka_ref[...])).astype(kz_out.dtype)
        b_out[...] = (kk * a).astype(b_out.dtype)


def _rwkv_prologue(ha, a_mu, a_w0, a_w_up, a_a0, a_a_up, a_g_up, a_k_k, a_k_a, a_r_k, seq_len, tm=256, tw=512):
    m = ha.shape[0]
    ncb = A_WIDTH // tw
    lora_block = 3 * A_WIDTH // A_LORA_IN
    in_specs = []
    for s in range(3):
        in_specs += _halo_specs(tm, tw, lambda j, s=s: s * ncb + j, m)
    in_specs += _halo_specs(tm, A_LORA_IN, lambda j: lora_block, m)
    for s in range(3):
        in_specs.append(pl.BlockSpec((2, tw), lambda i, j, s=s: (0, s * ncb + j)))
    in_specs.append(pl.BlockSpec((2, A_LORA_IN), lambda i, j: (0, lora_block)))
    vec2 = pl.BlockSpec((2, tw), lambda i, j: (0, j))
    up = pl.BlockSpec((LANES, tw), lambda i, j: (0, j))
    vec1 = pl.BlockSpec((1, tw), lambda i, j: (0, j))
    in_specs += [vec2, vec2, up, up, up, vec1, vec1, vec1]
    out_spec = pl.BlockSpec((tm, tw), lambda i, j: (i, j))
    out_dtypes = [BF16, BF16, BF16, F32, F32, F32, F32, BF16, BF16, BF16, BF16]
    return pl.pallas_call(
        functools.partial(_rwkv_pre_body, tiles_per_seq=seq_len // tm),
        grid=(m // tm, ncb),
        in_specs=in_specs,
        out_specs=[out_spec] * 11,
        out_shape=[jax.ShapeDtypeStruct((m, A_WIDTH), dt) for dt in out_dtypes],
        compiler_params=_params(("parallel", "arbitrary")),
        name="rwkv_prologue",
    )(ha, ha, ha, ha, ha, ha, ha, ha, ha, ha, ha, ha, a_mu, a_mu, a_mu, a_mu, a_w0, a_a0,
      a_w_up.reshape(LANES, A_WIDTH).astype(BF16), a_a_up.reshape(LANES, A_WIDTH).astype(BF16), a_g_up.astype(BF16),
      a_k_k.reshape(1, A_WIDTH), a_k_a.reshape(1, A_WIDTH), a_r_k.reshape(1, A_WIDTH))


def _rwkv_chunk_terms(insts, *, hp):
    c, w = insts[0][1].shape
    rows = c * hp
    n = len(insts)
    revs = [inst[0] for inst in insts]
    ti = lax.broadcasted_iota(jnp.int32, (c, c), 0)
    si = lax.broadcasted_iota(jnp.int32, (c, c), 1)
    tri = {False: jnp.where(si <= ti, 1.0, 0.0).astype(BF16), True: jnp.where(si >= ti, 1.0, 0.0).astype(BF16)}
    rt = lax.broadcasted_iota(jnp.int32, (rows, rows), 0)
    cs = lax.broadcasted_iota(jnp.int32, (rows, rows), 1)
    eye = jnp.where(rt == cs, 1.0, 0.0)
    rt = rt & (c - 1)
    cs = cs & (c - 1)
    strict = {False: cs < rt, True: cs > rt}
    incl = {False: cs <= rt, True: cs >= rt}
    lane_head = lax.broadcasted_iota(jnp.int32, (c, w), 1) >> 6
    ew = lax.broadcasted_iota(jnp.int32, (w, w), 0) == lax.broadcasted_iota(jnp.int32, (w, w), 1)

    def stack_f32(x):
        return jnp.concatenate([jnp.where(lane_head == h, x, 0.0) for h in range(hp)], axis=0)

    def stack(x):
        return _bf(stack_f32(x))

    cums = []
    for rev, _, _, _, lw, _, _ in insts:
        hi = lw.astype(BF16)
        mid = (lw - hi.astype(F32)).astype(BF16)
        lo = (lw - hi.astype(F32) - mid.astype(F32)).astype(BF16)
        parts = _dot(tri[rev], jnp.concatenate([hi, mid, lo], axis=1))
        cums.append(parts[:, :w] + parts[:, w:2 * w] + parts[:, 2 * w:])
    tots = [cum[0:1, :] if rev else cum[c - 1:c, :] for rev, cum in zip(revs, cums)]

    xs, rs_f, bs, ks, bh, kh, vs = [], [], [], [], [], [], []
    for (rev, r, v, kk, lw, kz, b), cum, tot in zip(insts, cums, tots):
        r, kk, kz, b = (x.astype(F32) for x in (r, kk, kz, b))
        e_neg = jnp.exp(-cum)
        e_rest = jnp.exp(tot - cum)
        xs.append(stack(kk * jnp.exp(cum - lw)))
        rs_f.append(stack_f32(r * jnp.exp(cum)))
        bs.append(stack(b * e_neg))
        ks.append(stack(kz * e_neg))
        bh.append(stack(b * e_rest))
        kh.append(stack(kz * e_rest))
        vs.append(stack(v))

    grams = [_dot_nt(jnp.concatenate([xs[i], _bf(rs_f[i])], axis=0), jnp.concatenate([bs[i], ks[i]], axis=0))
             for i in range(n)]
    l_b = [jnp.where(strict[revs[i]], grams[i][:rows, :rows], 0.0) for i in range(n)]
    l_k = [_bf(jnp.where(strict[revs[i]], grams[i][:rows, rows:], 0.0)) for i in range(n)]
    incl2 = {rev: jnp.concatenate([m, m], axis=1) for rev, m in incl.items()}
    m_kb = [_bf(jnp.where(incl2[revs[i]], grams[i][rows:, :], 0.0)) for i in range(n)]
    lk_v = [_dot(l_k[i], vs[i]) for i in range(n)]

    t_inv = [eye - l_b[i] for i in range(n)]
    lb = [_bf(x) for x in l_b]
    l_pow = [_dot(x, x) for x in lb]
    for _ in range(int(math.log2(c)) - 2):
        lp = [_bf(x) for x in l_pow]
        both = [_dot(lp[i], jnp.concatenate([_bf(t_inv[i]), lp[i]], axis=1)) for i in range(n)]
        t_inv = [t_inv[i] + both[i][:, :rows] for i in range(n)]
        l_pow = [both[i][:, rows:] for i in range(n)]
    t_inv = [t_inv[i] + _dot(_bf(l_pow[i]), _bf(t_inv[i])) for i in range(n)]

    gh = [_bf(_dot(_bf(t_inv[i]), jnp.concatenate([xs[i], _bf(lk_v[i])], axis=1))) for i in range(n)]
    zero = jnp.zeros((rows, w), BF16)
    lhs = [jnp.concatenate([m_kb[i], jnp.concatenate([bh[i].T, kh[i].T], axis=1)], axis=0) for i in range(n)]
    rhs = [jnp.concatenate([-gh[i], jnp.concatenate([zero, vs[i]], axis=1)], axis=0) for i in range(n)]
    terms = [_dot(lhs[i], rhs[i]) for i in range(n)]
    out = []
    for i in range(n):
        q_p = rs_f[i] + terms[i][:rows, :w]
        y0 = terms[i][:rows, w:]
        phi = jnp.where(ew, jnp.exp(tots[i]), 0.0) + terms[i][rows:, :w]
        psi = terms[i][rows:, w:]
        out.append((_bf(q_p), y0, _bf(phi), psi))
    return out


def _rwkv_scan_body(rf_ref, vf_ref, kkf_ref, lwf_ref, kzf_ref, bf_ref, rb_ref, vb_ref, kkb_ref, lwb_ref, kzb_ref,
                    bb_ref, yf_ref, yb_ref, state_ref, *, hp, gp, cps, chunk):
    c = chunk
    w = A_HEAD * hp

    @pl.when(pl.program_id(2) == 0)
    def _():
        state_ref[...] = jnp.zeros_like(state_ref)

    dirs = ((False, (rf_ref, vf_ref, kkf_ref, lwf_ref, kzf_ref, bf_ref), yf_ref),
            (True, (rb_ref, vb_ref, kkb_ref, lwb_ref, kzb_ref, bb_ref), yb_ref))
    keys = [(d, gi, ci) for d in range(2) for gi in range(gp) for ci in range(cps)]
    insts = [(dirs[d][0], *[x[ci * c:(ci + 1) * c, gi * w:(gi + 1) * w] for x in dirs[d][1]]) for d, gi, ci in keys]
    terms = dict(zip(keys, _rwkv_chunk_terms(insts, hp=hp)))
    chains = [(d, gi) for d in range(2) for gi in range(gp)]
    states = {k: state_ref[k[0], k[1]] for k in chains}
    for step in range(cps):
        for d, gi in chains:
            ci = cps - 1 - step if dirs[d][0] else step
            q_p, y0, phi, psi = terms[(d, gi, ci)]
            a_b = _bf(states[(d, gi)])
            y_st = _dot(q_p, a_b) + y0
            states[(d, gi)] = _dot(phi, a_b) + psi
            y = y_st[0:c, :]
            for h in range(1, hp):
                y = y + y_st[h * c:(h + 1) * c, :]
            dirs[d][2][ci * c:(ci + 1) * c, gi * w:(gi + 1) * w] = y
    for d, gi in chains:
        state_ref[d, gi] = states[(d, gi)]


def _rwkv_scan(r, v, kk, lw0, kz0, b0, lw1, kz1, b1, batch, seq_len, hp=RWKV_HEADS_PER_GROUP,
               gp=RWKV_GROUPS_PER_STEP, cps=RWKV_CHUNKS_PER_STEP, chunk=RWKV_CHUNK):
    m = r.shape[0]
    w = A_HEAD * hp
    rows = chunk * cps
    ns = seq_len // rows
    fwd = pl.BlockSpec((rows, w * gp), lambda bi, g, n: (bi * ns + n, g))
    bwd = pl.BlockSpec((rows, w * gp), lambda bi, g, n: (bi * ns + ns - 1 - n, g))
    out = jax.ShapeDtypeStruct((m, A_WIDTH), F32)
    return pl.pallas_call(
        functools.partial(_rwkv_scan_body, hp=hp, gp=gp, cps=cps, chunk=chunk),
        grid=(batch, A_WIDTH // (w * gp), ns),
        in_specs=[fwd] * 6 + [bwd] * 6,
        out_specs=[fwd, bwd],
        out_shape=[out, out],
        scratch_shapes=[pltpu.VMEM((2, gp, w, w), F32)],
        compiler_params=_params(("parallel", "parallel", "arbitrary")),
        name="rwkv_scan",
    )(r, v, kk, lw0, kz0, b0, r, v, kk, lw1, kz1, b1)


def _rwkv_post_body(yf_ref, yb_ref, bonus_ref, g_ref, lnw_ref, lnb_ref, o_ref):
    y = yf_ref[...] + yb_ref[...]
    inv_n = 1.0 / A_HEAD
    mu = _head_sum64(y) * inv_n
    d = y - mu
    var = _head_sum64(d * d) * inv_n
    yn = d * lax.rsqrt(var + A_GN_EPS) * lnw_ref[...] + lnb_ref[...]
    o_ref[...] = ((yn + bonus_ref[...]) * g_ref[...]).astype(o_ref.dtype)


def _rwkv_epilogue(yf, yb, bonus, g, ln_w, ln_b, tm=256, tw=512):
    m = yf.shape[0]
    spec = pl.BlockSpec((tm, tw), lambda i, j: (i, j))
    vec = pl.BlockSpec((1, tw), lambda i, j: (0, j))
    return pl.pallas_call(
        _rwkv_post_body,
        grid=(m // tm, A_WIDTH // tw),
        in_specs=[spec, spec, spec, spec, vec, vec],
        out_specs=spec,
        out_shape=jax.ShapeDtypeStruct((m, A_WIDTH), BF16),
        compiler_params=_params(("parallel", "parallel")),
        name="rwkv_epilogue",
    )(yf, yb, bonus, g, ln_w.reshape(1, A_WIDTH), ln_b.reshape(1, A_WIDTH))


ATTN_QUERY_ROWS = 1024
ATTN_SUBTILES = 4
ATTN_LOOKAHEAD = 2


def _attn_body(q_ref, k_ref, v_ref, o_ref):
    sub = q_ref.shape[0] // ATTN_SUBTILES
    k = k_ref[...]
    v = v_ref[...]

    def scores(i):
        return _dot_nt(q_ref[i * sub:(i + 1) * sub, :], k)

    s = {i: scores(i) for i in range(ATTN_LOOKAHEAD)}
    for i in range(ATTN_SUBTILES):
        si = s.pop(i)
        p = jnp.exp(_bf(si - jnp.max(si, axis=-1, keepdims=True)))
        if i + ATTN_LOOKAHEAD < ATTN_SUBTILES:
            s[i + ATTN_LOOKAHEAD] = scores(i + ATTN_LOOKAHEAD)
        o = _dot(p, v)
        o_ref[i * sub:(i + 1) * sub, :] = (o[:, :B_VDIM] / o[:, B_VDIM:]).astype(o_ref.dtype)


def _attention(q, k, v, batch, seq_len, tq):
    m = q.shape[1]
    nq = seq_len // tq
    return pl.pallas_call(
        _attn_body,
        grid=(B_HEADS, batch, nq),
        in_specs=[pl.BlockSpec((None, tq, B_QK_PAD), lambda h, bi, qi: (h, bi * nq + qi, 0)),
                  pl.BlockSpec((None, seq_len, B_QK_PAD), lambda h, bi, qi: (h, bi, 0)),
                  pl.BlockSpec((None, seq_len, 2 * B_VDIM), lambda h, bi, qi: (h, bi, 0))],
        out_specs=pl.BlockSpec((tq, B_VDIM), lambda h, bi, qi: (bi * nq + qi, h)),
        out_shape=jax.ShapeDtypeStruct((m, B_HEADS * B_VDIM), BF16),
        compiler_params=_params(("parallel", "parallel", "arbitrary")),
        name="mla_attention",
    )(q, k, v)


def _ret_body(lg_ref, q_ref, k_ref, v_ref, *rest, rev, final, chunk, hps):
    if final:
        other_ref, gate_ref, o_ref, state_ref = rest
    else:
        o_ref, state_ref = rest
    c = chunk
    heads = range(hps)
    cols = [slice(h * C_HEAD, (h + 1) * C_HEAD) for h in heads]

    @pl.when(pl.program_id(2) == 0)
    def _():
        state_ref[...] = jnp.zeros_like(state_ref)

    lg = [lg_ref[h, 0:1, 0:1] for h in heads]
    q = [q_ref[:, cols[h]] for h in heads]
    k = [k_ref[:, cols[h]] for h in heads]
    v = [v_ref[:, cols[h]] for h in heads]
    idx = lax.broadcasted_iota(jnp.int32, (c, 1), 0).astype(F32)
    if rev:
        q_pow, k_pow = c - idx, idx
    else:
        q_pow, k_pow = idx + 1.0, c - 1.0 - idx
    qd = [_bf(q[h].astype(F32) * jnp.exp(lg[h] * q_pow)) for h in heads]
    kd = [_bf(k[h].astype(F32) * jnp.exp(lg[h] * k_pow)) for h in heads]
    state = [state_ref[h] for h in heads]
    cross = [_dot(qd[h], _bf(state[h])) for h in heads]
    kv = [_dot_tn(kd[h], v[h]) for h in heads]
    for h in heads:
        state_ref[h] = state[h] * jnp.exp(lg[h] * float(c)) + kv[h]
    if final:
        ti = lax.broadcasted_iota(jnp.int32, (c, c), 0)
        si = lax.broadcasted_iota(jnp.int32, (c, c), 1)
        dist = jnp.abs(ti - si).astype(F32)
        s = [_dot_nt(q[h], k[h]) for h in heads]
        inner = [_dot(_bf(s[h] * jnp.exp(lg[h] * dist)), v[h]) for h in heads]
        for h in heads:
            o = inner[h] + cross[h] + other_ref[:, cols[h]]
            mu = jnp.mean(o, axis=-1, keepdims=True)
            d = o - mu
            var = jnp.mean(d * d, axis=-1, keepdims=True)
            o_ref[:, cols[h]] = (gate_ref[:, cols[h]].astype(F32) * (d * lax.rsqrt(var + C_GN_EPS))).astype(o_ref.dtype)
    else:
        for h in heads:
            o_ref[:, cols[h]] = cross[h]


def _retention_pass(log_gamma, qk, vg, other, batch, seq_len, rev, final, chunk=RET_CHUNK, hps=RET_HEADS_PER_STEP):
    m = qk.shape[0]
    nc = seq_len // chunk
    ng = C_HEADS // hps
    if rev:
        row = lambda bi, n: bi * nc + nc - 1 - n
    else:
        row = lambda bi, n: bi * nc + n
    blk = lambda off: pl.BlockSpec((chunk, C_HEAD * hps), lambda g, bi, n: (row(bi, n), g + off))
    in_specs = [pl.BlockSpec((hps, SUBLANES, LANES), lambda g, bi, n: (g, 0, 0)), blk(0), blk(ng), blk(0)]
    args = [log_gamma, qk, qk, vg]
    if final:
        in_specs += [blk(0), blk(ng)]
        args += [other, vg]
    return pl.pallas_call(
        functools.partial(_ret_body, rev=rev, final=final, chunk=chunk, hps=hps),
        grid=(ng, batch, nc),
        in_specs=in_specs,
        out_specs=blk(0),
        out_shape=jax.ShapeDtypeStruct((m, C_WIDTH), BF16 if final else F32),
        scratch_shapes=[pltpu.VMEM((hps, C_HEAD, C_HEAD), F32)],
        compiler_params=_params(("parallel", "parallel", "arbitrary")),
        name="retention_final" if final else "retention_cross",
    )(*args)


def _ffn_up_body(x_ref, xp_ref, xn_ref, wg_ref, wv_ref, cw_ref, cb_ref, o_ref, xext_ref, *, tiles_per_seq):
    tm = x_ref.shape[0]
    halo = xp_ref.shape[0]

    @pl.when(pl.program_id(1) == 0)
    def _():
        pos = pl.program_id(0) % tiles_per_seq
        xext_ref[0:halo, :] = jnp.where(pos == 0, jnp.zeros_like(xp_ref[...]), xp_ref[...])
        xext_ref[halo:halo + tm, :] = x_ref[...]
        xext_ref[halo + tm:, :] = jnp.where(pos == tiles_per_seq - 1, jnp.zeros_like(xn_ref[...]), xn_ref[...])

    g_ext = _dot(xext_ref[...], wg_ref[...])
    val = _dot(xext_ref[halo:halo + tm, :], wv_ref[...])
    rows = tm + 2 * halo
    g = g_ext[halo:halo + tm]
    gp = pltpu.roll(g_ext, 1, axis=0)[halo:halo + tm]
    gn = pltpu.roll(g_ext, rows - 1, axis=0)[halo:halo + tm]
    conv = cw_ref[0:1, :] * gp + cw_ref[1:2, :] * g + cw_ref[2:3, :] * gn + cb_ref[...]
    o_ref[...] = (_silu(conv) * val).astype(o_ref.dtype)


def _ffn_up_act(xn, w_up, conv_w, conv_b, seq_len, tm=MM_ROWS, tn=256):
    m, k = xn.shape
    nj = FFN_HIDDEN // tn
    halo = 2 * SUBLANES
    return pl.pallas_call(
        functools.partial(_ffn_up_body, tiles_per_seq=seq_len // tm),
        grid=(m // tm, nj),
        in_specs=[*_halo_specs(tm, k, lambda j: 0, m, halo=halo),
                  pl.BlockSpec((k, tn), lambda i, j: (0, j)),
                  pl.BlockSpec((k, tn), lambda i, j: (0, nj + j)),
                  pl.BlockSpec((3, tn), lambda i, j: (0, j)),
                  pl.BlockSpec((1, tn), lambda i, j: (0, j))],
        out_specs=pl.BlockSpec((tm, tn), lambda i, j: (i, j)),
        out_shape=jax.ShapeDtypeStruct((m, FFN_HIDDEN), BF16),
        scratch_shapes=[pltpu.VMEM((tm + 2 * halo, k), BF16)],
        compiler_params=_params(("parallel", "arbitrary")),
        name="ffn_up_conv_gate",
    )(xn, xn, xn, w_up, w_up, conv_w, conv_b.reshape(1, FFN_HIDDEN))


def _rope_tables(seq_len, dim):
    inv = ROPE_BASE ** (-jnp.arange(0, dim, 2, dtype=F32) / dim)
    ang = jnp.arange(seq_len, dtype=F32)[:, None] * inv[None, :]
    return jnp.cos(ang), jnp.sin(ang)


def _swap_halves_cols(w):
    h = w.shape[-1] // 2
    return jnp.concatenate([w[..., h:], w[..., :h]], axis=-1)


def _prep_layer(w_in, b_q_up, b_kv_up, w_branch, w_out, ffn_up, ffn_down):
    o_b = A_IN
    o_kpe = o_b + B_Q_LORA + B_KV_LORA
    o_c = o_kpe + B_ROPE
    o_g = o_c + 4 * C_WIDTH
    w_kpe = w_in[:, o_kpe:o_c]
    qh = b_q_up.reshape(B_Q_LORA, B_HEADS, B_NOPE + B_ROPE)
    q_pe = qh[..., B_NOPE:]
    w_q = jnp.concatenate([qh[..., :B_NOPE], q_pe, _swap_halves_cols(q_pe)], axis=-1)
    return dict(
        w_a=jnp.pad(w_in[:, :A_IN], ((0, 0), (0, LANES))).astype(BF16),
        w_b=w_in[:, o_b:o_kpe].astype(BF16),
        w_kpe=jnp.concatenate([w_kpe, _swap_halves_cols(w_kpe)], axis=-1).astype(BF16),
        w_cqk=w_in[:, o_c:o_c + 2 * C_WIDTH].astype(BF16),
        w_cvg=w_in[:, o_c + 2 * C_WIDTH:o_g].astype(BF16),
        w_g=w_in[:, o_g:].astype(BF16),
        w_q=jnp.transpose(w_q, (1, 0, 2)).astype(BF16),
        w_kv=jnp.transpose(b_kv_up.reshape(B_KV_LORA, B_HEADS, B_NOPE + B_VDIM), (1, 0, 2)).astype(BF16),
        w_br=w_branch.reshape(3, A_WIDTH, D_MODEL).astype(BF16),
        w_out=w_out.astype(BF16),
        ffn_up=ffn_up.astype(BF16),
        ffn_down=ffn_down.astype(BF16),
    )


def _layer(x, batch, seq_len, tabs, pw, attn_norm, a_mu, a_w0, a_w_up, a_a0, a_a_up, a_g_up, a_k_k, a_k_a, a_r_k,
           a_ln_w, a_ln_b, b_q_norm, b_kv_norm, ffn_norm, ffn_conv, ffn_conv_b):
    m = x.shape[0]
    tps = lambda tm: seq_len // tm
    xn = _rmsnorm(x, attn_norm, BF16)

    (ha,) = _matmul("in_proj_a", xn, pw["w_a"], tm=MM_ROWS, tn=512, epilogue=_ep_plain, out_dtypes=[F32])
    r, v, kk, g, bonus, lw0, lw1, kz0, kz1, b0, b1 = _rwkv_prologue(
        ha, a_mu, a_w0, a_w_up, a_a0, a_a_up, a_g_up, a_k_k, a_k_a, a_r_k, seq_len)
    yf, yb = _rwkv_scan(r, v, kk, lw0, kz0, b0, lw1, kz1, b1, batch, seq_len)
    oa = _rwkv_epilogue(yf, yb, bonus, g, a_ln_w, a_ln_b)

    tm_b = 512
    vec = lambda n: pl.BlockSpec((1, n), lambda i, j: (0, 0))
    qn, kvn = _matmul("in_proj_b", xn, pw["w_b"], tm=tm_b, tn=B_Q_LORA + B_KV_LORA, epilogue=_ep_qkv_norms,
                      out_dtypes=[BF16, BF16], out_widths=[B_Q_LORA, B_KV_LORA],
                      extras=(b_q_norm.reshape(1, -1), b_kv_norm.reshape(1, -1)),
                      extra_specs=(vec(B_Q_LORA), vec(B_KV_LORA)))
    tab64 = lambda tm: pl.BlockSpec((tm, LANES), lambda i, j: (i % tps(tm), 0))
    (kpe,) = _matmul("in_proj_kpe", xn, pw["w_kpe"], tm=MM_ROWS, tn=LANES, epilogue=_ep_kpe, out_dtypes=[BF16],
                     extras=(tabs["cos64"], tabs["sin64"]), extra_specs=(tab64(MM_ROWS), tab64(MM_ROWS)))
    (q,) = _matmul("q_up", qn, pw["w_q"], tm=MM_ROWS, tn=B_QK_PAD, epilogue=_ep_q_up, out_dtypes=[BF16],
                   extras=(tabs["cos64"], tabs["sin64"]), extra_specs=(tab64(MM_ROWS), tab64(MM_ROWS)),
                   b_head_major=True)
    k, vv = _matmul("kv_up", kvn, pw["w_kv"], tm=MM_ROWS, tn=B_NOPE + B_VDIM, epilogue=_ep_kv_up,
                    out_dtypes=[BF16, BF16], out_widths=[B_QK_PAD, 2 * B_VDIM],
                    extras=(kpe,), extra_specs=(pl.BlockSpec((MM_ROWS, LANES), lambda i, j: (i, 0)),),
                    b_head_major=True)
    ob = _attention(q, k, vv, batch, seq_len, tq=ATTN_QUERY_ROWS)

    tab128 = pl.BlockSpec((MM_ROWS, LANES), lambda i, j: (i % tps(MM_ROWS), 0))
    q_tiles = C_WIDTH // 512
    (qk,) = _matmul("in_proj_c_qk", xn, pw["w_cqk"], tm=MM_ROWS, tn=512,
                    epilogue=functools.partial(_ep_ret_qk, q_tiles=q_tiles), out_dtypes=[BF16],
                    extras=(tabs["cos256"], tabs["sin256"]), extra_specs=(tab128, tab128))
    (vg,) = _matmul("in_proj_c_vg", xn, pw["w_cvg"], tm=MM_ROWS, tn=512,
                    epilogue=functools.partial(_ep_ret_vg, v_tiles=q_tiles), out_dtypes=[BF16])
    cross_b = _retention_pass(tabs["log_gamma"], qk, vg, None, batch, seq_len, rev=True, final=False)
    oc = _retention_pass(tabs["log_gamma"], qk, vg, cross_b, batch, seq_len, rev=False, final=True)

    (gates,) = _matmul("in_proj_gates", xn, pw["w_g"], tm=MM_ROWS, tn=1024, epilogue=_ep_sigmoid, out_dtypes=[BF16])
    merged = _merge(oa, ob, oc, pw["w_br"], gates)
    res_spec = lambda tm, tn: pl.BlockSpec((tm, tn), lambda i, j: (i, j))
    (x,) = _matmul("out_proj", merged, pw["w_out"], tm=MM_ROWS, tn=512, epilogue=_ep_residual, out_dtypes=[F32],
                   extras=(x,), extra_specs=(res_spec(MM_ROWS, 512),))

    xn2 = _rmsnorm(x, ffn_norm, BF16)
    act = _ffn_up_act(xn2, pw["ffn_up"], ffn_conv, ffn_conv_b, seq_len)
    (x,) = _matmul("ffn_down", act, pw["ffn_down"], tm=512, tn=512, epilogue=_ep_residual, out_dtypes=[F32],
                   extras=(x,), extra_specs=(res_spec(512, 512),))
    return x


def _tables(seq_len):
    c64, s64 = _rope_tables(seq_len, B_ROPE)
    z = jnp.zeros((seq_len, B_ROPE), F32)
    c256, s256 = _rope_tables(seq_len, C_HEAD)
    log_gamma = jnp.log1p(-(2.0 ** (-5.0 - jnp.arange(C_HEADS, dtype=F32))))
    return dict(
        cos64=jnp.concatenate([c64, c64, z], axis=1),
        sin64=jnp.concatenate([-s64, s64, z], axis=1),
        cos256=c256, sin256=s256,
        log_gamma=jnp.broadcast_to(log_gamma[:, None, None], (C_HEADS, SUBLANES, LANES)),
    )


def _encoder(x3, prepped, per_layer, final_norm):
    batch, seq_len, d = x3.shape
    x = x3.reshape(batch * seq_len, d)
    tabs = _tables(seq_len)
    for pw, lw in zip(prepped, per_layer):
        x = _layer(x, batch, seq_len, tabs, pw, *lw)
    return _rmsnorm(x, final_norm, F32).reshape(batch, seq_len, d)


def kernel(x_prompt, x_sample, attn_norm, w_in, a_mu, a_w0, a_w_up, a_a0, a_a_up, a_g_up, a_k_k, a_k_a, a_r_k, a_ln_w, a_ln_b, b_q_norm, b_q_up, b_kv_norm, b_kv_up, w_branch, w_out, ffn_norm, ffn_up, ffn_conv, ffn_conv_b, ffn_down, final_norm):
    depth = w_in.shape[0]
    prepped = [_prep_layer(w_in[l], b_q_up[l], b_kv_up[l], w_branch[l], w_out[l], ffn_up[l], ffn_down[l])
               for l in range(depth)]
    per_layer = [(attn_norm[l], a_mu[l], a_w0[l], a_w_up[l], a_a0[l], a_a_up[l], a_g_up[l], a_k_k[l], a_k_a[l],
                  a_r_k[l], a_ln_w[l], a_ln_b[l], b_q_norm[l], b_kv_norm[l], ffn_norm[l], ffn_conv[l], ffn_conv_b[l])
                 for l in range(depth)]
    return (_encoder(x_prompt, prepped, per_layer, final_norm), _encoder(x_sample, prepped, per_layer, final_norm))
```

```python
import functools
import math

import jax
import jax.numpy as jnp
from jax import lax
from jax.experimental import pallas as pl
from jax.experimental.pallas import tpu as pltpu

F32 = jnp.float32
BF16 = jnp.bfloat16

D_MODEL = 4096
DEPTH = 2
A_HEAD = 64
A_WIDTH = 2048
A_LORA_IN = 384
A_IN = 3 * A_WIDTH + A_LORA_IN
A_GN_EPS = 64e-5
B_NOPE = 128
B_ROPE = 64
B_VDIM = 128
B_HEADS = 16
B_Q_LORA = 1536
B_KV_LORA = 512
B_QK_PAD = 256
C_HEAD = 256
C_HEADS = 8
C_WIDTH = 2048
C_GN_EPS = 1e-5
FFN_HIDDEN = 11008
ROPE_BASE = 10000.0
NORM_EPS = 1e-6

LANES = 128
SUBLANES = 8
VMEM_LIMIT_BYTES = 56 * 2**20
MM_ROWS = 1024

RWKV_CHUNK = 64
RWKV_HEADS_PER_GROUP = 2
RWKV_GROUPS_PER_STEP = 2
RWKV_CHUNKS_PER_STEP = 4
RET_CHUNK = 256
RET_HEADS_PER_STEP = 4


def _params(sem):
    return pltpu.CompilerParams(dimension_semantics=sem, vmem_limit_bytes=VMEM_LIMIT_BYTES)


def _dot(a, b):
    return jnp.dot(a, b, preferred_element_type=F32)


def _dot_nt(a, b):
    return lax.dot_general(a, b, (((1,), (1,)), ((), ())), preferred_element_type=F32)


def _dot_tn(a, b):
    return lax.dot_general(a, b, (((0,), (0,)), ((), ())), preferred_element_type=F32)


def _bf(x):
    return x.astype(BF16)


def _sigmoid(x):
    return 1.0 / (1.0 + jnp.exp(-x))


def _silu(x):
    return x * _sigmoid(x)


def _rmsnorm_body(x_ref, g_ref, o_ref):
    x = x_ref[...]
    y = x * lax.rsqrt(jnp.mean(x * x, axis=-1, keepdims=True) + NORM_EPS)
    o_ref[...] = (y * g_ref[...]).astype(o_ref.dtype)


def _rmsnorm(x, g, out_dtype, tm=256):
    m, d = x.shape
    return pl.pallas_call(
        _rmsnorm_body,
        grid=(m // tm,),
        in_specs=[pl.BlockSpec((tm, d), lambda i: (i, 0)), pl.BlockSpec((1, d), lambda i: (0, 0))],
        out_specs=pl.BlockSpec((tm, d), lambda i: (i, 0)),
        out_shape=jax.ShapeDtypeStruct((m, d), out_dtype),
        compiler_params=_params(("parallel",)),
        name="rmsnorm",
    )(x, g.reshape(1, d))


def _mm_body(a_ref, b_ref, *rest, n_extra, epilogue):
    extras = rest[:n_extra]
    outs = rest[n_extra:]
    acc = _dot(a_ref[...], b_ref[...])
    res = epilogue(acc, *extras)
    for o_ref, r in zip(outs, res):
        o_ref[...] = r.astype(o_ref.dtype)


def _matmul(name, a, b, *, tm, tn, epilogue, out_dtypes, out_widths=None, extras=(), extra_specs=(),
            b_head_major=False):
    m, k = a.shape
    if b_head_major:
        nj = b.shape[0]
        b_spec = pl.BlockSpec((None, k, tn), lambda i, j: (j, 0, 0))
    else:
        nj = b.shape[1] // tn
        b_spec = pl.BlockSpec((k, tn), lambda i, j: (0, j))
    out_widths = out_widths or [tn] * len(out_dtypes)
    if b_head_major:
        out_specs = [pl.BlockSpec((None, tm, w), lambda i, j: (j, i, 0)) for w in out_widths]
        out_shape = [jax.ShapeDtypeStruct((nj, m, w), dt) for w, dt in zip(out_widths, out_dtypes)]
    else:
        out_specs = [pl.BlockSpec((tm, w), lambda i, j: (i, j)) for w in out_widths]
        out_shape = [jax.ShapeDtypeStruct((m, nj * w), dt) for w, dt in zip(out_widths, out_dtypes)]
    return pl.pallas_call(
        functools.partial(_mm_body, n_extra=len(extras), epilogue=epilogue),
        grid=(m // tm, nj),
        in_specs=[pl.BlockSpec((tm, k), lambda i, j: (i, 0)), b_spec, *extra_specs],
        out_specs=out_specs,
        out_shape=out_shape,
        compiler_params=_params(("parallel", "arbitrary")),
        name=name,
    )(a, b, *extras)


def _ep_plain(acc):
    return (acc,)


def _ep_sigmoid(acc):
    return (_sigmoid(acc),)


def _ep_residual(acc, res_ref):
    return (acc + res_ref[...],)


def _ep_qkv_norms(acc, gq_ref, gkv_ref):
    q = acc[:, :B_Q_LORA]
    kv = acc[:, B_Q_LORA:]
    qn = q * lax.rsqrt(jnp.mean(q * q, axis=-1, keepdims=True) + NORM_EPS) * gq_ref[...]
    kvn = kv * lax.rsqrt(jnp.mean(kv * kv, axis=-1, keepdims=True) + NORM_EPS) * gkv_ref[...]
    return qn, kvn


def _rope64_paired(x, cos_ref, sin_ref):
    return x * cos_ref[...] + pltpu.roll(x, 64, axis=1) * sin_ref[...]


def _ep_kpe(acc, cos_ref, sin_ref):
    return (_rope64_paired(acc, cos_ref, sin_ref),)


def _ep_q_up(acc, cos_ref, sin_ref):
    scale = (B_NOPE + B_ROPE) ** -0.5
    return (jnp.concatenate([acc[:, :B_NOPE], _rope64_paired(acc[:, B_NOPE:], cos_ref, sin_ref)], axis=1) * scale,)


def _ep_kv_up(acc, kpe_ref):
    k = jnp.concatenate([acc[:, :B_NOPE].astype(BF16), kpe_ref[...]], axis=1)
    v = acc[:, B_NOPE:]
    return k, jnp.concatenate([v, jnp.ones_like(v)], axis=1)


def _ep_ret_qk(acc, cos_ref, sin_ref, *, q_tiles):
    cos = cos_ref[...]
    sin = sin_ref[...]
    half = C_HEAD // 2
    pieces = []
    for h in range(acc.shape[1] // C_HEAD):
        x1 = acc[:, h * C_HEAD:h * C_HEAD + half]
        x2 = acc[:, h * C_HEAD + half:(h + 1) * C_HEAD]
        pieces += [x1 * cos - x2 * sin, x1 * sin + x2 * cos]
    out = jnp.concatenate(pieces, axis=1)
    scale = jnp.where(pl.program_id(1) < q_tiles, C_HEAD ** -0.5, 1.0)
    return (out * scale,)


def _ep_ret_vg(acc, *, v_tiles):
    return (jnp.where(pl.program_id(1) < v_tiles, acc, _silu(acc)),)


def _merge_body(oa_ref, ob_ref, oc_ref, w_ref, g_ref, out_ref, acc_ref):
    br = pl.program_id(2)
    g = g_ref[...].astype(F32)

    @pl.when(br == 0)
    def _():
        acc_ref[...] = g * _dot(oa_ref[...], w_ref[...])

    @pl.when(br == 1)
    def _():
        acc_ref[...] += g * _dot(ob_ref[...], w_ref[...])

    @pl.when(br == 2)
    def _():
        out_ref[...] = (acc_ref[...] + g * _dot(oc_ref[...], w_ref[...])).astype(out_ref.dtype)


def _merge(oa, ob, oc, w3, gates, tm=MM_ROWS, tn=512):
    m, kb = oa.shape
    n = w3.shape[2]
    nj = n // tn
    o_spec = pl.BlockSpec((tm, kb), lambda i, j, br: (i, 0))
    return pl.pallas_call(
        _merge_body,
        grid=(m // tm, nj, 3),
        in_specs=[o_spec, o_spec, o_spec,
                  pl.BlockSpec((None, kb, tn), lambda i, j, br: (br, 0, j)),
                  pl.BlockSpec((tm, tn), lambda i, j, br: (i, br * nj + j))],
        out_specs=pl.BlockSpec((tm, tn), lambda i, j, br: (i, j)),
        out_shape=jax.ShapeDtypeStruct((m, n), BF16),
        scratch_shapes=[pltpu.VMEM((tm, tn), F32)],
        compiler_params=_params(("parallel", "arbitrary", "arbitrary")),
        name="branch_merge",
    )(oa, ob, oc, w3, gates)


def _halo_specs(tm, w, col_block, n_rows, halo=SUBLANES):
    per_tile = tm // halo
    last = n_rows // halo - 1
    main = pl.BlockSpec((tm, w), lambda i, j: (i, col_block(j)))
    prev = pl.BlockSpec((halo, w), lambda i, j: (jnp.maximum(i * per_tile - 1, 0), col_block(j)))
    nxt = pl.BlockSpec((halo, w), lambda i, j: (jnp.minimum((i + 1) * per_tile, last), col_block(j)))
    return [main, prev, nxt]


def _shifted(x, prev_ref, next_ref, tiles_per_seq):
    tm = x.shape[0]
    i = pl.program_id(0)
    pos = i % tiles_per_seq
    halo = prev_ref.shape[0]
    prev_row = jnp.where(pos == 0, 0.0, prev_ref[halo - 1:halo, :].astype(F32))
    next_row = jnp.where(pos == tiles_per_seq - 1, 0.0, next_ref[0:1, :].astype(F32))
    row = lax.broadcasted_iota(jnp.int32, x.shape, 0)
    xp = jnp.where(row == 0, prev_row, pltpu.roll(x, 1, axis=0))
    xn = jnp.where(row == tm - 1, next_row, pltpu.roll(x, tm - 1, axis=0))
    return xp, xn


def _head_sum64(x):
    w = x.shape[1]
    r = lax.broadcasted_iota(jnp.int32, (w, w), 0) >> 6
    c = lax.broadcasted_iota(jnp.int32, (w, w), 1) >> 6
    ones = jnp.where(r == c, 1.0, 0.0).astype(BF16)
    return _dot(_bf(x), ones)


def _rwkv_pre_body(r_ref, rp_ref, rn_ref, k_ref, kp_ref, kn_ref, v_ref, vp_ref, vn_ref, l_ref, lp_ref, ln_ref,
                   mur_ref, muk_ref, muv_ref, mul_ref, w0_ref, a0_ref, wup_ref, aup_ref, gup_ref,
                   kk_w_ref, ka_ref, rk_ref,
                   r_out, v_out, kk_out, g_out, bonus_out, lw0_out, lw1_out, kz0_out, kz1_out, b0_out, b1_out,
                   *, tiles_per_seq):
    def mix(x_ref, p_ref, n_ref, mu_ref):
        x = x_ref[...].astype(F32)
        xp, xn = _shifted(x, p_ref, n_ref, tiles_per_seq)
        return x + mu_ref[0:1, :] * (xp - x) + mu_ref[1:2, :] * (xn - x)

    r = mix(r_ref, rp_ref, rn_ref, mur_ref)
    k = mix(k_ref, kp_ref, kn_ref, muk_ref)
    v = mix(v_ref, vp_ref, vn_ref, muv_ref)
    lo = mix(l_ref, lp_ref, ln_ref, mul_ref)
    wd = jnp.tanh(lo[:, 0:LANES])
    ad = lo[:, LANES:2 * LANES]
    gd = _sigmoid(lo[:, 2 * LANES:3 * LANES])
    lane = lax.broadcasted_iota(jnp.int32, (1, LANES), 1)

    g_out[...] = _dot(_bf(gd), gup_ref[...]).astype(g_out.dtype)
    kk = k * kk_w_ref[...]
    kk = kk * lax.rsqrt(jnp.maximum(_head_sum64(kk * kk), 1e-24))
    r_out[...] = r.astype(r_out.dtype)
    v_out[...] = v.astype(v_out.dtype)
    kk_out[...] = kk.astype(kk_out.dtype)
    bonus_out[...] = (_head_sum64(r * k * rk_ref[...]) * v).astype(bonus_out.dtype)
    for z, (lw_out, kz_out, b_out) in enumerate(((lw0_out, kz0_out, b0_out), (lw1_out, kz1_out, b1_out))):
        sel = (lane >> 6) == z
        w_pre = w0_ref[z:z + 1, :] + _dot(_bf(jnp.where(sel, wd, 0.0)), wup_ref[...])
        lw_out[...] = -math.exp(-0.5) * _sigmoid(w_pre)
        a = _sigmoid(a0_ref[z:z + 1, :] + _dot(_bf(jnp.where(sel, ad, 0.0)), aup_ref[...]))
        kz_out[...] = (k * (1.0 + (a - 1.0) * ka_ref[...])).astype(kz_out.dtype)
        b_out[...] = (kk * a).astype(b_out.dtype)


def _rwkv_prologue(ha, a_mu, a_w0, a_w_up, a_a0, a_a_up, a_g_up, a_k_k, a_k_a, a_r_k, seq_len, tm=256, tw=512):
    m = ha.shape[0]
    ncb = A_WIDTH // tw
    lora_block = 3 * A_WIDTH // A_LORA_IN
    halo = 2 * SUBLANES
    in_specs = []
    for s in range(3):
        in_specs += _halo_specs(tm, tw, lambda j, s=s: s * ncb + j, m, halo=halo)
    in_specs += _halo_specs(tm, A_LORA_IN, lambda j: lora_block, m, halo=halo)
    for s in range(3):
        in_specs.append(pl.BlockSpec((2, tw), lambda i, j, s=s: (0, s * ncb + j)))
    in_specs.append(pl.BlockSpec((2, A_LORA_IN), lambda i, j: (0, lora_block)))
    vec2 = pl.BlockSpec((2, tw), lambda i, j: (0, j))
    up = pl.BlockSpec((LANES, tw), lambda i, j: (0, j))
    vec1 = pl.BlockSpec((1, tw), lambda i, j: (0, j))
    in_specs += [vec2, vec2, up, up, up, vec1, vec1, vec1]
    out_spec = pl.BlockSpec((tm, tw), lambda i, j: (i, j))
    out_dtypes = [BF16, BF16, BF16, BF16, BF16, F32, F32, BF16, BF16, BF16, BF16]
    return pl.pallas_call(
        functools.partial(_rwkv_pre_body, tiles_per_seq=seq_len // tm),
        grid=(m // tm, ncb),
        in_specs=in_specs,
        out_specs=[out_spec] * 11,
        out_shape=[jax.ShapeDtypeStruct((m, A_WIDTH), dt) for dt in out_dtypes],
        compiler_params=_params(("parallel", "arbitrary")),
        name="rwkv_prologue",
    )(ha, ha, ha, ha, ha, ha, ha, ha, ha, ha, ha, ha, a_mu, a_mu, a_mu, a_mu, a_w0, a_a0,
      a_w_up.reshape(LANES, A_WIDTH).astype(BF16), a_a_up.reshape(LANES, A_WIDTH).astype(BF16), a_g_up.astype(BF16),
      a_k_k.reshape(1, A_WIDTH), a_k_a.reshape(1, A_WIDTH), a_r_k.reshape(1, A_WIDTH))


def _rwkv_chunk_terms(insts, *, hp):
    c, w = insts[0][1].shape
    rows = c * hp
    n = len(insts)
    revs = [inst[0] for inst in insts]
    ti = lax.broadcasted_iota(jnp.int32, (c, c), 0)
    si = lax.broadcasted_iota(jnp.int32, (c, c), 1)
    tri = {False: jnp.where(si <= ti, 1.0, 0.0).astype(BF16), True: jnp.where(si >= ti, 1.0, 0.0).astype(BF16)}
    rt = lax.broadcasted_iota(jnp.int32, (rows, rows), 0)
    cs = lax.broadcasted_iota(jnp.int32, (rows, rows), 1)
    eye = jnp.where(rt == cs, 1.0, 0.0)
    rt = rt & (c - 1)
    cs = cs & (c - 1)
    strict = {False: cs < rt, True: cs > rt}
    incl = {False: cs <= rt, True: cs >= rt}
    lane_head = lax.broadcasted_iota(jnp.int32, (c, w), 1) >> 6
    ew = lax.broadcasted_iota(jnp.int32, (w, w), 0) == lax.broadcasted_iota(jnp.int32, (w, w), 1)

    def stack_f32(x):
        return jnp.concatenate([jnp.where(lane_head == h, x, 0.0) for h in range(hp)], axis=0)

    def stack(x):
        return _bf(stack_f32(x))

    cums = []
    for rev, _, _, _, lw, _, _ in insts:
        hi = lw.astype(BF16)
        mid = (lw - hi.astype(F32)).astype(BF16)
        lo = (lw - hi.astype(F32) - mid.astype(F32)).astype(BF16)
        parts = _dot(tri[rev], jnp.concatenate([hi, mid, lo], axis=1))
        cums.append(parts[:, :w] + parts[:, w:2 * w] + parts[:, 2 * w:])
    tots = [cum[0:1, :] if rev else cum[c - 1:c, :] for rev, cum in zip(revs, cums)]

    xs, rs_f, bs, ks, bh, kh, vs = [], [], [], [], [], [], []
    for (rev, r, v, kk, lw, kz, b), cum, tot in zip(insts, cums, tots):
        r, kk, kz, b = (x.astype(F32) for x in (r, kk, kz, b))
        e_neg = jnp.exp(-cum)
        e_rest = jnp.exp(tot - cum)
        xs.append(stack(kk * jnp.exp(cum - lw)))
        rs_f.append(stack_f32(r * jnp.exp(cum)))
        bs.append(stack(b * e_neg))
        ks.append(stack(kz * e_neg))
        bh.append(stack(b * e_rest))
        kh.append(stack(kz * e_rest))
        vs.append(stack(v))

    grams = [_dot_nt(jnp.concatenate([xs[i], _bf(rs_f[i])], axis=0), jnp.concatenate([bs[i], ks[i]], axis=0))
             for i in range(n)]
    l_b = [jnp.where(strict[revs[i]], grams[i][:rows, :rows], 0.0) for i in range(n)]
    l_k = [_bf(jnp.where(strict[revs[i]], grams[i][:rows, rows:], 0.0)) for i in range(n)]
    incl2 = {rev: jnp.concatenate([m, m], axis=1) for rev, m in incl.items()}
    m_kb = [_bf(jnp.where(incl2[revs[i]], grams[i][rows:, :], 0.0)) for i in range(n)]
    lk_v = [_dot(l_k[i], vs[i]) for i in range(n)]

    t_inv = [eye - l_b[i] for i in range(n)]
    lb = [_bf(x) for x in l_b]
    l_pow = [_dot(x, x) for x in lb]
    for _ in range(int(math.log2(c)) - 2):
        lp = [_bf(x) for x in l_pow]
        both = [_dot(lp[i], jnp.concatenate([_bf(t_inv[i]), lp[i]], axis=1)) for i in range(n)]
        t_inv = [t_inv[i] + both[i][:, :rows] for i in range(n)]
        l_pow = [both[i][:, rows:] for i in range(n)]
    t_inv = [t_inv[i] + _dot(_bf(l_pow[i]), _bf(t_inv[i])) for i in range(n)]

    gh = [_bf(_dot(_bf(t_inv[i]), jnp.concatenate([xs[i], _bf(lk_v[i])], axis=1))) for i in range(n)]
    zero = jnp.zeros((rows, w), BF16)
    lhs = [jnp.concatenate([m_kb[i], jnp.concatenate([bh[i].T, kh[i].T], axis=1)], axis=0) for i in range(n)]
    rhs = [jnp.concatenate([-gh[i], jnp.concatenate([zero, vs[i]], axis=1)], axis=0) for i in range(n)]
    terms = [_dot(lhs[i], rhs[i]) for i in range(n)]
    out = []
    for i in range(n):
        q_p = rs_f[i] + terms[i][:rows, :w]
        y0 = terms[i][:rows, w:]
        phi = jnp.where(ew, jnp.exp(tots[i]), 0.0) + terms[i][rows:, :w]
        psi = terms[i][rows:, w:]
        out.append((_bf(q_p), y0, _bf(phi), psi))
    return out


def _rwkv_scan_body(rf_ref, vf_ref, kkf_ref, lwf_ref, kzf_ref, bf_ref, rb_ref, vb_ref, kkb_ref, lwb_ref, kzb_ref,
                    bb_ref, yf_ref, yb_ref, state_ref, *, hp, gp, cps, chunk):
    c = chunk
    w = A_HEAD * hp

    @pl.when(pl.program_id(2) == 0)
    def _():
        state_ref[...] = jnp.zeros_like(state_ref)

    dirs = ((False, (rf_ref, vf_ref, kkf_ref, lwf_ref, kzf_ref, bf_ref), yf_ref),
            (True, (rb_ref, vb_ref, kkb_ref, lwb_ref, kzb_ref, bb_ref), yb_ref))
    keys = [(d, gi, ci) for d in range(2) for gi in range(gp) for ci in range(cps)]
    insts = [(dirs[d][0], *[x[ci * c:(ci + 1) * c, gi * w:(gi + 1) * w] for x in dirs[d][1]]) for d, gi, ci in keys]
    terms = dict(zip(keys, _rwkv_chunk_terms(insts, hp=hp)))
    chains = [(d, gi) for d in range(2) for gi in range(gp)]
    states = {k: state_ref[k[0], k[1]] for k in chains}
    for step in range(cps):
        for d, gi in chains:
            ci = cps - 1 - step if dirs[d][0] else step
            q_p, y0, phi, psi = terms[(d, gi, ci)]
            a_b = _bf(states[(d, gi)])
            y_st = _dot(q_p, a_b) + y0
            states[(d, gi)] = _dot(phi, a_b) + psi
            y = y_st[0:c, :]
            for h in range(1, hp):
                y = y + y_st[h * c:(h + 1) * c, :]
            dirs[d][2][ci * c:(ci + 1) * c, gi * w:(gi + 1) * w] = y.astype(dirs[d][2].dtype)
    for d, gi in chains:
        state_ref[d, gi] = states[(d, gi)]


def _rwkv_scan(r, v, kk, lw0, kz0, b0, lw1, kz1, b1, batch, seq_len, hp=RWKV_HEADS_PER_GROUP,
               gp=RWKV_GROUPS_PER_STEP, cps=RWKV_CHUNKS_PER_STEP, chunk=RWKV_CHUNK):
    m = r.shape[0]
    w = A_HEAD * hp
    rows = chunk * cps
    ns = seq_len // rows
    fwd = pl.BlockSpec((rows, w * gp), lambda bi, g, n: (bi * ns + n, g))
    bwd = pl.BlockSpec((rows, w * gp), lambda bi, g, n: (bi * ns + ns - 1 - n, g))
    out = jax.ShapeDtypeStruct((m, A_WIDTH), BF16)
    return pl.pallas_call(
        functools.partial(_rwkv_scan_body, hp=hp, gp=gp, cps=cps, chunk=chunk),
        grid=(batch, A_WIDTH // (w * gp), ns),
        in_specs=[fwd] * 6 + [bwd] * 6,
        out_specs=[fwd, bwd],
        out_shape=[out, out],
        scratch_shapes=[pltpu.VMEM((2, gp, w, w), F32)],
        compiler_params=_params(("parallel", "parallel", "arbitrary")),
        name="rwkv_scan",
    )(r, v, kk, lw0, kz0, b0, r, v, kk, lw1, kz1, b1)


def _rwkv_post_body(yf_ref, yb_ref, bonus_ref, g_ref, lnw_ref, lnb_ref, o_ref):
    y = yf_ref[...].astype(F32) + yb_ref[...].astype(F32)
    inv_n = 1.0 / A_HEAD
    mu = _head_sum64(y) * inv_n
    d = y - mu
    var = _head_sum64(d * d) * inv_n
    yn = d * lax.rsqrt(var + A_GN_EPS) * lnw_ref[...] + lnb_ref[...]
    o_ref[...] = ((yn + bonus_ref[...].astype(F32)) * g_ref[...].astype(F32)).astype(o_ref.dtype)


def _rwkv_epilogue(yf, yb, bonus, g, ln_w, ln_b, tm=256, tw=512):
    m = yf.shape[0]
    spec = pl.BlockSpec((tm, tw), lambda i, j: (i, j))
    vec = pl.BlockSpec((1, tw), lambda i, j: (0, j))
    return pl.pallas_call(
        _rwkv_post_body,
        grid=(m // tm, A_WIDTH // tw),
        in_specs=[spec, spec, spec, spec, vec, vec],
        out_specs=spec,
        out_shape=jax.ShapeDtypeStruct((m, A_WIDTH), BF16),
        compiler_params=_params(("parallel", "parallel")),
        name="rwkv_epilogue",
    )(yf, yb, bonus, g, ln_w.reshape(1, A_WIDTH), ln_b.reshape(1, A_WIDTH))


ATTN_QUERY_ROWS = 1024
ATTN_SUBTILES = 4
ATTN_LOOKAHEAD = 2


def _attn_body(q_ref, k_ref, v_ref, o_ref):
    sub = q_ref.shape[0] // ATTN_SUBTILES
    k = k_ref[...]
    v = v_ref[...]

    def scores(i):
        return _dot_nt(q_ref[i * sub:(i + 1) * sub, :], k)

    s = {i: scores(i) for i in range(ATTN_LOOKAHEAD)}
    for i in range(ATTN_SUBTILES):
        si = s.pop(i)
        p = jnp.exp(_bf(si - jnp.max(si, axis=-1, keepdims=True)))
        if i + ATTN_LOOKAHEAD < ATTN_SUBTILES:
            s[i + ATTN_LOOKAHEAD] = scores(i + ATTN_LOOKAHEAD)
        o = _dot(p, v)
        o_ref[i * sub:(i + 1) * sub, :] = (o[:, :B_VDIM] / o[:, B_VDIM:]).astype(o_ref.dtype)


def _attention(q, k, v, batch, seq_len, tq):
    m = q.shape[1]
    nq = seq_len // tq
    return pl.pallas_call(
        _attn_body,
        grid=(B_HEADS, batch, nq),
        in_specs=[pl.BlockSpec((None, tq, B_QK_PAD), lambda h, bi, qi: (h, bi * nq + qi, 0)),
                  pl.BlockSpec((None, seq_len, B_QK_PAD), lambda h, bi, qi: (h, bi, 0)),
                  pl.BlockSpec((None, seq_len, 2 * B_VDIM), lambda h, bi, qi: (h, bi, 0))],
        out_specs=pl.BlockSpec((tq, B_VDIM), lambda h, bi, qi: (bi * nq + qi, h)),
        out_shape=jax.ShapeDtypeStruct((m, B_HEADS * B_VDIM), BF16),
        compiler_params=_params(("parallel", "parallel", "arbitrary")),
        name="mla_attention",
    )(q, k, v)


def _ret_body(lg_ref, q_ref, k_ref, v_ref, *rest, rev, final, chunk, hps):
    if final:
        other_ref, gate_ref, o_ref, state_ref = rest
    else:
        o_ref, state_ref = rest
    c = chunk
    heads = range(hps)
    cols = [slice(h * C_HEAD, (h + 1) * C_HEAD) for h in heads]

    @pl.when(pl.program_id(2) == 0)
    def _():
        state_ref[...] = jnp.zeros_like(state_ref)

    lg = [lg_ref[h, 0:1, 0:1] for h in heads]
    q = [q_ref[:, cols[h]] for h in heads]
    k = [k_ref[:, cols[h]] for h in heads]
    v = [v_ref[:, cols[h]] for h in heads]
    idx = lax.broadcasted_iota(jnp.int32, (c, 1), 0).astype(F32)
    if rev:
        q_pow, k_pow = c - idx, idx
    else:
        q_pow, k_pow = idx + 1.0, c - 1.0 - idx
    qd = [_bf(q[h].astype(F32) * jnp.exp(lg[h] * q_pow)) for h in heads]
    kd = [_bf(k[h].astype(F32) * jnp.exp(lg[h] * k_pow)) for h in heads]
    state = [state_ref[h] for h in heads]
    cross = [_dot(qd[h], _bf(state[h])) for h in heads]
    kv = [_dot_tn(kd[h], v[h]) for h in heads]
    for h in heads:
        state_ref[h] = state[h] * jnp.exp(lg[h] * float(c)) + kv[h]
    if final:
        ti = lax.broadcasted_iota(jnp.int32, (c, c), 0)
        si = lax.broadcasted_iota(jnp.int32, (c, c), 1)
        dist = jnp.abs(ti - si).astype(F32)
        s = [_dot_nt(q[h], k[h]) for h in heads]
        inner = [_dot(_bf(s[h] * jnp.exp(lg[h] * dist)), v[h]) for h in heads]
        for h in heads:
            o = inner[h] + cross[h] + other_ref[:, cols[h]]
            mu = jnp.mean(o, axis=-1, keepdims=True)
            d = o - mu
            var = jnp.mean(d * d, axis=-1, keepdims=True)
            o_ref[:, cols[h]] = (gate_ref[:, cols[h]].astype(F32) * (d * lax.rsqrt(var + C_GN_EPS))).astype(o_ref.dtype)
    else:
        for h in heads:
            o_ref[:, cols[h]] = cross[h]


def _retention_pass(log_gamma, qk, vg, other, batch, seq_len, rev, final, chunk=RET_CHUNK, hps=RET_HEADS_PER_STEP):
    m = qk.shape[0]
    nc = seq_len // chunk
    ng = C_HEADS // hps
    if rev:
        row = lambda bi, n: bi * nc + nc - 1 - n
    else:
        row = lambda bi, n: bi * nc + n
    blk = lambda off: pl.BlockSpec((chunk, C_HEAD * hps), lambda g, bi, n: (row(bi, n), g + off))
    in_specs = [pl.BlockSpec((hps, SUBLANES, LANES), lambda g, bi, n: (g, 0, 0)), blk(0), blk(ng), blk(0)]
    args = [log_gamma, qk, qk, vg]
    if final:
        in_specs += [blk(0), blk(ng)]
        args += [other, vg]
    return pl.pallas_call(
        functools.partial(_ret_body, rev=rev, final=final, chunk=chunk, hps=hps),
        grid=(ng, batch, nc),
        in_specs=in_specs,
        out_specs=blk(0),
        out_shape=jax.ShapeDtypeStruct((m, C_WIDTH), BF16 if final else F32),
        scratch_shapes=[pltpu.VMEM((hps, C_HEAD, C_HEAD), F32)],
        compiler_params=_params(("parallel", "parallel", "arbitrary")),
        name="retention_final" if final else "retention_cross",
    )(*args)


def _ffn_up_body(x_ref, xp_ref, xn_ref, wg_ref, wv_ref, cw_ref, cb_ref, o_ref, xext_ref, *, tiles_per_seq):
    tm = x_ref.shape[0]
    halo = xp_ref.shape[0]

    @pl.when(pl.program_id(1) == 0)
    def _():
        pos = pl.program_id(0) % tiles_per_seq
        xext_ref[0:halo, :] = jnp.where(pos == 0, jnp.zeros_like(xp_ref[...]), xp_ref[...])
        xext_ref[halo:halo + tm, :] = x_ref[...]
        xext_ref[halo + tm:, :] = jnp.where(pos == tiles_per_seq - 1, jnp.zeros_like(xn_ref[...]), xn_ref[...])

    g_ext = _dot(xext_ref[...], wg_ref[...])
    val = _dot(xext_ref[halo:halo + tm, :], wv_ref[...])
    rows = tm + 2 * halo
    g = g_ext[halo:halo + tm]
    gp = pltpu.roll(g_ext, 1, axis=0)[halo:halo + tm]
    gn = pltpu.roll(g_ext, rows - 1, axis=0)[halo:halo + tm]
    conv = cw_ref[0:1, :] * gp + cw_ref[1:2, :] * g + cw_ref[2:3, :] * gn + cb_ref[...]
    o_ref[...] = (_silu(conv) * val).astype(o_ref.dtype)


def _ffn_up_act(xn, w_up, conv_w, conv_b, seq_len, tm=MM_ROWS, tn=256):
    m, k = xn.shape
    nj = FFN_HIDDEN // tn
    halo = 2 * SUBLANES
    return pl.pallas_call(
        functools.partial(_ffn_up_body, tiles_per_seq=seq_len // tm),
        grid=(m // tm, nj),
        in_specs=[*_halo_specs(tm, k, lambda j: 0, m, halo=halo),
                  pl.BlockSpec((k, tn), lambda i, j: (0, j)),
                  pl.BlockSpec((k, tn), lambda i, j: (0, nj + j)),
                  pl.BlockSpec((3, tn), lambda i, j: (0, j)),
                  pl.BlockSpec((1, tn), lambda i, j: (0, j))],
        out_specs=pl.BlockSpec((tm, tn), lambda i, j: (i, j)),
        out_shape=jax.ShapeDtypeStruct((m, FFN_HIDDEN), BF16),
        scratch_shapes=[pltpu.VMEM((tm + 2 * halo, k), BF16)],
        compiler_params=_params(("parallel", "arbitrary")),
        name="ffn_up_conv_gate",
    )(xn, xn, xn, w_up, w_up, conv_w, conv_b.reshape(1, FFN_HIDDEN))


def _rope_tables(seq_len, dim):
    inv = ROPE_BASE ** (-jnp.arange(0, dim, 2, dtype=F32) / dim)
    ang = jnp.arange(seq_len, dtype=F32)[:, None] * inv[None, :]
    return jnp.cos(ang), jnp.sin(ang)


def _swap_halves_cols(w):
    h = w.shape[-1] // 2
    return jnp.concatenate([w[..., h:], w[..., :h]], axis=-1)


def _prep_layer(w_in, b_q_up, b_kv_up, w_branch, w_out, ffn_up, ffn_down):
    o_b = A_IN
    o_kpe = o_b + B_Q_LORA + B_KV_LORA
    o_c = o_kpe + B_ROPE
    o_g = o_c + 4 * C_WIDTH
    w_kpe = w_in[:, o_kpe:o_c]
    qh = b_q_up.reshape(B_Q_LORA, B_HEADS, B_NOPE + B_ROPE)
    q_pe = qh[..., B_NOPE:]
    w_q = jnp.concatenate([qh[..., :B_NOPE], q_pe, _swap_halves_cols(q_pe)], axis=-1)
    return dict(
        w_a=jnp.pad(w_in[:, :A_IN], ((0, 0), (0, LANES))).astype(BF16),
        w_b=w_in[:, o_b:o_kpe].astype(BF16),
        w_kpe=jnp.concatenate([w_kpe, _swap_halves_cols(w_kpe)], axis=-1).astype(BF16),
        w_cqk=w_in[:, o_c:o_c + 2 * C_WIDTH].astype(BF16),
        w_cvg=w_in[:, o_c + 2 * C_WIDTH:o_g].astype(BF16),
        w_g=w_in[:, o_g:].astype(BF16),
        w_q=jnp.transpose(w_q, (1, 0, 2)).astype(BF16),
        w_kv=jnp.transpose(b_kv_up.reshape(B_KV_LORA, B_HEADS, B_NOPE + B_VDIM), (1, 0, 2)).astype(BF16),
        w_br=w_branch.reshape(3, A_WIDTH, D_MODEL).astype(BF16),
        w_out=w_out.astype(BF16),
        ffn_up=ffn_up.astype(BF16),
        ffn_down=ffn_down.astype(BF16),
    )


def _layer(x, batch, seq_len, tabs, pw, attn_norm, a_mu, a_w0, a_w_up, a_a0, a_a_up, a_g_up, a_k_k, a_k_a, a_r_k,
           a_ln_w, a_ln_b, b_q_norm, b_kv_norm, ffn_norm, ffn_conv, ffn_conv_b):
    m = x.shape[0]
    tps = lambda tm: seq_len // tm
    xn = _rmsnorm(x, attn_norm, BF16)

    (ha,) = _matmul("in_proj_a", xn, pw["w_a"], tm=MM_ROWS, tn=512, epilogue=_ep_plain, out_dtypes=[BF16])
    r, v, kk, g, bonus, lw0, lw1, kz0, kz1, b0, b1 = _rwkv_prologue(
        ha, a_mu, a_w0, a_w_up, a_a0, a_a_up, a_g_up, a_k_k, a_k_a, a_r_k, seq_len)
    yf, yb = _rwkv_scan(r, v, kk, lw0, kz0, b0, lw1, kz1, b1, batch, seq_len)
    oa = _rwkv_epilogue(yf, yb, bonus, g, a_ln_w, a_ln_b)

    tm_b = 512
    vec = lambda n: pl.BlockSpec((1, n), lambda i, j: (0, 0))
    qn, kvn = _matmul("in_proj_b", xn, pw["w_b"], tm=tm_b, tn=B_Q_LORA + B_KV_LORA, epilogue=_ep_qkv_norms,
                      out_dtypes=[BF16, BF16], out_widths=[B_Q_LORA, B_KV_LORA],
                      extras=(b_q_norm.reshape(1, -1), b_kv_norm.reshape(1, -1)),
                      extra_specs=(vec(B_Q_LORA), vec(B_KV_LORA)))
    tab64 = lambda tm: pl.BlockSpec((tm, LANES), lambda i, j: (i % tps(tm), 0))
    (kpe,) = _matmul("in_proj_kpe", xn, pw["w_kpe"], tm=MM_ROWS, tn=LANES, epilogue=_ep_kpe, out_dtypes=[BF16],
                     extras=(tabs["cos64"], tabs["sin64"]), extra_specs=(tab64(MM_ROWS), tab64(MM_ROWS)))
    (q,) = _matmul("q_up", qn, pw["w_q"], tm=MM_ROWS, tn=B_QK_PAD, epilogue=_ep_q_up, out_dtypes=[BF16],
                   extras=(tabs["cos64"], tabs["sin64"]), extra_specs=(tab64(MM_ROWS), tab64(MM_ROWS)),
                   b_head_major=True)
    k, vv = _matmul("kv_up", kvn, pw["w_kv"], tm=MM_ROWS, tn=B_NOPE + B_VDIM, epilogue=_ep_kv_up,
                    out_dtypes=[BF16, BF16], out_widths=[B_QK_PAD, 2 * B_VDIM],
                    extras=(kpe,), extra_specs=(pl.BlockSpec((MM_ROWS, LANES), lambda i, j: (i, 0)),),
                    b_head_major=True)
    ob = _attention(q, k, vv, batch, seq_len, tq=ATTN_QUERY_ROWS)

    tab128 = pl.BlockSpec((MM_ROWS, LANES), lambda i, j: (i % tps(MM_ROWS), 0))
    q_tiles = C_WIDTH // 512
    (qk,) = _matmul("in_proj_c_qk", xn, pw["w_cqk"], tm=MM_ROWS, tn=512,
                    epilogue=functools.partial(_ep_ret_qk, q_tiles=q_tiles), out_dtypes=[BF16],
                    extras=(tabs["cos256"], tabs["sin256"]), extra_specs=(tab128, tab128))
    (vg,) = _matmul("in_proj_c_vg", xn, pw["w_cvg"], tm=MM_ROWS, tn=512,
                    epilogue=functools.partial(_ep_ret_vg, v_tiles=q_tiles), out_dtypes=[BF16])
    cross_b = _retention_pass(tabs["log_gamma"], qk, vg, None, batch, seq_len, rev=True, final=False)
    oc = _retention_pass(tabs["log_gamma"], qk, vg, cross_b, batch, seq_len, rev=False, final=True)

    (gates,) = _matmul("in_proj_gates", xn, pw["w_g"], tm=MM_ROWS, tn=1024, epilogue=_ep_sigmoid, out_dtypes=[BF16])
    merged = _merge(oa, ob, oc, pw["w_br"], gates)
    res_spec = lambda tm, tn: pl.BlockSpec((tm, tn), lambda i, j: (i, j))
    (x,) = _matmul("out_proj", merged, pw["w_out"], tm=MM_ROWS, tn=512, epilogue=_ep_residual, out_dtypes=[F32],
                   extras=(x,), extra_specs=(res_spec(MM_ROWS, 512),))

    xn2 = _rmsnorm(x, ffn_norm, BF16)
    act = _ffn_up_act(xn2, pw["ffn_up"], ffn_conv, ffn_conv_b, seq_len)
    (x,) = _matmul("ffn_down", act, pw["ffn_down"], tm=512, tn=512, epilogue=_ep_residual, out_dtypes=[F32],
                   extras=(x,), extra_specs=(res_spec(512, 512),))
    return x


def _tables(seq_len):
    c64, s64 = _rope_tables(seq_len, B_ROPE)
    z = jnp.zeros((seq_len, B_ROPE), F32)
    c256, s256 = _rope_tables(seq_len, C_HEAD)
    log_gamma = jnp.log1p(-(2.0 ** (-5.0 - jnp.arange(C_HEADS, dtype=F32))))
    return dict(
        cos64=jnp.concatenate([c64, c64, z], axis=1),
        sin64=jnp.concatenate([-s64, s64, z], axis=1),
        cos256=c256, sin256=s256,
        log_gamma=jnp.broadcast_to(log_gamma[:, None, None], (C_HEADS, SUBLANES, LANES)),
    )


def _encoder(x3, prepped, per_layer, final_norm):
    batch, seq_len, d = x3.shape
    x = x3.reshape(batch * seq_len, d)
    tabs = _tables(seq_len)
    for pw, lw in zip(prepped, per_layer):
        x = _layer(x, batch, seq_len, tabs, pw, *lw)
    return _rmsnorm(x, final_norm, F32).reshape(batch, seq_len, d)


def kernel(x_prompt, x_sample, attn_norm, w_in, a_mu, a_w0, a_w_up, a_a0, a_a_up, a_g_up, a_k_k, a_k_a, a_r_k, a_ln_w, a_ln_b, b_q_norm, b_q_up, b_kv_norm, b_kv_up, w_branch, w_out, ffn_norm, ffn_up, ffn_conv, ffn_conv_b, ffn_down, final_norm):
    depth = w_in.shape[0]
    prepped = [_prep_layer(w_in[l], b_q_up[l], b_kv_up[l], w_branch[l], w_out[l], ffn_up[l], ffn_down[l])
               for l in range(depth)]
    per_layer = [(attn_norm[l], a_mu[l], a_w0[l], a_w_up[l], a_a0[l], a_a_up[l], a_g_up[l], a_k_k[l], a_k_a[l],
                  a_r_k[l], a_ln_w[l], a_ln_b[l], b_q_norm[l], b_kv_norm[l], ffn_norm[l], ffn_conv[l], ffn_conv_b[l])
                 for l in range(depth)]
    return (_encoder(x_prompt, prepped, per_layer, final_norm), _encoder(x_sample, prepped, per_layer, final_norm))
```

```python
import functools
import math

import jax
import jax.numpy as jnp
from jax import lax
from jax.experimental import pallas as pl
from jax.experimental.pallas import tpu as pltpu

F32 = jnp.float32
BF16 = jnp.bfloat16

D_MODEL = 4096
DEPTH = 2
A_HEAD = 64
A_WIDTH = 2048
A_LORA_IN = 384
A_IN = 3 * A_WIDTH + A_LORA_IN
A_GN_EPS = 64e-5
B_NOPE = 128
B_ROPE = 64
B_VDIM = 128
B_HEADS = 16
B_Q_LORA = 1536
B_KV_LORA = 512
B_QK_PAD = 256
C_HEAD = 256
C_HEADS = 8
C_WIDTH = 2048
C_GN_EPS = 1e-5
FFN_HIDDEN = 11008
ROPE_BASE = 10000.0
NORM_EPS = 1e-6

LANES = 128
SUBLANES = 8
VMEM_LIMIT_BYTES = 56 * 2**20
MM_ROWS = 1024

RWKV_CHUNK = 64
RWKV_HEADS_PER_GROUP = 2
RWKV_GROUPS_PER_STEP = 2
RWKV_CHUNKS_PER_STEP = 4
RET_CHUNK = 256
RET_HEADS_PER_STEP = 8


def _params(sem):
    return pltpu.CompilerParams(dimension_semantics=sem, vmem_limit_bytes=VMEM_LIMIT_BYTES)


def _dot(a, b):
    return jnp.dot(a, b, preferred_element_type=F32)


def _dot_nt(a, b):
    return lax.dot_general(a, b, (((1,), (1,)), ((), ())), preferred_element_type=F32)


def _dot_tn(a, b):
    return lax.dot_general(a, b, (((0,), (0,)), ((), ())), preferred_element_type=F32)


def _bf(x):
    return x.astype(BF16)


def _sigmoid(x):
    return 1.0 / (1.0 + jnp.exp(-x))


def _silu(x):
    return x * _sigmoid(x)


def _rmsnorm_body(x_ref, g_ref, o_ref):
    x = x_ref[...]
    y = x * lax.rsqrt(jnp.mean(x * x, axis=-1, keepdims=True) + NORM_EPS)
    o_ref[...] = (y * g_ref[...]).astype(o_ref.dtype)


def _rmsnorm(x, g, out_dtype, tm=256):
    m, d = x.shape
    return pl.pallas_call(
        _rmsnorm_body,
        grid=(m // tm,),
        in_specs=[pl.BlockSpec((tm, d), lambda i: (i, 0)), pl.BlockSpec((1, d), lambda i: (0, 0))],
        out_specs=pl.BlockSpec((tm, d), lambda i: (i, 0)),
        out_shape=jax.ShapeDtypeStruct((m, d), out_dtype),
        compiler_params=_params(("parallel",)),
        name="rmsnorm",
    )(x, g.reshape(1, d))


def _mm_body(a_ref, b_ref, *rest, n_extra, epilogue):
    extras = rest[:n_extra]
    outs = rest[n_extra:]
    acc = _dot(a_ref[...], b_ref[...])
    res = epilogue(acc, *extras)
    for o_ref, r in zip(outs, res):
        o_ref[...] = r.astype(o_ref.dtype)


def _matmul(name, a, b, *, tm, tn, epilogue, out_dtypes, out_widths=None, extras=(), extra_specs=(),
            b_head_major=False):
    m, k = a.shape
    if b_head_major:
        nj = b.shape[0]
        b_spec = pl.BlockSpec((None, k, tn), lambda i, j: (j, 0, 0))
    else:
        nj = b.shape[1] // tn
        b_spec = pl.BlockSpec((k, tn), lambda i, j: (0, j))
    out_widths = out_widths or [tn] * len(out_dtypes)
    if b_head_major:
        out_specs = [pl.BlockSpec((None, tm, w), lambda i, j: (j, i, 0)) for w in out_widths]
        out_shape = [jax.ShapeDtypeStruct((nj, m, w), dt) for w, dt in zip(out_widths, out_dtypes)]
    else:
        out_specs = [pl.BlockSpec((tm, w), lambda i, j: (i, j)) for w in out_widths]
        out_shape = [jax.ShapeDtypeStruct((m, nj * w), dt) for w, dt in zip(out_widths, out_dtypes)]
    return pl.pallas_call(
        functools.partial(_mm_body, n_extra=len(extras), epilogue=epilogue),
        grid=(m // tm, nj),
        in_specs=[pl.BlockSpec((tm, k), lambda i, j: (i, 0)), b_spec, *extra_specs],
        out_specs=out_specs,
        out_shape=out_shape,
        compiler_params=_params(("parallel", "arbitrary")),
        name=name,
    )(a, b, *extras)


def _ep_plain(acc):
    return (acc,)


def _ep_sigmoid(acc):
    return (_sigmoid(acc),)


def _ep_residual(acc, res_ref):
    return (acc + res_ref[...],)


def _ep_qkv_norms(acc, gq_ref, gkv_ref):
    q = acc[:, :B_Q_LORA]
    kv = acc[:, B_Q_LORA:]
    qn = q * lax.rsqrt(jnp.mean(q * q, axis=-1, keepdims=True) + NORM_EPS) * gq_ref[...]
    kvn = kv * lax.rsqrt(jnp.mean(kv * kv, axis=-1, keepdims=True) + NORM_EPS) * gkv_ref[...]
    return qn, kvn


def _rope64_paired(x, cos_ref, sin_ref):
    return x * cos_ref[...] + pltpu.roll(x, 64, axis=1) * sin_ref[...]


def _ep_kpe(acc, cos_ref, sin_ref):
    return (_rope64_paired(acc, cos_ref, sin_ref),)


def _ep_q_up(acc, cos_ref, sin_ref):
    scale = (B_NOPE + B_ROPE) ** -0.5
    return (jnp.concatenate([acc[:, :B_NOPE], _rope64_paired(acc[:, B_NOPE:], cos_ref, sin_ref)], axis=1) * scale,)


def _ep_kv_up(acc, kpe_ref):
    k = jnp.concatenate([acc[:, :B_NOPE].astype(BF16), kpe_ref[...]], axis=1)
    v = acc[:, B_NOPE:]
    return k, jnp.concatenate([v, jnp.ones_like(v)], axis=1)


def _ep_ret_qk(acc, cos_ref, sin_ref, *, q_tiles):
    cos = cos_ref[...]
    sin = sin_ref[...]
    half = C_HEAD // 2
    pieces = []
    for h in range(acc.shape[1] // C_HEAD):
        x1 = acc[:, h * C_HEAD:h * C_HEAD + half]
        x2 = acc[:, h * C_HEAD + half:(h + 1) * C_HEAD]
        pieces += [x1 * cos - x2 * sin, x1 * sin + x2 * cos]
    out = jnp.concatenate(pieces, axis=1)
    scale = jnp.where(pl.program_id(1) < q_tiles, C_HEAD ** -0.5, 1.0)
    return (out * scale,)


def _ep_ret_vg(acc, *, v_tiles):
    return (jnp.where(pl.program_id(1) < v_tiles, acc, _silu(acc)),)


def _merge_body(oa_ref, ob_ref, oc_ref, w_ref, g_ref, out_ref, acc_ref):
    br = pl.program_id(2)
    g = g_ref[...].astype(F32)

    @pl.when(br == 0)
    def _():
        acc_ref[...] = g * _dot(oa_ref[...], w_ref[...])

    @pl.when(br == 1)
    def _():
        acc_ref[...] += g * _dot(ob_ref[...], w_ref[...])

    @pl.when(br == 2)
    def _():
        out_ref[...] = (acc_ref[...] + g * _dot(oc_ref[...], w_ref[...])).astype(out_ref.dtype)


def _merge(oa, ob, oc, w3, gates, tm=MM_ROWS, tn=512):
    m, kb = oa.shape
    n = w3.shape[2]
    nj = n // tn
    o_spec = pl.BlockSpec((tm, kb), lambda i, j, br: (i, 0))
    return pl.pallas_call(
        _merge_body,
        grid=(m // tm, nj, 3),
        in_specs=[o_spec, o_spec, o_spec,
                  pl.BlockSpec((None, kb, tn), lambda i, j, br: (br, 0, j)),
                  pl.BlockSpec((tm, tn), lambda i, j, br: (i, br * nj + j))],
        out_specs=pl.BlockSpec((tm, tn), lambda i, j, br: (i, j)),
        out_shape=jax.ShapeDtypeStruct((m, n), BF16),
        scratch_shapes=[pltpu.VMEM((tm, tn), F32)],
        compiler_params=_params(("parallel", "arbitrary", "arbitrary")),
        name="branch_merge",
    )(oa, ob, oc, w3, gates)


def _halo_specs(tm, w, col_block, n_rows, halo=SUBLANES):
    per_tile = tm // halo
    last = n_rows // halo - 1
    main = pl.BlockSpec((tm, w), lambda i, j: (i, col_block(j)))
    prev = pl.BlockSpec((halo, w), lambda i, j: (jnp.maximum(i * per_tile - 1, 0), col_block(j)))
    nxt = pl.BlockSpec((halo, w), lambda i, j: (jnp.minimum((i + 1) * per_tile, last), col_block(j)))
    return [main, prev, nxt]


def _shifted(x, prev_ref, next_ref, tiles_per_seq):
    tm = x.shape[0]
    i = pl.program_id(0)
    pos = i % tiles_per_seq
    halo = prev_ref.shape[0]
    prev_row = jnp.where(pos == 0, 0.0, prev_ref[halo - 1:halo, :].astype(F32))
    next_row = jnp.where(pos == tiles_per_seq - 1, 0.0, next_ref[0:1, :].astype(F32))
    row = lax.broadcasted_iota(jnp.int32, x.shape, 0)
    xp = jnp.where(row == 0, prev_row, pltpu.roll(x, 1, axis=0))
    xn = jnp.where(row == tm - 1, next_row, pltpu.roll(x, tm - 1, axis=0))
    return xp, xn


def _head_sum64(x):
    w = x.shape[1]
    r = lax.broadcasted_iota(jnp.int32, (w, w), 0) >> 6
    c = lax.broadcasted_iota(jnp.int32, (w, w), 1) >> 6
    ones = jnp.where(r == c, 1.0, 0.0).astype(BF16)
    return _dot(_bf(x), ones)


def _rwkv_pre_body(r_ref, rp_ref, rn_ref, k_ref, kp_ref, kn_ref, v_ref, vp_ref, vn_ref, l_ref, lp_ref, ln_ref,
                   mur_ref, muk_ref, muv_ref, mul_ref, w0_ref, a0_ref, wup_ref, aup_ref, gup_ref,
                   kk_w_ref, ka_ref, rk_ref,
                   r_out, v_out, kk_out, g_out, bonus_out, lw0_out, lw1_out, kz0_out, kz1_out, b0_out, b1_out,
                   *, tiles_per_seq):
    def mix(x_ref, p_ref, n_ref, mu_ref):
        x = x_ref[...].astype(F32)
        xp, xn = _shifted(x, p_ref, n_ref, tiles_per_seq)
        return x + mu_ref[0:1, :] * (xp - x) + mu_ref[1:2, :] * (xn - x)

    r = mix(r_ref, rp_ref, rn_ref, mur_ref)
    k = mix(k_ref, kp_ref, kn_ref, muk_ref)
    v = mix(v_ref, vp_ref, vn_ref, muv_ref)
    lo = mix(l_ref, lp_ref, ln_ref, mul_ref)
    wd = jnp.tanh(lo[:, 0:LANES])
    ad = lo[:, LANES:2 * LANES]
    gd = _sigmoid(lo[:, 2 * LANES:3 * LANES])
    lane = lax.broadcasted_iota(jnp.int32, (1, LANES), 1)

    g_out[...] = _dot(_bf(gd), gup_ref[...]).astype(g_out.dtype)
    kk = k * kk_w_ref[...]
    kk = kk * lax.rsqrt(jnp.maximum(_head_sum64(kk * kk), 1e-24))
    r_out[...] = r.astype(r_out.dtype)
    v_out[...] = v.astype(v_out.dtype)
    kk_out[...] = kk.astype(kk_out.dtype)
    bonus_out[...] = (_head_sum64(r * k * rk_ref[...]) * v).astype(bonus_out.dtype)
    for z, (lw_out, kz_out, b_out) in enumerate(((lw0_out, kz0_out, b0_out), (lw1_out, kz1_out, b1_out))):
        sel = (lane >> 6) == z
        w_pre = w0_ref[z:z + 1, :] + _dot(_bf(jnp.where(sel, wd, 0.0)), wup_ref[...])
        lw_out[...] = -math.exp(-0.5) * _sigmoid(w_pre)
        a = _sigmoid(a0_ref[z:z + 1, :] + _dot(_bf(jnp.where(sel, ad, 0.0)), aup_ref[...]))
        kz_out[...] = (k * (1.0 + (a - 1.0) * ka_ref[...])).astype(kz_out.dtype)
        b_out[...] = (kk * a).astype(b_out.dtype)


def _rwkv_prologue(ha, a_mu, a_w0, a_w_up, a_a0, a_a_up, a_g_up, a_k_k, a_k_a, a_r_k, seq_len, tm=256, tw=512):
    m = ha.shape[0]
    ncb = A_WIDTH // tw
    lora_block = 3 * A_WIDTH // A_LORA_IN
    halo = 2 * SUBLANES
    in_specs = []
    for s in range(3):
        in_specs += _halo_specs(tm, tw, lambda j, s=s: s * ncb + j, m, halo=halo)
    in_specs += _halo_specs(tm, A_LORA_IN, lambda j: lora_block, m, halo=halo)
    for s in range(3):
        in_specs.append(pl.BlockSpec((2, tw), lambda i, j, s=s: (0, s * ncb + j)))
    in_specs.append(pl.BlockSpec((2, A_LORA_IN), lambda i, j: (0, lora_block)))
    vec2 = pl.BlockSpec((2, tw), lambda i, j: (0, j))
    up = pl.BlockSpec((LANES, tw), lambda i, j: (0, j))
    vec1 = pl.BlockSpec((1, tw), lambda i, j: (0, j))
    in_specs += [vec2, vec2, up, up, up, vec1, vec1, vec1]
    out_spec = pl.BlockSpec((tm, tw), lambda i, j: (i, j))
    out_dtypes = [BF16, BF16, BF16, BF16, BF16, F32, F32, BF16, BF16, BF16, BF16]
    return pl.pallas_call(
        functools.partial(_rwkv_pre_body, tiles_per_seq=seq_len // tm),
        grid=(m // tm, ncb),
        in_specs=in_specs,
        out_specs=[out_spec] * 11,
        out_shape=[jax.ShapeDtypeStruct((m, A_WIDTH), dt) for dt in out_dtypes],
        compiler_params=_params(("parallel", "arbitrary")),
        name="rwkv_prologue",
    )(ha, ha, ha, ha, ha, ha, ha, ha, ha, ha, ha, ha, a_mu, a_mu, a_mu, a_mu, a_w0, a_a0,
      a_w_up.reshape(LANES, A_WIDTH).astype(BF16), a_a_up.reshape(LANES, A_WIDTH).astype(BF16), a_g_up.astype(BF16),
      a_k_k.reshape(1, A_WIDTH), a_k_a.reshape(1, A_WIDTH), a_r_k.reshape(1, A_WIDTH))


def _rwkv_chunk_terms(insts, *, hp):
    c, w = insts[0][1].shape
    rows = c * hp
    n = len(insts)
    revs = [inst[0] for inst in insts]
    ti = lax.broadcasted_iota(jnp.int32, (c, c), 0)
    si = lax.broadcasted_iota(jnp.int32, (c, c), 1)
    tri = {False: jnp.where(si <= ti, 1.0, 0.0).astype(BF16), True: jnp.where(si >= ti, 1.0, 0.0).astype(BF16)}
    rt = lax.broadcasted_iota(jnp.int32, (rows, rows), 0)
    cs = lax.broadcasted_iota(jnp.int32, (rows, rows), 1)
    eye = jnp.where(rt == cs, 1.0, 0.0)
    rt = rt & (c - 1)
    cs = cs & (c - 1)
    strict = {False: cs < rt, True: cs > rt}
    incl = {False: cs <= rt, True: cs >= rt}
    lane_head = lax.broadcasted_iota(jnp.int32, (c, w), 1) >> 6
    ew = lax.broadcasted_iota(jnp.int32, (w, w), 0) == lax.broadcasted_iota(jnp.int32, (w, w), 1)

    def stack_f32(x):
        return jnp.concatenate([jnp.where(lane_head == h, x, 0.0) for h in range(hp)], axis=0)

    def stack(x):
        return _bf(stack_f32(x))

    cums = []
    for rev, _, _, _, lw, _, _ in insts:
        hi = lw.astype(BF16)
        mid = (lw - hi.astype(F32)).astype(BF16)
        lo = (lw - hi.astype(F32) - mid.astype(F32)).astype(BF16)
        parts = _dot(tri[rev], jnp.concatenate([hi, mid, lo], axis=1))
        cums.append(parts[:, :w] + parts[:, w:2 * w] + parts[:, 2 * w:])
    tots = [cum[0:1, :] if rev else cum[c - 1:c, :] for rev, cum in zip(revs, cums)]

    xs, rs_f, bs, ks, bh, kh, vs = [], [], [], [], [], [], []
    for (rev, r, v, kk, lw, kz, b), cum, tot in zip(insts, cums, tots):
        r, kk, kz, b = (x.astype(F32) for x in (r, kk, kz, b))
        e_neg = jnp.exp(-cum)
        e_rest = jnp.exp(tot - cum)
        xs.append(stack(kk * jnp.exp(cum - lw)))
        rs_f.append(stack_f32(r * jnp.exp(cum)))
        bs.append(stack(b * e_neg))
        ks.append(stack(kz * e_neg))
        bh.append(stack(b * e_rest))
        kh.append(stack(kz * e_rest))
        vs.append(stack(v))

    grams = [_dot_nt(jnp.concatenate([xs[i], _bf(rs_f[i])], axis=0), jnp.concatenate([bs[i], ks[i]], axis=0))
             for i in range(n)]
    l_b = [jnp.where(strict[revs[i]], grams[i][:rows, :rows], 0.0) for i in range(n)]
    l_k = [_bf(jnp.where(strict[revs[i]], grams[i][:rows, rows:], 0.0)) for i in range(n)]
    incl2 = {rev: jnp.concatenate([m, m], axis=1) for rev, m in incl.items()}
    m_kb = [_bf(jnp.where(incl2[revs[i]], grams[i][rows:, :], 0.0)) for i in range(n)]
    lk_v = [_dot(l_k[i], vs[i]) for i in range(n)]

    t_inv = [eye - l_b[i] for i in range(n)]
    lb = [_bf(x) for x in l_b]
    l_pow = [_dot(x, x) for x in lb]
    for _ in range(int(math.log2(c)) - 2):
        lp = [_bf(x) for x in l_pow]
        both = [_dot(lp[i], jnp.concatenate([_bf(t_inv[i]), lp[i]], axis=1)) for i in range(n)]
        t_inv = [t_inv[i] + both[i][:, :rows] for i in range(n)]
        l_pow = [both[i][:, rows:] for i in range(n)]
    t_inv = [t_inv[i] + _dot(_bf(l_pow[i]), _bf(t_inv[i])) for i in range(n)]

    gh = [_bf(_dot(_bf(t_inv[i]), jnp.concatenate([xs[i], _bf(lk_v[i])], axis=1))) for i in range(n)]
    zero = jnp.zeros((rows, w), BF16)
    lhs = [jnp.concatenate([m_kb[i], jnp.concatenate([bh[i].T, kh[i].T], axis=1)], axis=0) for i in range(n)]
    rhs = [jnp.concatenate([-gh[i], jnp.concatenate([zero, vs[i]], axis=1)], axis=0) for i in range(n)]
    terms = [_dot(lhs[i], rhs[i]) for i in range(n)]
    out = []
    for i in range(n):
        q_p = rs_f[i] + terms[i][:rows, :w]
        y0 = terms[i][:rows, w:]
        phi = jnp.where(ew, jnp.exp(tots[i]), 0.0) + terms[i][rows:, :w]
        psi = terms[i][rows:, w:]
        out.append((_bf(q_p), y0, _bf(phi), psi))
    return out


def _rwkv_scan_body(rf_ref, vf_ref, kkf_ref, lwf_ref, kzf_ref, bf_ref, rb_ref, vb_ref, kkb_ref, lwb_ref, kzb_ref,
                    bb_ref, yf_ref, yb_ref, state_ref, *, hp, gp, cps, chunk):
    c = chunk
    w = A_HEAD * hp

    @pl.when(pl.program_id(2) == 0)
    def _():
        state_ref[...] = jnp.zeros_like(state_ref)

    dirs = ((False, (rf_ref, vf_ref, kkf_ref, lwf_ref, kzf_ref, bf_ref), yf_ref),
            (True, (rb_ref, vb_ref, kkb_ref, lwb_ref, kzb_ref, bb_ref), yb_ref))
    keys = [(d, gi, ci) for d in range(2) for gi in range(gp) for ci in range(cps)]
    insts = [(dirs[d][0], *[x[ci * c:(ci + 1) * c, gi * w:(gi + 1) * w] for x in dirs[d][1]]) for d, gi, ci in keys]
    terms = dict(zip(keys, _rwkv_chunk_terms(insts, hp=hp)))
    chains = [(d, gi) for d in range(2) for gi in range(gp)]
    states = {k: state_ref[k[0], k[1]] for k in chains}
    for step in range(cps):
        for d, gi in chains:
            ci = cps - 1 - step if dirs[d][0] else step
            q_p, y0, phi, psi = terms[(d, gi, ci)]
            a_b = _bf(states[(d, gi)])
            y_st = _dot(q_p, a_b) + y0
            states[(d, gi)] = _dot(phi, a_b) + psi
            y = y_st[0:c, :]
            for h in range(1, hp):
                y = y + y_st[h * c:(h + 1) * c, :]
            dirs[d][2][ci * c:(ci + 1) * c, gi * w:(gi + 1) * w] = y.astype(dirs[d][2].dtype)
    for d, gi in chains:
        state_ref[d, gi] = states[(d, gi)]


def _rwkv_scan(r, v, kk, lw0, kz0, b0, lw1, kz1, b1, batch, seq_len, hp=RWKV_HEADS_PER_GROUP,
               gp=RWKV_GROUPS_PER_STEP, cps=RWKV_CHUNKS_PER_STEP, chunk=RWKV_CHUNK):
    m = r.shape[0]
    w = A_HEAD * hp
    rows = chunk * cps
    ns = seq_len // rows
    fwd = pl.BlockSpec((rows, w * gp), lambda bi, g, n: (bi * ns + n, g))
    bwd = pl.BlockSpec((rows, w * gp), lambda bi, g, n: (bi * ns + ns - 1 - n, g))
    out = jax.ShapeDtypeStruct((m, A_WIDTH), BF16)
    return pl.pallas_call(
        functools.partial(_rwkv_scan_body, hp=hp, gp=gp, cps=cps, chunk=chunk),
        grid=(batch, A_WIDTH // (w * gp), ns),
        in_specs=[fwd] * 6 + [bwd] * 6,
        out_specs=[fwd, bwd],
        out_shape=[out, out],
        scratch_shapes=[pltpu.VMEM((2, gp, w, w), F32)],
        compiler_params=_params(("parallel", "parallel", "arbitrary")),
        name="rwkv_scan",
    )(r, v, kk, lw0, kz0, b0, r, v, kk, lw1, kz1, b1)


def _rwkv_post_body(yf_ref, yb_ref, bonus_ref, g_ref, lnw_ref, lnb_ref, o_ref):
    y = yf_ref[...].astype(F32) + yb_ref[...].astype(F32)
    inv_n = 1.0 / A_HEAD
    mu = _head_sum64(y) * inv_n
    d = y - mu
    var = _head_sum64(d * d) * inv_n
    yn = d * lax.rsqrt(var + A_GN_EPS) * lnw_ref[...] + lnb_ref[...]
    o_ref[...] = ((yn + bonus_ref[...].astype(F32)) * g_ref[...].astype(F32)).astype(o_ref.dtype)


def _rwkv_epilogue(yf, yb, bonus, g, ln_w, ln_b, tm=256, tw=512):
    m = yf.shape[0]
    spec = pl.BlockSpec((tm, tw), lambda i, j: (i, j))
    vec = pl.BlockSpec((1, tw), lambda i, j: (0, j))
    return pl.pallas_call(
        _rwkv_post_body,
        grid=(m // tm, A_WIDTH // tw),
        in_specs=[spec, spec, spec, spec, vec, vec],
        out_specs=spec,
        out_shape=jax.ShapeDtypeStruct((m, A_WIDTH), BF16),
        compiler_params=_params(("parallel", "parallel")),
        name="rwkv_epilogue",
    )(yf, yb, bonus, g, ln_w.reshape(1, A_WIDTH), ln_b.reshape(1, A_WIDTH))


ATTN_QUERY_ROWS = 1024
ATTN_SUBTILES = 4
ATTN_LOOKAHEAD = 2


def _attn_body(q_ref, k_ref, v_ref, o_ref):
    sub = q_ref.shape[0] // ATTN_SUBTILES
    k = k_ref[...]
    v = v_ref[...]

    def scores(i):
        return _dot_nt(q_ref[i * sub:(i + 1) * sub, :], k)

    s = {i: scores(i) for i in range(ATTN_LOOKAHEAD)}
    for i in range(ATTN_SUBTILES):
        si = s.pop(i)
        p = jnp.exp(_bf(si - jnp.max(si, axis=-1, keepdims=True)))
        if i + ATTN_LOOKAHEAD < ATTN_SUBTILES:
            s[i + ATTN_LOOKAHEAD] = scores(i + ATTN_LOOKAHEAD)
        o = _dot(p, v)
        o_ref[i * sub:(i + 1) * sub, :] = (o[:, :B_VDIM] / o[:, B_VDIM:]).astype(o_ref.dtype)


def _attention(q, k, v, batch, seq_len, tq):
    m = q.shape[1]
    nq = seq_len // tq
    return pl.pallas_call(
        _attn_body,
        grid=(B_HEADS, batch, nq),
        in_specs=[pl.BlockSpec((None, tq, B_QK_PAD), lambda h, bi, qi: (h, bi * nq + qi, 0)),
                  pl.BlockSpec((None, seq_len, B_QK_PAD), lambda h, bi, qi: (h, bi, 0)),
                  pl.BlockSpec((None, seq_len, 2 * B_VDIM), lambda h, bi, qi: (h, bi, 0))],
        out_specs=pl.BlockSpec((tq, B_VDIM), lambda h, bi, qi: (bi * nq + qi, h)),
        out_shape=jax.ShapeDtypeStruct((m, B_HEADS * B_VDIM), BF16),
        compiler_params=_params(("parallel", "parallel", "arbitrary")),
        name="mla_attention",
    )(q, k, v)


def _ret_body(lg_ref, q_ref, k_ref, v_ref, *rest, rev, final, chunk, hps):
    if final:
        other_ref, gate_ref, o_ref, state_ref = rest
    else:
        o_ref, state_ref = rest
    c = chunk
    heads = range(hps)
    cols = [slice(h * C_HEAD, (h + 1) * C_HEAD) for h in heads]

    @pl.when(pl.program_id(2) == 0)
    def _():
        state_ref[...] = jnp.zeros_like(state_ref)

    lg = [lg_ref[h, 0:1, 0:1] for h in heads]
    q = [q_ref[:, cols[h]] for h in heads]
    k = [k_ref[:, cols[h]] for h in heads]
    v = [v_ref[:, cols[h]] for h in heads]
    idx = lax.broadcasted_iota(jnp.int32, (c, 1), 0).astype(F32)
    if rev:
        q_pow, k_pow = c - idx, idx
    else:
        q_pow, k_pow = idx + 1.0, c - 1.0 - idx
    qd = [_bf(q[h].astype(F32) * jnp.exp(lg[h] * q_pow)) for h in heads]
    kd = [_bf(k[h].astype(F32) * jnp.exp(lg[h] * k_pow)) for h in heads]
    state = [state_ref[h] for h in heads]
    cross = [_dot(qd[h], _bf(state[h])) for h in heads]
    kv = [_dot_tn(kd[h], v[h]) for h in heads]
    for h in heads:
        state_ref[h] = state[h] * jnp.exp(lg[h] * float(c)) + kv[h]
    if final:
        ti = lax.broadcasted_iota(jnp.int32, (c, c), 0)
        si = lax.broadcasted_iota(jnp.int32, (c, c), 1)
        dist = jnp.abs(ti - si).astype(F32)
        s = [_dot_nt(q[h], k[h]) for h in heads]
        inner = [_dot(_bf(s[h] * jnp.exp(lg[h] * dist)), v[h]) for h in heads]
        for h in heads:
            o = inner[h] + cross[h] + other_ref[:, cols[h]]
            mu = jnp.mean(o, axis=-1, keepdims=True)
            d = o - mu
            var = jnp.mean(d * d, axis=-1, keepdims=True)
            o_ref[:, cols[h]] = (gate_ref[:, cols[h]].astype(F32) * (d * lax.rsqrt(var + C_GN_EPS))).astype(o_ref.dtype)
    else:
        for h in heads:
            o_ref[:, cols[h]] = cross[h]


def _retention_pass(log_gamma, qk, vg, other, batch, seq_len, rev, final, chunk=RET_CHUNK, hps=RET_HEADS_PER_STEP):
    m = qk.shape[0]
    nc = seq_len // chunk
    ng = C_HEADS // hps
    if rev:
        row = lambda bi, n: bi * nc + nc - 1 - n
    else:
        row = lambda bi, n: bi * nc + n
    blk = lambda off: pl.BlockSpec((chunk, C_HEAD * hps), lambda g, bi, n: (row(bi, n), g + off))
    in_specs = [pl.BlockSpec((hps, SUBLANES, LANES), lambda g, bi, n: (g, 0, 0)), blk(0), blk(ng), blk(0)]
    args = [log_gamma, qk, qk, vg]
    if final:
        in_specs += [blk(0), blk(ng)]
        args += [other, vg]
    return pl.pallas_call(
        functools.partial(_ret_body, rev=rev, final=final, chunk=chunk, hps=hps),
        grid=(ng, batch, nc),
        in_specs=in_specs,
        out_specs=blk(0),
        out_shape=jax.ShapeDtypeStruct((m, C_WIDTH), BF16 if final else F32),
        scratch_shapes=[pltpu.VMEM((hps, C_HEAD, C_HEAD), F32)],
        compiler_params=_params(("parallel", "parallel", "arbitrary")),
        name="retention_final" if final else "retention_cross",
    )(*args)


def _ffn_up_body(x_ref, xp_ref, xn_ref, wg_ref, wv_ref, cw_ref, cb_ref, o_ref, xext_ref, *, tiles_per_seq):
    tm = x_ref.shape[0]
    halo = xp_ref.shape[0]

    @pl.when(pl.program_id(1) == 0)
    def _():
        pos = pl.program_id(0) % tiles_per_seq
        xext_ref[0:halo, :] = jnp.where(pos == 0, jnp.zeros_like(xp_ref[...]), xp_ref[...])
        xext_ref[halo:halo + tm, :] = x_ref[...]
        xext_ref[halo + tm:, :] = jnp.where(pos == tiles_per_seq - 1, jnp.zeros_like(xn_ref[...]), xn_ref[...])

    g_ext = _dot(xext_ref[...], wg_ref[...])
    val = _dot(xext_ref[halo:halo + tm, :], wv_ref[...])
    rows = tm + 2 * halo
    g = g_ext[halo:halo + tm]
    gp = pltpu.roll(g_ext, 1, axis=0)[halo:halo + tm]
    gn = pltpu.roll(g_ext, rows - 1, axis=0)[halo:halo + tm]
    conv = cw_ref[0:1, :] * gp + cw_ref[1:2, :] * g + cw_ref[2:3, :] * gn + cb_ref[...]
    o_ref[...] = (_silu(conv) * val).astype(o_ref.dtype)


def _ffn_up_act(xn, w_up, conv_w, conv_b, seq_len, tm=MM_ROWS, tn=256):
    m, k = xn.shape
    nj = FFN_HIDDEN // tn
    halo = 2 * SUBLANES
    return pl.pallas_call(
        functools.partial(_ffn_up_body, tiles_per_seq=seq_len // tm),
        grid=(m // tm, nj),
        in_specs=[*_halo_specs(tm, k, lambda j: 0, m, halo=halo),
                  pl.BlockSpec((k, tn), lambda i, j: (0, j)),
                  pl.BlockSpec((k, tn), lambda i, j: (0, nj + j)),
                  pl.BlockSpec((3, tn), lambda i, j: (0, j)),
                  pl.BlockSpec((1, tn), lambda i, j: (0, j))],
        out_specs=pl.BlockSpec((tm, tn), lambda i, j: (i, j)),
        out_shape=jax.ShapeDtypeStruct((m, FFN_HIDDEN), BF16),
        scratch_shapes=[pltpu.VMEM((tm + 2 * halo, k), BF16)],
        compiler_params=_params(("parallel", "arbitrary")),
        name="ffn_up_conv_gate",
    )(xn, xn, xn, w_up, w_up, conv_w, conv_b.reshape(1, FFN_HIDDEN))


def _rope_tables(seq_len, dim):
    inv = ROPE_BASE ** (-jnp.arange(0, dim, 2, dtype=F32) / dim)
    ang = jnp.arange(seq_len, dtype=F32)[:, None] * inv[None, :]
    return jnp.cos(ang), jnp.sin(ang)


def _swap_halves_cols(w):
    h = w.shape[-1] // 2
    return jnp.concatenate([w[..., h:], w[..., :h]], axis=-1)


def _prep_layer(w_in, b_q_up, b_kv_up, w_branch, w_out, ffn_up, ffn_down):
    o_b = A_IN
    o_kpe = o_b + B_Q_LORA + B_KV_LORA
    o_c = o_kpe + B_ROPE
    o_g = o_c + 4 * C_WIDTH
    w_kpe = w_in[:, o_kpe:o_c]
    qh = b_q_up.reshape(B_Q_LORA, B_HEADS, B_NOPE + B_ROPE)
    q_pe = qh[..., B_NOPE:]
    w_q = jnp.concatenate([qh[..., :B_NOPE], q_pe, _swap_halves_cols(q_pe)], axis=-1)
    return dict(
        w_a=jnp.pad(w_in[:, :A_IN], ((0, 0), (0, LANES))).astype(BF16),
        w_b=w_in[:, o_b:o_kpe].astype(BF16),
        w_kpe=jnp.concatenate([w_kpe, _swap_halves_cols(w_kpe)], axis=-1).astype(BF16),
        w_cqk=w_in[:, o_c:o_c + 2 * C_WIDTH].astype(BF16),
        w_cvg=w_in[:, o_c + 2 * C_WIDTH:o_g].astype(BF16),
        w_g=w_in[:, o_g:].astype(BF16),
        w_q=jnp.transpose(w_q, (1, 0, 2)).astype(BF16),
        w_kv=jnp.transpose(b_kv_up.reshape(B_KV_LORA, B_HEADS, B_NOPE + B_VDIM), (1, 0, 2)).astype(BF16),
        w_br=w_branch.reshape(3, A_WIDTH, D_MODEL).astype(BF16),
        w_out=w_out.astype(BF16),
        ffn_up=ffn_up.astype(BF16),
        ffn_down=ffn_down.astype(BF16),
    )


def _layer(x, batch, seq_len, tabs, pw, attn_norm, a_mu, a_w0, a_w_up, a_a0, a_a_up, a_g_up, a_k_k, a_k_a, a_r_k,
           a_ln_w, a_ln_b, b_q_norm, b_kv_norm, ffn_norm, ffn_conv, ffn_conv_b):
    m = x.shape[0]
    tps = lambda tm: seq_len // tm
    xn = _rmsnorm(x, attn_norm, BF16)

    (ha,) = _matmul("in_proj_a", xn, pw["w_a"], tm=MM_ROWS, tn=512, epilogue=_ep_plain, out_dtypes=[BF16])
    r, v, kk, g, bonus, lw0, lw1, kz0, kz1, b0, b1 = _rwkv_prologue(
        ha, a_mu, a_w0, a_w_up, a_a0, a_a_up, a_g_up, a_k_k, a_k_a, a_r_k, seq_len)
    yf, yb = _rwkv_scan(r, v, kk, lw0, kz0, b0, lw1, kz1, b1, batch, seq_len)
    oa = _rwkv_epilogue(yf, yb, bonus, g, a_ln_w, a_ln_b)

    tm_b = 512
    vec = lambda n: pl.BlockSpec((1, n), lambda i, j: (0, 0))
    qn, kvn = _matmul("in_proj_b", xn, pw["w_b"], tm=tm_b, tn=B_Q_LORA + B_KV_LORA, epilogue=_ep_qkv_norms,
                      out_dtypes=[BF16, BF16], out_widths=[B_Q_LORA, B_KV_LORA],
                      extras=(b_q_norm.reshape(1, -1), b_kv_norm.reshape(1, -1)),
                      extra_specs=(vec(B_Q_LORA), vec(B_KV_LORA)))
    tm_up = 2 * MM_ROWS
    tab64 = lambda tm: pl.BlockSpec((tm, LANES), lambda i, j: (i % tps(tm), 0))
    (kpe,) = _matmul("in_proj_kpe", xn, pw["w_kpe"], tm=MM_ROWS, tn=LANES, epilogue=_ep_kpe, out_dtypes=[BF16],
                     extras=(tabs["cos64"], tabs["sin64"]), extra_specs=(tab64(MM_ROWS), tab64(MM_ROWS)))
    (q,) = _matmul("q_up", qn, pw["w_q"], tm=tm_up, tn=B_QK_PAD, epilogue=_ep_q_up, out_dtypes=[BF16],
                   extras=(tabs["cos64"], tabs["sin64"]), extra_specs=(tab64(tm_up), tab64(tm_up)),
                   b_head_major=True)
    k, vv = _matmul("kv_up", kvn, pw["w_kv"], tm=tm_up, tn=B_NOPE + B_VDIM, epilogue=_ep_kv_up,
                    out_dtypes=[BF16, BF16], out_widths=[B_QK_PAD, 2 * B_VDIM],
                    extras=(kpe,), extra_specs=(pl.BlockSpec((tm_up, LANES), lambda i, j: (i, 0)),),
                    b_head_major=True)
    ob = _attention(q, k, vv, batch, seq_len, tq=ATTN_QUERY_ROWS)

    tab128 = pl.BlockSpec((MM_ROWS, LANES), lambda i, j: (i % tps(MM_ROWS), 0))
    q_tiles = C_WIDTH // 512
    (qk,) = _matmul("in_proj_c_qk", xn, pw["w_cqk"], tm=MM_ROWS, tn=512,
                    epilogue=functools.partial(_ep_ret_qk, q_tiles=q_tiles), out_dtypes=[BF16],
                    extras=(tabs["cos256"], tabs["sin256"]), extra_specs=(tab128, tab128))
    (vg,) = _matmul("in_proj_c_vg", xn, pw["w_cvg"], tm=MM_ROWS, tn=512,
                    epilogue=functools.partial(_ep_ret_vg, v_tiles=q_tiles), out_dtypes=[BF16])
    cross_b = _retention_pass(tabs["log_gamma"], qk, vg, None, batch, seq_len, rev=True, final=False)
    oc = _retention_pass(tabs["log_gamma"], qk, vg, cross_b, batch, seq_len, rev=False, final=True)

    (gates,) = _matmul("in_proj_gates", xn, pw["w_g"], tm=MM_ROWS, tn=1024, epilogue=_ep_sigmoid, out_dtypes=[BF16])
    merged = _merge(oa, ob, oc, pw["w_br"], gates)
    res_spec = lambda tm, tn: pl.BlockSpec((tm, tn), lambda i, j: (i, j))
    (x,) = _matmul("out_proj", merged, pw["w_out"], tm=MM_ROWS, tn=512, epilogue=_ep_residual, out_dtypes=[F32],
                   extras=(x,), extra_specs=(res_spec(MM_ROWS, 512),))

    xn2 = _rmsnorm(x, ffn_norm, BF16)
    act = _ffn_up_act(xn2, pw["ffn_up"], ffn_conv, ffn_conv_b, seq_len)
    (x,) = _matmul("ffn_down", act, pw["ffn_down"], tm=512, tn=512, epilogue=_ep_residual, out_dtypes=[F32],
                   extras=(x,), extra_specs=(res_spec(512, 512),))
    return x


def _tables(seq_len):
    c64, s64 = _rope_tables(seq_len, B_ROPE)
    z = jnp.zeros((seq_len, B_ROPE), F32)
    c256, s256 = _rope_tables(seq_len, C_HEAD)
    log_gamma = jnp.log1p(-(2.0 ** (-5.0 - jnp.arange(C_HEADS, dtype=F32))))
    return dict(
        cos64=jnp.concatenate([c64, c64, z], axis=1),
        sin64=jnp.concatenate([-s64, s64, z], axis=1),
        cos256=c256, sin256=s256,
        log_gamma=jnp.broadcast_to(log_gamma[:, None, None], (C_HEADS, SUBLANES, LANES)),
    )


def _encoder(x3, prepped, per_layer, final_norm):
    batch, seq_len, d = x3.shape
    x = x3.reshape(batch * seq_len, d)
    tabs = _tables(seq_len)
    for pw, lw in zip(prepped, per_layer):
        x = _layer(x, batch, seq_len, tabs, pw, *lw)
    return _rmsnorm(x, final_norm, F32).reshape(batch, seq_len, d)


def kernel(x_prompt, x_sample, attn_norm, w_in, a_mu, a_w0, a_w_up, a_a0, a_a_up, a_g_up, a_k_k, a_k_a, a_r_k, a_ln_w, a_ln_b, b_q_norm, b_q_up, b_kv_norm, b_kv_up, w_branch, w_out, ffn_norm, ffn_up, ffn_conv, ffn_conv_b, ffn_down, final_norm):
    depth = w_in.shape[0]
    prepped = [_prep_layer(w_in[l], b_q_up[l], b_kv_up[l], w_branch[l], w_out[l], ffn_up[l], ffn_down[l])
               for l in range(depth)]
    per_layer = [(attn_norm[l], a_mu[l], a_w0[l], a_w_up[l], a_a0[l], a_a_up[l], a_g_up[l], a_k_k[l], a_k_a[l],
                  a_r_k[l], a_ln_w[l], a_ln_b[l], b_q_norm[l], b_kv_norm[l], ffn_norm[l], ffn_conv[l], ffn_conv_b[l])
                 for l in range(depth)]
    return (_encoder(x_prompt, prepped, per_layer, final_norm), _encoder(x_sample, prepped, per_layer, final_norm))
```

```python
import functools
import math

import jax
import jax.numpy as jnp
from jax import lax
from jax.experimental import pallas as pl
from jax.experimental.pallas import tpu as pltpu

F32 = jnp.float32
BF16 = jnp.bfloat16

D_MODEL = 4096
DEPTH = 2
A_HEAD = 64
A_WIDTH = 2048
A_LORA_IN = 384
A_IN = 3 * A_WIDTH + A_LORA_IN
A_GN_EPS = 64e-5
B_NOPE = 128
B_ROPE = 64
B_VDIM = 128
B_HEADS = 16
B_Q_LORA = 1536
B_KV_LORA = 512
B_QK_PAD = 256
C_HEAD = 256
C_HEADS = 8
C_WIDTH = 2048
C_GN_EPS = 1e-5
FFN_HIDDEN = 11008
ROPE_BASE = 10000.0
NORM_EPS = 1e-6

LANES = 128
SUBLANES = 8
VMEM_LIMIT_BYTES = 56 * 2**20
MM_ROWS = 1024

RWKV_CHUNK = 64
RWKV_HEADS_PER_GROUP = 2
RWKV_GROUPS_PER_STEP = 2
RWKV_CHUNKS_PER_STEP = 4
RET_CHUNK = 256
RET_HEADS_PER_STEP = 8


def _params(sem):
    return pltpu.CompilerParams(dimension_semantics=sem, vmem_limit_bytes=VMEM_LIMIT_BYTES)


def _dot(a, b):
    return jnp.dot(a, b, preferred_element_type=F32)


def _dot_nt(a, b):
    return lax.dot_general(a, b, (((1,), (1,)), ((), ())), preferred_element_type=F32)


def _dot_tn(a, b):
    return lax.dot_general(a, b, (((0,), (0,)), ((), ())), preferred_element_type=F32)


def _bf(x):
    return x.astype(BF16)


def _sigmoid(x):
    return 1.0 / (1.0 + jnp.exp(-x))


def _silu(x):
    return x * _sigmoid(x)


def _rmsnorm_body(x_ref, g_ref, o_ref):
    x = x_ref[...]
    y = x * lax.rsqrt(jnp.mean(x * x, axis=-1, keepdims=True) + NORM_EPS)
    o_ref[...] = (y * g_ref[...]).astype(o_ref.dtype)


def _rmsnorm(x, g, out_dtype, tm=256):
    m, d = x.shape
    return pl.pallas_call(
        _rmsnorm_body,
        grid=(m // tm,),
        in_specs=[pl.BlockSpec((tm, d), lambda i: (i, 0)), pl.BlockSpec((1, d), lambda i: (0, 0))],
        out_specs=pl.BlockSpec((tm, d), lambda i: (i, 0)),
        out_shape=jax.ShapeDtypeStruct((m, d), out_dtype),
        compiler_params=_params(("parallel",)),
        name="rmsnorm",
    )(x, g.reshape(1, d))


def _mm_body(a_ref, b_ref, *rest, n_extra, epilogue):
    extras = rest[:n_extra]
    outs = rest[n_extra:]
    acc = _dot(a_ref[...], b_ref[...])
    res = epilogue(acc, *extras)
    for o_ref, r in zip(outs, res):
        o_ref[...] = r.astype(o_ref.dtype)


def _matmul(name, a, b, *, tm, tn, epilogue, out_dtypes, out_widths=None, extras=(), extra_specs=(),
            b_head_major=False):
    m, k = a.shape
    if b_head_major:
        nj = b.shape[0]
        b_spec = pl.BlockSpec((None, k, tn), lambda i, j: (j, 0, 0))
    else:
        nj = b.shape[1] // tn
        b_spec = pl.BlockSpec((k, tn), lambda i, j: (0, j))
    out_widths = out_widths or [tn] * len(out_dtypes)
    if b_head_major:
        out_specs = [pl.BlockSpec((None, tm, w), lambda i, j: (j, i, 0)) for w in out_widths]
        out_shape = [jax.ShapeDtypeStruct((nj, m, w), dt) for w, dt in zip(out_widths, out_dtypes)]
    else:
        out_specs = [pl.BlockSpec((tm, w), lambda i, j: (i, j)) for w in out_widths]
        out_shape = [jax.ShapeDtypeStruct((m, nj * w), dt) for w, dt in zip(out_widths, out_dtypes)]
    return pl.pallas_call(
        functools.partial(_mm_body, n_extra=len(extras), epilogue=epilogue),
        grid=(m // tm, nj),
        in_specs=[pl.BlockSpec((tm, k), lambda i, j: (i, 0)), b_spec, *extra_specs],
        out_specs=out_specs,
        out_shape=out_shape,
        compiler_params=_params(("parallel", "arbitrary")),
        name=name,
    )(a, b, *extras)


def _ep_plain(acc):
    return (acc,)


def _ep_sigmoid(acc):
    return (_sigmoid(acc),)


def _ep_residual(acc, res_ref):
    return (acc + res_ref[...],)


def _ep_qkv_norms(acc, gq_ref, gkv_ref):
    q = acc[:, :B_Q_LORA]
    kv = acc[:, B_Q_LORA:]
    qn = q * lax.rsqrt(jnp.mean(q * q, axis=-1, keepdims=True) + NORM_EPS) * gq_ref[...]
    kvn = kv * lax.rsqrt(jnp.mean(kv * kv, axis=-1, keepdims=True) + NORM_EPS) * gkv_ref[...]
    return qn, kvn


def _rope64_paired(x, cos_ref, sin_ref):
    return x * cos_ref[...] + pltpu.roll(x, 64, axis=1) * sin_ref[...]


def _ep_kpe(acc, cos_ref, sin_ref):
    return (_rope64_paired(acc, cos_ref, sin_ref),)


def _ep_q_up(acc, cos_ref, sin_ref):
    scale = (B_NOPE + B_ROPE) ** -0.5
    return (jnp.concatenate([acc[:, :B_NOPE], _rope64_paired(acc[:, B_NOPE:], cos_ref, sin_ref)], axis=1) * scale,)


def _ep_kv_up(acc, kpe_ref):
    k = jnp.concatenate([acc[:, :B_NOPE].astype(BF16), kpe_ref[...]], axis=1)
    v = acc[:, B_NOPE:]
    return k, jnp.concatenate([v, jnp.ones_like(v)], axis=1)


def _ep_ret_qk(acc, cos_ref, sin_ref, *, q_tiles):
    cos = cos_ref[...]
    sin = sin_ref[...]
    half = C_HEAD // 2
    pieces = []
    for h in range(acc.shape[1] // C_HEAD):
        x1 = acc[:, h * C_HEAD:h * C_HEAD + half]
        x2 = acc[:, h * C_HEAD + half:(h + 1) * C_HEAD]
        pieces += [x1 * cos - x2 * sin, x1 * sin + x2 * cos]
    out = jnp.concatenate(pieces, axis=1)
    scale = jnp.where(pl.program_id(1) < q_tiles, C_HEAD ** -0.5, 1.0)
    return (out * scale,)


def _ep_ret_vg(acc, *, v_tiles):
    return (jnp.where(pl.program_id(1) < v_tiles, acc, _silu(acc)),)


def _merge_body(oa_ref, ob_ref, oc_ref, w_ref, ga_ref, gb_ref, gc_ref, out_ref):
    terms = [g_ref[...].astype(F32) * _dot(o_ref[...], w_ref[br])
             for br, (o_ref, g_ref) in enumerate(((oa_ref, ga_ref), (ob_ref, gb_ref), (oc_ref, gc_ref)))]
    out_ref[...] = (terms[0] + terms[1] + terms[2]).astype(out_ref.dtype)


def _merge(oa, ob, oc, w3, gates, tm=MM_ROWS, tn=512):
    m, kb = oa.shape
    n = w3.shape[2]
    nj = n // tn
    o_spec = pl.BlockSpec((tm, kb), lambda i, j: (i, 0))
    gate_specs = [pl.BlockSpec((tm, tn), lambda i, j, br=br: (i, br * nj + j)) for br in range(3)]
    return pl.pallas_call(
        _merge_body,
        grid=(m // tm, nj),
        in_specs=[o_spec, o_spec, o_spec, pl.BlockSpec((3, kb, tn), lambda i, j: (0, 0, j)), *gate_specs],
        out_specs=pl.BlockSpec((tm, tn), lambda i, j: (i, j)),
        out_shape=jax.ShapeDtypeStruct((m, n), BF16),
        compiler_params=_params(("parallel", "arbitrary")),
        name="branch_merge",
    )(oa, ob, oc, w3, gates, gates, gates)


def _halo_specs(tm, w, col_block, n_rows, halo=SUBLANES):
    per_tile = tm // halo
    last = n_rows // halo - 1
    main = pl.BlockSpec((tm, w), lambda i, j: (i, col_block(j)))
    prev = pl.BlockSpec((halo, w), lambda i, j: (jnp.maximum(i * per_tile - 1, 0), col_block(j)))
    nxt = pl.BlockSpec((halo, w), lambda i, j: (jnp.minimum((i + 1) * per_tile, last), col_block(j)))
    return [main, prev, nxt]


def _shifted(x, prev_ref, next_ref, tiles_per_seq):
    tm = x.shape[0]
    i = pl.program_id(0)
    pos = i % tiles_per_seq
    halo = prev_ref.shape[0]
    prev_row = jnp.where(pos == 0, 0.0, prev_ref[halo - 1:halo, :].astype(F32))
    next_row = jnp.where(pos == tiles_per_seq - 1, 0.0, next_ref[0:1, :].astype(F32))
    row = lax.broadcasted_iota(jnp.int32, x.shape, 0)
    xp = jnp.where(row == 0, prev_row, pltpu.roll(x, 1, axis=0))
    xn = jnp.where(row == tm - 1, next_row, pltpu.roll(x, tm - 1, axis=0))
    return xp, xn


def _head_sum64(x):
    w = x.shape[1]
    r = lax.broadcasted_iota(jnp.int32, (w, w), 0) >> 6
    c = lax.broadcasted_iota(jnp.int32, (w, w), 1) >> 6
    ones = jnp.where(r == c, 1.0, 0.0).astype(BF16)
    return _dot(_bf(x), ones)


def _rwkv_pre_body(r_ref, rp_ref, rn_ref, k_ref, kp_ref, kn_ref, v_ref, vp_ref, vn_ref, l_ref, lp_ref, ln_ref,
                   mur_ref, muk_ref, muv_ref, mul_ref, w0_ref, a0_ref, wup_ref, aup_ref, gup_ref,
                   kk_w_ref, ka_ref, rk_ref,
                   r_out, v_out, kk_out, g_out, bonus_out, lw0_out, lw1_out, kz0_out, kz1_out, b0_out, b1_out,
                   *, tiles_per_seq):
    def mix(x_ref, p_ref, n_ref, mu_ref):
        x = x_ref[...].astype(F32)
        xp, xn = _shifted(x, p_ref, n_ref, tiles_per_seq)
        return x + mu_ref[0:1, :] * (xp - x) + mu_ref[1:2, :] * (xn - x)

    r = mix(r_ref, rp_ref, rn_ref, mur_ref)
    k = mix(k_ref, kp_ref, kn_ref, muk_ref)
    v = mix(v_ref, vp_ref, vn_ref, muv_ref)
    lo = mix(l_ref, lp_ref, ln_ref, mul_ref)
    wd = jnp.tanh(lo[:, 0:LANES])
    ad = lo[:, LANES:2 * LANES]
    gd = _sigmoid(lo[:, 2 * LANES:3 * LANES])
    lane = lax.broadcasted_iota(jnp.int32, (1, LANES), 1)

    g_out[...] = _dot(_bf(gd), gup_ref[...]).astype(g_out.dtype)
    kk = k * kk_w_ref[...]
    kk = kk * lax.rsqrt(jnp.maximum(_head_sum64(kk * kk), 1e-24))
    r_out[...] = r.astype(r_out.dtype)
    v_out[...] = v.astype(v_out.dtype)
    kk_out[...] = kk.astype(kk_out.dtype)
    bonus_out[...] = (_head_sum64(r * k * rk_ref[...]) * v).astype(bonus_out.dtype)
    for z, (lw_out, kz_out, b_out) in enumerate(((lw0_out, kz0_out, b0_out), (lw1_out, kz1_out, b1_out))):
        sel = (lane >> 6) == z
        w_pre = w0_ref[z:z + 1, :] + _dot(_bf(jnp.where(sel, wd, 0.0)), wup_ref[...])
        lw_out[...] = -math.exp(-0.5) * _sigmoid(w_pre)
        a = _sigmoid(a0_ref[z:z + 1, :] + _dot(_bf(jnp.where(sel, ad, 0.0)), aup_ref[...]))
        kz_out[...] = (k * (1.0 + (a - 1.0) * ka_ref[...])).astype(kz_out.dtype)
        b_out[...] = (kk * a).astype(b_out.dtype)


def _rwkv_prologue(ha, a_mu, a_w0, a_w_up, a_a0, a_a_up, a_g_up, a_k_k, a_k_a, a_r_k, seq_len, tm=256, tw=512):
    m = ha.shape[0]
    ncb = A_WIDTH // tw
    lora_block = 3 * A_WIDTH // A_LORA_IN
    halo = 2 * SUBLANES
    in_specs = []
    for s in range(3):
        in_specs += _halo_specs(tm, tw, lambda j, s=s: s * ncb + j, m, halo=halo)
    in_specs += _halo_specs(tm, A_LORA_IN, lambda j: lora_block, m, halo=halo)
    for s in range(3):
        in_specs.append(pl.BlockSpec((2, tw), lambda i, j, s=s: (0, s * ncb + j)))
    in_specs.append(pl.BlockSpec((2, A_LORA_IN), lambda i, j: (0, lora_block)))
    vec2 = pl.BlockSpec((2, tw), lambda i, j: (0, j))
    up = pl.BlockSpec((LANES, tw), lambda i, j: (0, j))
    vec1 = pl.BlockSpec((1, tw), lambda i, j: (0, j))
    in_specs += [vec2, vec2, up, up, up, vec1, vec1, vec1]
    out_spec = pl.BlockSpec((tm, tw), lambda i, j: (i, j))
    out_dtypes = [BF16, BF16, BF16, BF16, BF16, F32, F32, BF16, BF16, BF16, BF16]
    return pl.pallas_call(
        functools.partial(_rwkv_pre_body, tiles_per_seq=seq_len // tm),
        grid=(m // tm, ncb),
        in_specs=in_specs,
        out_specs=[out_spec] * 11,
        out_shape=[jax.ShapeDtypeStruct((m, A_WIDTH), dt) for dt in out_dtypes],
        compiler_params=_params(("parallel", "arbitrary")),
        name="rwkv_prologue",
    )(ha, ha, ha, ha, ha, ha, ha, ha, ha, ha, ha, ha, a_mu, a_mu, a_mu, a_mu, a_w0, a_a0,
      a_w_up.reshape(LANES, A_WIDTH).astype(BF16), a_a_up.reshape(LANES, A_WIDTH).astype(BF16), a_g_up.astype(BF16),
      a_k_k.reshape(1, A_WIDTH), a_k_a.reshape(1, A_WIDTH), a_r_k.reshape(1, A_WIDTH))


def _rwkv_chunk_terms(insts, *, hp):
    c, w = insts[0][1].shape
    rows = c * hp
    n = len(insts)
    revs = [inst[0] for inst in insts]
    ti = lax.broadcasted_iota(jnp.int32, (c, c), 0)
    si = lax.broadcasted_iota(jnp.int32, (c, c), 1)
    tri = {False: jnp.where(si <= ti, 1.0, 0.0).astype(BF16), True: jnp.where(si >= ti, 1.0, 0.0).astype(BF16)}
    rt = lax.broadcasted_iota(jnp.int32, (rows, rows), 0)
    cs = lax.broadcasted_iota(jnp.int32, (rows, rows), 1)
    eye = jnp.where(rt == cs, 1.0, 0.0)
    rt = rt & (c - 1)
    cs = cs & (c - 1)
    strict = {False: cs < rt, True: cs > rt}
    incl = {False: cs <= rt, True: cs >= rt}
    lane_head = lax.broadcasted_iota(jnp.int32, (c, w), 1) >> 6
    ew = lax.broadcasted_iota(jnp.int32, (w, w), 0) == lax.broadcasted_iota(jnp.int32, (w, w), 1)

    def stack_f32(x):
        return jnp.concatenate([jnp.where(lane_head == h, x, 0.0) for h in range(hp)], axis=0)

    def stack(x):
        return _bf(stack_f32(x))

    cums = []
    for rev, _, _, _, lw, _, _ in insts:
        hi = lw.astype(BF16)
        mid = (lw - hi.astype(F32)).astype(BF16)
        lo = (lw - hi.astype(F32) - mid.astype(F32)).astype(BF16)
        parts = _dot(tri[rev], jnp.concatenate([hi, mid, lo], axis=1))
        cums.append(parts[:, :w] + parts[:, w:2 * w] + parts[:, 2 * w:])
    tots = [cum[0:1, :] if rev else cum[c - 1:c, :] for rev, cum in zip(revs, cums)]

    xs, rs_f, bs, ks, bh, kh, vs = [], [], [], [], [], [], []
    for (rev, r, v, kk, lw, kz, b), cum, tot in zip(insts, cums, tots):
        r, kk, kz, b = (x.astype(F32) for x in (r, kk, kz, b))
        e_neg = jnp.exp(-cum)
        e_rest = jnp.exp(tot - cum)
        xs.append(stack(kk * jnp.exp(cum - lw)))
        rs_f.append(stack_f32(r * jnp.exp(cum)))
        bs.append(stack(b * e_neg))
        ks.append(stack(kz * e_neg))
        bh.append(stack(b * e_rest))
        kh.append(stack(kz * e_rest))
        vs.append(stack(v))

    grams = [_dot_nt(jnp.concatenate([xs[i], _bf(rs_f[i])], axis=0), jnp.concatenate([bs[i], ks[i]], axis=0))
             for i in range(n)]
    l_b = [jnp.where(strict[revs[i]], grams[i][:rows, :rows], 0.0) for i in range(n)]
    l_k = [_bf(jnp.where(strict[revs[i]], grams[i][:rows, rows:], 0.0)) for i in range(n)]
    incl2 = {rev: jnp.concatenate([m, m], axis=1) for rev, m in incl.items()}
    m_kb = [_bf(jnp.where(incl2[revs[i]], grams[i][rows:, :], 0.0)) for i in range(n)]
    lk_v = [_dot(l_k[i], vs[i]) for i in range(n)]

    t_inv = [eye - l_b[i] for i in range(n)]
    lb = [_bf(x) for x in l_b]
    l_pow = [_dot(x, x) for x in lb]
    for _ in range(int(math.log2(c)) - 2):
        lp = [_bf(x) for x in l_pow]
        both = [_dot(lp[i], jnp.concatenate([_bf(t_inv[i]), lp[i]], axis=1)) for i in range(n)]
        t_inv = [t_inv[i] + both[i][:, :rows] for i in range(n)]
        l_pow = [both[i][:, rows:] for i in range(n)]
    t_inv = [t_inv[i] + _dot(_bf(l_pow[i]), _bf(t_inv[i])) for i in range(n)]

    gh = [_bf(_dot(_bf(t_inv[i]), jnp.concatenate([xs[i], _bf(lk_v[i])], axis=1))) for i in range(n)]
    zero = jnp.zeros((rows, w), BF16)
    lhs = [jnp.concatenate([m_kb[i], jnp.concatenate([bh[i].T, kh[i].T], axis=1)], axis=0) for i in range(n)]
    rhs = [jnp.concatenate([-gh[i], jnp.concatenate([zero, vs[i]], axis=1)], axis=0) for i in range(n)]
    terms = [_dot(lhs[i], rhs[i]) for i in range(n)]
    out = []
    for i in range(n):
        q_p = rs_f[i] + terms[i][:rows, :w]
        y0 = terms[i][:rows, w:]
        phi = jnp.where(ew, jnp.exp(tots[i]), 0.0) + terms[i][rows:, :w]
        psi = terms[i][rows:, w:]
        out.append((_bf(q_p), y0, _bf(phi), psi))
    return out


def _rwkv_scan_body(rf_ref, vf_ref, kkf_ref, lwf_ref, kzf_ref, bf_ref, rb_ref, vb_ref, kkb_ref, lwb_ref, kzb_ref,
                    bb_ref, yf_ref, yb_ref, state_ref, *, hp, gp, cps, chunk):
    c = chunk
    w = A_HEAD * hp

    @pl.when(pl.program_id(2) == 0)
    def _():
        state_ref[...] = jnp.zeros_like(state_ref)

    dirs = ((False, (rf_ref, vf_ref, kkf_ref, lwf_ref, kzf_ref, bf_ref), yf_ref),
            (True, (rb_ref, vb_ref, kkb_ref, lwb_ref, kzb_ref, bb_ref), yb_ref))
    keys = [(d, gi, ci) for d in range(2) for gi in range(gp) for ci in range(cps)]
    insts = [(dirs[d][0], *[x[ci * c:(ci + 1) * c, gi * w:(gi + 1) * w] for x in dirs[d][1]]) for d, gi, ci in keys]
    terms = dict(zip(keys, _rwkv_chunk_terms(insts, hp=hp)))
    chains = [(d, gi) for d in range(2) for gi in range(gp)]
    states = {k: state_ref[k[0], k[1]] for k in chains}
    for step in range(cps):
        for d, gi in chains:
            ci = cps - 1 - step if dirs[d][0] else step
            q_p, y0, phi, psi = terms[(d, gi, ci)]
            a_b = _bf(states[(d, gi)])
            y_st = _dot(q_p, a_b) + y0
            states[(d, gi)] = _dot(phi, a_b) + psi
            y = y_st[0:c, :]
            for h in range(1, hp):
                y = y + y_st[h * c:(h + 1) * c, :]
            dirs[d][2][ci * c:(ci + 1) * c, gi * w:(gi + 1) * w] = y.astype(dirs[d][2].dtype)
    for d, gi in chains:
        state_ref[d, gi] = states[(d, gi)]


def _rwkv_scan(r, v, kk, lw0, kz0, b0, lw1, kz1, b1, batch, seq_len, hp=RWKV_HEADS_PER_GROUP,
               gp=RWKV_GROUPS_PER_STEP, cps=RWKV_CHUNKS_PER_STEP, chunk=RWKV_CHUNK):
    m = r.shape[0]
    w = A_HEAD * hp
    rows = chunk * cps
    ns = seq_len // rows
    fwd = pl.BlockSpec((rows, w * gp), lambda bi, g, n: (bi * ns + n, g))
    bwd = pl.BlockSpec((rows, w * gp), lambda bi, g, n: (bi * ns + ns - 1 - n, g))
    out = jax.ShapeDtypeStruct((m, A_WIDTH), BF16)
    return pl.pallas_call(
        functools.partial(_rwkv_scan_body, hp=hp, gp=gp, cps=cps, chunk=chunk),
        grid=(batch, A_WIDTH // (w * gp), ns),
        in_specs=[fwd] * 6 + [bwd] * 6,
        out_specs=[fwd, bwd],
        out_shape=[out, out],
        scratch_shapes=[pltpu.VMEM((2, gp, w, w), F32)],
        compiler_params=_params(("parallel", "parallel", "arbitrary")),
        name="rwkv_scan",
    )(r, v, kk, lw0, kz0, b0, r, v, kk, lw1, kz1, b1)


def _rwkv_post_body(yf_ref, yb_ref, bonus_ref, g_ref, lnw_ref, lnb_ref, o_ref):
    y = yf_ref[...].astype(F32) + yb_ref[...].astype(F32)
    inv_n = 1.0 / A_HEAD
    mu = _head_sum64(y) * inv_n
    d = y - mu
    var = _head_sum64(d * d) * inv_n
    yn = d * lax.rsqrt(var + A_GN_EPS) * lnw_ref[...] + lnb_ref[...]
    o_ref[...] = ((yn + bonus_ref[...].astype(F32)) * g_ref[...].astype(F32)).astype(o_ref.dtype)


def _rwkv_epilogue(yf, yb, bonus, g, ln_w, ln_b, tm=256, tw=512):
    m = yf.shape[0]
    spec = pl.BlockSpec((tm, tw), lambda i, j: (i, j))
    vec = pl.BlockSpec((1, tw), lambda i, j: (0, j))
    return pl.pallas_call(
        _rwkv_post_body,
        grid=(m // tm, A_WIDTH // tw),
        in_specs=[spec, spec, spec, spec, vec, vec],
        out_specs=spec,
        out_shape=jax.ShapeDtypeStruct((m, A_WIDTH), BF16),
        compiler_params=_params(("parallel", "parallel")),
        name="rwkv_epilogue",
    )(yf, yb, bonus, g, ln_w.reshape(1, A_WIDTH), ln_b.reshape(1, A_WIDTH))


ATTN_QUERY_ROWS = 1024
ATTN_SUBTILES = 4
ATTN_LOOKAHEAD = 2


def _attn_body(q_ref, k_ref, v_ref, o_ref):
    sub = q_ref.shape[0] // ATTN_SUBTILES
    k = k_ref[...]
    v = v_ref[...]

    def scores(i):
        return _dot_nt(q_ref[i * sub:(i + 1) * sub, :], k)

    s = {i: scores(i) for i in range(ATTN_LOOKAHEAD)}
    for i in range(ATTN_SUBTILES):
        si = s.pop(i)
        p = jnp.exp(_bf(si - jnp.max(si, axis=-1, keepdims=True)))
        if i + ATTN_LOOKAHEAD < ATTN_SUBTILES:
            s[i + ATTN_LOOKAHEAD] = scores(i + ATTN_LOOKAHEAD)
        o = _dot(p, v)
        o_ref[i * sub:(i + 1) * sub, :] = (o[:, :B_VDIM] / o[:, B_VDIM:]).astype(o_ref.dtype)


def _attention(q, k, v, batch, seq_len, tq):
    m = q.shape[1]
    nq = seq_len // tq
    return pl.pallas_call(
        _attn_body,
        grid=(B_HEADS, batch, nq),
        in_specs=[pl.BlockSpec((None, tq, B_QK_PAD), lambda h, bi, qi: (h, bi * nq + qi, 0)),
                  pl.BlockSpec((None, seq_len, B_QK_PAD), lambda h, bi, qi: (h, bi, 0)),
                  pl.BlockSpec((None, seq_len, 2 * B_VDIM), lambda h, bi, qi: (h, bi, 0))],
        out_specs=pl.BlockSpec((tq, B_VDIM), lambda h, bi, qi: (bi * nq + qi, h)),
        out_shape=jax.ShapeDtypeStruct((m, B_HEADS * B_VDIM), BF16),
        compiler_params=_params(("parallel", "parallel", "arbitrary")),
        name="mla_attention",
    )(q, k, v)


def _ret_body(lg_ref, q_ref, k_ref, v_ref, *rest, rev, final, chunk, hps):
    if final:
        other_ref, gate_ref, o_ref, state_ref = rest
    else:
        o_ref, state_ref = rest
    c = chunk
    heads = range(hps)
    cols = [slice(h * C_HEAD, (h + 1) * C_HEAD) for h in heads]

    @pl.when(pl.program_id(2) == 0)
    def _():
        state_ref[...] = jnp.zeros_like(state_ref)

    lg = [lg_ref[h, 0:1, 0:1] for h in heads]
    q = [q_ref[:, cols[h]] for h in heads]
    k = [k_ref[:, cols[h]] for h in heads]
    v = [v_ref[:, cols[h]] for h in heads]
    idx = lax.broadcasted_iota(jnp.int32, (c, 1), 0).astype(F32)
    if rev:
        q_pow, k_pow = c - idx, idx
    else:
        q_pow, k_pow = idx + 1.0, c - 1.0 - idx
    qd = [_bf(q[h].astype(F32) * jnp.exp(lg[h] * q_pow)) for h in heads]
    kd = [_bf(k[h].astype(F32) * jnp.exp(lg[h] * k_pow)) for h in heads]
    state = [state_ref[h] for h in heads]
    cross = [_dot(qd[h], _bf(state[h])) for h in heads]
    kv = [_dot_tn(kd[h], v[h]) for h in heads]
    for h in heads:
        state_ref[h] = state[h] * jnp.exp(lg[h] * float(c)) + kv[h]
    if final:
        ti = lax.broadcasted_iota(jnp.int32, (c, c), 0)
        si = lax.broadcasted_iota(jnp.int32, (c, c), 1)
        dist = jnp.abs(ti - si).astype(F32)
        s = [_dot_nt(q[h], k[h]) for h in heads]
        inner = [_dot(_bf(s[h] * jnp.exp(lg[h] * dist)), v[h]) for h in heads]
        for h in heads:
            o = inner[h] + cross[h] + other_ref[:, cols[h]]
            mu = jnp.mean(o, axis=-1, keepdims=True)
            d = o - mu
            var = jnp.mean(d * d, axis=-1, keepdims=True)
            o_ref[:, cols[h]] = (gate_ref[:, cols[h]].astype(F32) * (d * lax.rsqrt(var + C_GN_EPS))).astype(o_ref.dtype)
    else:
        for h in heads:
            o_ref[:, cols[h]] = cross[h]


def _retention_pass(log_gamma, qk, vg, other, batch, seq_len, rev, final, chunk=RET_CHUNK, hps=RET_HEADS_PER_STEP):
    m = qk.shape[0]
    nc = seq_len // chunk
    ng = C_HEADS // hps
    if rev:
        row = lambda bi, n: bi * nc + nc - 1 - n
    else:
        row = lambda bi, n: bi * nc + n
    blk = lambda off: pl.BlockSpec((chunk, C_HEAD * hps), lambda g, bi, n: (row(bi, n), g + off))
    in_specs = [pl.BlockSpec((hps, SUBLANES, LANES), lambda g, bi, n: (g, 0, 0)), blk(0), blk(ng), blk(0)]
    args = [log_gamma, qk, qk, vg]
    if final:
        in_specs += [blk(0), blk(ng)]
        args += [other, vg]
    return pl.pallas_call(
        functools.partial(_ret_body, rev=rev, final=final, chunk=chunk, hps=hps),
        grid=(ng, batch, nc),
        in_specs=in_specs,
        out_specs=blk(0),
        out_shape=jax.ShapeDtypeStruct((m, C_WIDTH), BF16 if final else F32),
        scratch_shapes=[pltpu.VMEM((hps, C_HEAD, C_HEAD), F32)],
        compiler_params=_params(("parallel", "parallel", "arbitrary")),
        name="retention_final" if final else "retention_cross",
    )(*args)


def _ffn_up_body(x_ref, xp_ref, xn_ref, wg_ref, wv_ref, cw_ref, cb_ref, o_ref, xext_ref, *, tiles_per_seq):
    tm = x_ref.shape[0]
    halo = xp_ref.shape[0]

    @pl.when(pl.program_id(1) == 0)
    def _():
        pos = pl.program_id(0) % tiles_per_seq
        xext_ref[0:halo, :] = jnp.where(pos == 0, jnp.zeros_like(xp_ref[...]), xp_ref[...])
        xext_ref[halo:halo + tm, :] = x_ref[...]
        xext_ref[halo + tm:, :] = jnp.where(pos == tiles_per_seq - 1, jnp.zeros_like(xn_ref[...]), xn_ref[...])

    g_ext = _dot(xext_ref[...], wg_ref[...])
    val = _dot(xext_ref[halo:halo + tm, :], wv_ref[...])
    rows = tm + 2 * halo
    g = g_ext[halo:halo + tm]
    gp = pltpu.roll(g_ext, 1, axis=0)[halo:halo + tm]
    gn = pltpu.roll(g_ext, rows - 1, axis=0)[halo:halo + tm]
    conv = cw_ref[0:1, :] * gp + cw_ref[1:2, :] * g + cw_ref[2:3, :] * gn + cb_ref[...]
    o_ref[...] = (_silu(conv) * val).astype(o_ref.dtype)


def _ffn_up_act(xn, w_up, conv_w, conv_b, seq_len, tm=MM_ROWS, tn=256):
    m, k = xn.shape
    nj = FFN_HIDDEN // tn
    halo = 2 * SUBLANES
    return pl.pallas_call(
        functools.partial(_ffn_up_body, tiles_per_seq=seq_len // tm),
        grid=(m // tm, nj),
        in_specs=[*_halo_specs(tm, k, lambda j: 0, m, halo=halo),
                  pl.BlockSpec((k, tn), lambda i, j: (0, j)),
                  pl.BlockSpec((k, tn), lambda i, j: (0, nj + j)),
                  pl.BlockSpec((3, tn), lambda i, j: (0, j)),
                  pl.BlockSpec((1, tn), lambda i, j: (0, j))],
        out_specs=pl.BlockSpec((tm, tn), lambda i, j: (i, j)),
        out_shape=jax.ShapeDtypeStruct((m, FFN_HIDDEN), BF16),
        scratch_shapes=[pltpu.VMEM((tm + 2 * halo, k), BF16)],
        compiler_params=_params(("parallel", "arbitrary")),
        name="ffn_up_conv_gate",
    )(xn, xn, xn, w_up, w_up, conv_w, conv_b.reshape(1, FFN_HIDDEN))


def _rope_tables(seq_len, dim):
    inv = ROPE_BASE ** (-jnp.arange(0, dim, 2, dtype=F32) / dim)
    ang = jnp.arange(seq_len, dtype=F32)[:, None] * inv[None, :]
    return jnp.cos(ang), jnp.sin(ang)


def _swap_halves_cols(w):
    h = w.shape[-1] // 2
    return jnp.concatenate([w[..., h:], w[..., :h]], axis=-1)


def _prep_layer(w_in, b_q_up, b_kv_up, w_branch, w_out, ffn_up, ffn_down):
    o_b = A_IN
    o_kpe = o_b + B_Q_LORA + B_KV_LORA
    o_c = o_kpe + B_ROPE
    o_g = o_c + 4 * C_WIDTH
    w_kpe = w_in[:, o_kpe:o_c]
    qh = b_q_up.reshape(B_Q_LORA, B_HEADS, B_NOPE + B_ROPE)
    q_pe = qh[..., B_NOPE:]
    w_q = jnp.concatenate([qh[..., :B_NOPE], q_pe, _swap_halves_cols(q_pe)], axis=-1)
    return dict(
        w_a=jnp.pad(w_in[:, :A_IN], ((0, 0), (0, LANES))).astype(BF16),
        w_b=w_in[:, o_b:o_kpe].astype(BF16),
        w_kpe=jnp.concatenate([w_kpe, _swap_halves_cols(w_kpe)], axis=-1).astype(BF16),
        w_cqk=w_in[:, o_c:o_c + 2 * C_WIDTH].astype(BF16),
        w_cvg=w_in[:, o_c + 2 * C_WIDTH:o_g].astype(BF16),
        w_g=w_in[:, o_g:].astype(BF16),
        w_q=jnp.transpose(w_q, (1, 0, 2)).astype(BF16),
        w_kv=jnp.transpose(b_kv_up.reshape(B_KV_LORA, B_HEADS, B_NOPE + B_VDIM), (1, 0, 2)).astype(BF16),
        w_br=w_branch.reshape(3, A_WIDTH, D_MODEL).astype(BF16),
        w_out=w_out.astype(BF16),
        ffn_up=ffn_up.astype(BF16),
        ffn_down=ffn_down.astype(BF16),
    )


def _layer(x, batch, seq_len, tabs, pw, attn_norm, a_mu, a_w0, a_w_up, a_a0, a_a_up, a_g_up, a_k_k, a_k_a, a_r_k,
           a_ln_w, a_ln_b, b_q_norm, b_kv_norm, ffn_norm, ffn_conv, ffn_conv_b):
    m = x.shape[0]
    tps = lambda tm: seq_len // tm
    xn = _rmsnorm(x, attn_norm, BF16)

    (ha,) = _matmul("in_proj_a", xn, pw["w_a"], tm=MM_ROWS, tn=512, epilogue=_ep_plain, out_dtypes=[BF16])
    r, v, kk, g, bonus, lw0, lw1, kz0, kz1, b0, b1 = _rwkv_prologue(
        ha, a_mu, a_w0, a_w_up, a_a0, a_a_up, a_g_up, a_k_k, a_k_a, a_r_k, seq_len)
    yf, yb = _rwkv_scan(r, v, kk, lw0, kz0, b0, lw1, kz1, b1, batch, seq_len)
    oa = _rwkv_epilogue(yf, yb, bonus, g, a_ln_w, a_ln_b)

    tm_b = 512
    vec = lambda n: pl.BlockSpec((1, n), lambda i, j: (0, 0))
    qn, kvn = _matmul("in_proj_b", xn, pw["w_b"], tm=tm_b, tn=B_Q_LORA + B_KV_LORA, epilogue=_ep_qkv_norms,
                      out_dtypes=[BF16, BF16], out_widths=[B_Q_LORA, B_KV_LORA],
                      extras=(b_q_norm.reshape(1, -1), b_kv_norm.reshape(1, -1)),
                      extra_specs=(vec(B_Q_LORA), vec(B_KV_LORA)))
    tm_up = 2 * MM_ROWS
    tab64 = lambda tm: pl.BlockSpec((tm, LANES), lambda i, j: (i % tps(tm), 0))
    (kpe,) = _matmul("in_proj_kpe", xn, pw["w_kpe"], tm=MM_ROWS, tn=LANES, epilogue=_ep_kpe, out_dtypes=[BF16],
                     extras=(tabs["cos64"], tabs["sin64"]), extra_specs=(tab64(MM_ROWS), tab64(MM_ROWS)))
    (q,) = _matmul("q_up", qn, pw["w_q"], tm=tm_up, tn=B_QK_PAD, epilogue=_ep_q_up, out_dtypes=[BF16],
                   extras=(tabs["cos64"], tabs["sin64"]), extra_specs=(tab64(tm_up), tab64(tm_up)),
                   b_head_major=True)
    k, vv = _matmul("kv_up", kvn, pw["w_kv"], tm=tm_up, tn=B_NOPE + B_VDIM, epilogue=_ep_kv_up,
                    out_dtypes=[BF16, BF16], out_widths=[B_QK_PAD, 2 * B_VDIM],
                    extras=(kpe,), extra_specs=(pl.BlockSpec((tm_up, LANES), lambda i, j: (i, 0)),),
                    b_head_major=True)
    ob = _attention(q, k, vv, batch, seq_len, tq=ATTN_QUERY_ROWS)

    tab128 = pl.BlockSpec((MM_ROWS, LANES), lambda i, j: (i % tps(MM_ROWS), 0))
    q_tiles = C_WIDTH // 512
    (qk,) = _matmul("in_proj_c_qk", xn, pw["w_cqk"], tm=MM_ROWS, tn=512,
                    epilogue=functools.partial(_ep_ret_qk, q_tiles=q_tiles), out_dtypes=[BF16],
                    extras=(tabs["cos256"], tabs["sin256"]), extra_specs=(tab128, tab128))
    (vg,) = _matmul("in_proj_c_vg", xn, pw["w_cvg"], tm=MM_ROWS, tn=512,
                    epilogue=functools.partial(_ep_ret_vg, v_tiles=q_tiles), out_dtypes=[BF16])
    cross_b = _retention_pass(tabs["log_gamma"], qk, vg, None, batch, seq_len, rev=True, final=False)
    oc = _retention_pass(tabs["log_gamma"], qk, vg, cross_b, batch, seq_len, rev=False, final=True)

    (gates,) = _matmul("in_proj_gates", xn, pw["w_g"], tm=MM_ROWS, tn=1024, epilogue=_ep_sigmoid, out_dtypes=[BF16])
    merged = _merge(oa, ob, oc, pw["w_br"], gates)
    res_spec = lambda tm, tn: pl.BlockSpec((tm, tn), lambda i, j: (i, j))
    (x,) = _matmul("out_proj", merged, pw["w_out"], tm=MM_ROWS, tn=512, epilogue=_ep_residual, out_dtypes=[F32],
                   extras=(x,), extra_specs=(res_spec(MM_ROWS, 512),))

    xn2 = _rmsnorm(x, ffn_norm, BF16)
    act = _ffn_up_act(xn2, pw["ffn_up"], ffn_conv, ffn_conv_b, seq_len)
    (x,) = _matmul("ffn_down", act, pw["ffn_down"], tm=512, tn=512, epilogue=_ep_residual, out_dtypes=[F32],
                   extras=(x,), extra_specs=(res_spec(512, 512),))
    return x


def _tables(seq_len):
    c64, s64 = _rope_tables(seq_len, B_ROPE)
    z = jnp.zeros((seq_len, B_ROPE), F32)
    c256, s256 = _rope_tables(seq_len, C_HEAD)
    log_gamma = jnp.log1p(-(2.0 ** (-5.0 - jnp.arange(C_HEADS, dtype=F32))))
    return dict(
        cos64=jnp.concatenate([c64, c64, z], axis=1),
        sin64=jnp.concatenate([-s64, s64, z], axis=1),
        cos256=c256, sin256=s256,
        log_gamma=jnp.broadcast_to(log_gamma[:, None, None], (C_HEADS, SUBLANES, LANES)),
    )


def _encoder(x3, prepped, per_layer, final_norm):
    batch, seq_len, d = x3.shape
    x = x3.reshape(batch * seq_len, d)
    tabs = _tables(seq_len)
    for pw, lw in zip(prepped, per_layer):
        x = _layer(x, batch, seq_len, tabs, pw, *lw)
    return _rmsnorm(x, final_norm, F32).reshape(batch, seq_len, d)


def kernel(x_prompt, x_sample, attn_norm, w_in, a_mu, a_w0, a_w_up, a_a0, a_a_up, a_g_up, a_k_k, a_k_a, a_r_k, a_ln_w, a_ln_b, b_q_norm, b_q_up, b_kv_norm, b_kv_up, w_branch, w_out, ffn_norm, ffn_up, ffn_conv, ffn_conv_b, ffn_down, final_norm):
    depth = w_in.shape[0]
    prepped = [_prep_layer(w_in[l], b_q_up[l], b_kv_up[l], w_branch[l], w_out[l], ffn_up[l], ffn_down[l])
               for l in range(depth)]
    per_layer = [(attn_norm[l], a_mu[l], a_w0[l], a_w_up[l], a_a0[l], a_a_up[l], a_g_up[l], a_k_k[l], a_k_a[l],
                  a_r_k[l], a_ln_w[l], a_ln_b[l], b_q_norm[l], b_kv_norm[l], ffn_norm[l], ffn_conv[l], ffn_conv_b[l])
                 for l in range(depth)]
    return (_encoder(x_prompt, prepped, per_layer, final_norm), _encoder(x_sample, prepped, per_layer, final_norm))
```

```python
import functools
import math

import jax
import jax.numpy as jnp
from jax import lax
from jax.experimental import pallas as pl
from jax.experimental.pallas import tpu as pltpu

F32 = jnp.float32
BF16 = jnp.bfloat16

D_MODEL = 4096
DEPTH = 2
A_HEAD = 64
A_WIDTH = 2048
A_LORA_IN = 384
A_IN = 3 * A_WIDTH + A_LORA_IN
A_GN_EPS = 64e-5
B_NOPE = 128
B_ROPE = 64
B_VDIM = 128
B_HEADS = 16
B_Q_LORA = 1536
B_KV_LORA = 512
B_QK_PAD = 256
C_HEAD = 256
C_HEADS = 8
C_WIDTH = 2048
C_GN_EPS = 1e-5
FFN_HIDDEN = 11008
ROPE_BASE = 10000.0
NORM_EPS = 1e-6

LANES = 128
SUBLANES = 8
VMEM_LIMIT_BYTES = 56 * 2**20
MM_ROWS = 1024

RWKV_CHUNK = 64
RWKV_HEADS_PER_GROUP = 2
RWKV_GROUPS_PER_STEP = 2
RWKV_CHUNKS_PER_STEP = 4
RET_CHUNK = 256
RET_HEADS_PER_STEP = 8


def _params(sem):
    return pltpu.CompilerParams(dimension_semantics=sem, vmem_limit_bytes=VMEM_LIMIT_BYTES)


def _dot(a, b):
    return jnp.dot(a, b, preferred_element_type=F32)


def _dot_nt(a, b):
    return lax.dot_general(a, b, (((1,), (1,)), ((), ())), preferred_element_type=F32)


def _dot_tn(a, b):
    return lax.dot_general(a, b, (((0,), (0,)), ((), ())), preferred_element_type=F32)


def _bf(x):
    return x.astype(BF16)


def _sigmoid(x):
    return 1.0 / (1.0 + jnp.exp(-x))


def _silu(x):
    return x * _sigmoid(x)


def _rmsnorm_body(x_ref, g_ref, o_ref):
    x = x_ref[...]
    y = x * lax.rsqrt(jnp.mean(x * x, axis=-1, keepdims=True) + NORM_EPS)
    o_ref[...] = (y * g_ref[...]).astype(o_ref.dtype)


def _rmsnorm(x, g, out_dtype, tm=256):
    m, d = x.shape
    return pl.pallas_call(
        _rmsnorm_body,
        grid=(m // tm,),
        in_specs=[pl.BlockSpec((tm, d), lambda i: (i, 0)), pl.BlockSpec((1, d), lambda i: (0, 0))],
        out_specs=pl.BlockSpec((tm, d), lambda i: (i, 0)),
        out_shape=jax.ShapeDtypeStruct((m, d), out_dtype),
        compiler_params=_params(("parallel",)),
        name="rmsnorm",
    )(x, g.reshape(1, d))


def _mm_body(a_ref, b_ref, *rest, n_extra, epilogue):
    extras = rest[:n_extra]
    outs = rest[n_extra:]
    acc = _dot(a_ref[...], b_ref[...])
    res = epilogue(acc, *extras)
    for o_ref, r in zip(outs, res):
        o_ref[...] = r.astype(o_ref.dtype)


def _matmul(name, a, b, *, tm, tn, epilogue, out_dtypes, out_widths=None, extras=(), extra_specs=(),
            b_head_major=False):
    m, k = a.shape
    if b_head_major:
        nj = b.shape[0]
        b_spec = pl.BlockSpec((None, k, tn), lambda i, j: (j, 0, 0))
    else:
        nj = b.shape[1] // tn
        b_spec = pl.BlockSpec((k, tn), lambda i, j: (0, j))
    out_widths = out_widths or [tn] * len(out_dtypes)
    if b_head_major:
        out_specs = [pl.BlockSpec((None, tm, w), lambda i, j: (j, i, 0)) for w in out_widths]
        out_shape = [jax.ShapeDtypeStruct((nj, m, w), dt) for w, dt in zip(out_widths, out_dtypes)]
    else:
        out_specs = [pl.BlockSpec((tm, w), lambda i, j: (i, j)) for w in out_widths]
        out_shape = [jax.ShapeDtypeStruct((m, nj * w), dt) for w, dt in zip(out_widths, out_dtypes)]
    return pl.pallas_call(
        functools.partial(_mm_body, n_extra=len(extras), epilogue=epilogue),
        grid=(m // tm, nj),
        in_specs=[pl.BlockSpec((tm, k), lambda i, j: (i, 0)), b_spec, *extra_specs],
        out_specs=out_specs,
        out_shape=out_shape,
        compiler_params=_params(("parallel", "arbitrary")),
        name=name,
    )(a, b, *extras)


def _ep_plain(acc):
    return (acc,)


def _ep_sigmoid(acc):
    return (_sigmoid(acc),)


def _ep_residual(acc, res_ref):
    return (acc + res_ref[...],)


def _ep_qkv_norms(acc, gq_ref, gkv_ref):
    q = acc[:, :B_Q_LORA]
    kv = acc[:, B_Q_LORA:]
    qn = q * lax.rsqrt(jnp.mean(q * q, axis=-1, keepdims=True) + NORM_EPS) * gq_ref[...]
    kvn = kv * lax.rsqrt(jnp.mean(kv * kv, axis=-1, keepdims=True) + NORM_EPS) * gkv_ref[...]
    return qn, kvn


def _rope64_paired(x, cos_ref, sin_ref):
    return x * cos_ref[...] + pltpu.roll(x, 64, axis=1) * sin_ref[...]


def _ep_kpe(acc, cos_ref, sin_ref):
    return (_rope64_paired(acc, cos_ref, sin_ref),)


def _ep_q_up(acc, cos_ref, sin_ref):
    scale = (B_NOPE + B_ROPE) ** -0.5
    return (jnp.concatenate([acc[:, :B_NOPE], _rope64_paired(acc[:, B_NOPE:], cos_ref, sin_ref)], axis=1) * scale,)


def _ep_kv_up(acc, kpe_ref):
    k = jnp.concatenate([acc[:, :B_NOPE].astype(BF16), kpe_ref[...]], axis=1)
    v = acc[:, B_NOPE:]
    return k, jnp.concatenate([v, jnp.ones_like(v)], axis=1)


def _ep_ret_qk(acc, cos_ref, sin_ref, *, q_tiles):
    cos = cos_ref[...]
    sin = sin_ref[...]
    half = C_HEAD // 2
    pieces = []
    for h in range(acc.shape[1] // C_HEAD):
        x1 = acc[:, h * C_HEAD:h * C_HEAD + half]
        x2 = acc[:, h * C_HEAD + half:(h + 1) * C_HEAD]
        pieces += [x1 * cos - x2 * sin, x1 * sin + x2 * cos]
    out = jnp.concatenate(pieces, axis=1)
    scale = jnp.where(pl.program_id(1) < q_tiles, C_HEAD ** -0.5, 1.0)
    return (out * scale,)


def _ep_ret_vg(acc, *, v_tiles):
    return (jnp.where(pl.program_id(1) < v_tiles, acc, _silu(acc)),)


def _merge_body(oa_ref, ob_ref, oc_ref, w_ref, ga_ref, gb_ref, gc_ref, out_ref):
    terms = [g_ref[...].astype(F32) * _dot(o_ref[...], w_ref[br])
             for br, (o_ref, g_ref) in enumerate(((oa_ref, ga_ref), (ob_ref, gb_ref), (oc_ref, gc_ref)))]
    out_ref[...] = (terms[0] + terms[1] + terms[2]).astype(out_ref.dtype)


def _merge(oa, ob, oc, w3, gates, tm=MM_ROWS, tn=512):
    m, kb = oa.shape
    n = w3.shape[2]
    nj = n // tn
    o_spec = pl.BlockSpec((tm, kb), lambda i, j: (i, 0))
    gate_specs = [pl.BlockSpec((tm, tn), lambda i, j, br=br: (i, br * nj + j)) for br in range(3)]
    return pl.pallas_call(
        _merge_body,
        grid=(m // tm, nj),
        in_specs=[o_spec, o_spec, o_spec, pl.BlockSpec((3, kb, tn), lambda i, j: (0, 0, j)), *gate_specs],
        out_specs=pl.BlockSpec((tm, tn), lambda i, j: (i, j)),
        out_shape=jax.ShapeDtypeStruct((m, n), BF16),
        compiler_params=_params(("parallel", "arbitrary")),
        name="branch_merge",
    )(oa, ob, oc, w3, gates, gates, gates)


def _halo_specs(tm, w, col_block, n_rows, halo=SUBLANES):
    per_tile = tm // halo
    last = n_rows // halo - 1
    main = pl.BlockSpec((tm, w), lambda i, j: (i, col_block(j)))
    prev = pl.BlockSpec((halo, w), lambda i, j: (jnp.maximum(i * per_tile - 1, 0), col_block(j)))
    nxt = pl.BlockSpec((halo, w), lambda i, j: (jnp.minimum((i + 1) * per_tile, last), col_block(j)))
    return [main, prev, nxt]


def _shifted(x, prev_ref, next_ref, tiles_per_seq):
    tm = x.shape[0]
    i = pl.program_id(0)
    pos = i % tiles_per_seq
    halo = prev_ref.shape[0]
    prev_row = jnp.where(pos == 0, 0.0, prev_ref[halo - 1:halo, :].astype(F32))
    next_row = jnp.where(pos == tiles_per_seq - 1, 0.0, next_ref[0:1, :].astype(F32))
    row = lax.broadcasted_iota(jnp.int32, x.shape, 0)
    xp = jnp.where(row == 0, prev_row, pltpu.roll(x, 1, axis=0))
    xn = jnp.where(row == tm - 1, next_row, pltpu.roll(x, tm - 1, axis=0))
    return xp, xn


def _head_sum64(x):
    w = x.shape[1]
    r = lax.broadcasted_iota(jnp.int32, (w, w), 0) >> 6
    c = lax.broadcasted_iota(jnp.int32, (w, w), 1) >> 6
    ones = jnp.where(r == c, 1.0, 0.0).astype(BF16)
    return _dot(_bf(x), ones)


def _rwkv_pre_body(r_ref, rp_ref, rn_ref, k_ref, kp_ref, kn_ref, v_ref, vp_ref, vn_ref, l_ref, lp_ref, ln_ref,
                   mur_ref, muk_ref, muv_ref, mul_ref, w0_ref, a0_ref, wup_ref, aup_ref, gup_ref,
                   kk_w_ref, ka_ref, rk_ref,
                   r_out, v_out, kk_out, g_out, bonus_out, lw0_out, lw1_out, kz0_out, kz1_out, b0_out, b1_out,
                   *, tiles_per_seq):
    def mix(x_ref, p_ref, n_ref, mu_ref):
        x = x_ref[...].astype(F32)
        xp, xn = _shifted(x, p_ref, n_ref, tiles_per_seq)
        return x + mu_ref[0:1, :] * (xp - x) + mu_ref[1:2, :] * (xn - x)

    r = mix(r_ref, rp_ref, rn_ref, mur_ref)
    k = mix(k_ref, kp_ref, kn_ref, muk_ref)
    v = mix(v_ref, vp_ref, vn_ref, muv_ref)
    lo = mix(l_ref, lp_ref, ln_ref, mul_ref)
    wd = jnp.tanh(lo[:, 0:LANES])
    ad = lo[:, LANES:2 * LANES]
    gd = _sigmoid(lo[:, 2 * LANES:3 * LANES])
    lane = lax.broadcasted_iota(jnp.int32, (1, LANES), 1)

    g_out[...] = _dot(_bf(gd), gup_ref[...]).astype(g_out.dtype)
    kk = k * kk_w_ref[...]
    kk = kk * lax.rsqrt(jnp.maximum(_head_sum64(kk * kk), 1e-24))
    r_out[...] = r.astype(r_out.dtype)
    v_out[...] = v.astype(v_out.dtype)
    kk_out[...] = kk.astype(kk_out.dtype)
    bonus_out[...] = (_head_sum64(r * k * rk_ref[...]) * v).astype(bonus_out.dtype)
    for z, (lw_out, kz_out, b_out) in enumerate(((lw0_out, kz0_out, b0_out), (lw1_out, kz1_out, b1_out))):
        sel = (lane >> 6) == z
        w_pre = w0_ref[z:z + 1, :] + _dot(_bf(jnp.where(sel, wd, 0.0)), wup_ref[...])
        lw_out[...] = -math.exp(-0.5) * _sigmoid(w_pre)
        a = _sigmoid(a0_ref[z:z + 1, :] + _dot(_bf(jnp.where(sel, ad, 0.0)), aup_ref[...]))
        kz_out[...] = (k * (1.0 + (a - 1.0) * ka_ref[...])).astype(kz_out.dtype)
        b_out[...] = (kk * a).astype(b_out.dtype)


def _rwkv_prologue(ha, a_mu, a_w0, a_w_up, a_a0, a_a_up, a_g_up, a_k_k, a_k_a, a_r_k, seq_len, tm=256, tw=512):
    m = ha.shape[0]
    ncb = A_WIDTH // tw
    lora_block = 3 * A_WIDTH // A_LORA_IN
    halo = 2 * SUBLANES
    in_specs = []
    for s in range(3):
        in_specs += _halo_specs(tm, tw, lambda j, s=s: s * ncb + j, m, halo=halo)
    in_specs += _halo_specs(tm, A_LORA_IN, lambda j: lora_block, m, halo=halo)
    for s in range(3):
        in_specs.append(pl.BlockSpec((2, tw), lambda i, j, s=s: (0, s * ncb + j)))
    in_specs.append(pl.BlockSpec((2, A_LORA_IN), lambda i, j: (0, lora_block)))
    vec2 = pl.BlockSpec((2, tw), lambda i, j: (0, j))
    up = pl.BlockSpec((LANES, tw), lambda i, j: (0, j))
    vec1 = pl.BlockSpec((1, tw), lambda i, j: (0, j))
    in_specs += [vec2, vec2, up, up, up, vec1, vec1, vec1]
    out_spec = pl.BlockSpec((tm, tw), lambda i, j: (i, j))
    out_dtypes = [BF16, BF16, BF16, BF16, BF16, F32, F32, BF16, BF16, BF16, BF16]
    return pl.pallas_call(
        functools.partial(_rwkv_pre_body, tiles_per_seq=seq_len // tm),
        grid=(m // tm, ncb),
        in_specs=in_specs,
        out_specs=[out_spec] * 11,
        out_shape=[jax.ShapeDtypeStruct((m, A_WIDTH), dt) for dt in out_dtypes],
        compiler_params=_params(("parallel", "arbitrary")),
        name="rwkv_prologue",
    )(ha, ha, ha, ha, ha, ha, ha, ha, ha, ha, ha, ha, a_mu, a_mu, a_mu, a_mu, a_w0, a_a0,
      a_w_up.reshape(LANES, A_WIDTH).astype(BF16), a_a_up.reshape(LANES, A_WIDTH).astype(BF16), a_g_up.astype(BF16),
      a_k_k.reshape(1, A_WIDTH), a_k_a.reshape(1, A_WIDTH), a_r_k.reshape(1, A_WIDTH))


def _rwkv_chunk_terms(insts, *, hp):
    c, w = insts[0][1].shape
    rows = c * hp
    n = len(insts)
    revs = [inst[0] for inst in insts]
    ti = lax.broadcasted_iota(jnp.int32, (c, c), 0)
    si = lax.broadcasted_iota(jnp.int32, (c, c), 1)
    tri = {False: jnp.where(si <= ti, 1.0, 0.0).astype(BF16), True: jnp.where(si >= ti, 1.0, 0.0).astype(BF16)}
    rt = lax.broadcasted_iota(jnp.int32, (rows, rows), 0)
    cs = lax.broadcasted_iota(jnp.int32, (rows, rows), 1)
    eye = jnp.where(rt == cs, 1.0, 0.0)
    rt = rt & (c - 1)
    cs = cs & (c - 1)
    strict = {False: cs < rt, True: cs > rt}
    incl = {False: cs <= rt, True: cs >= rt}
    lane_head = lax.broadcasted_iota(jnp.int32, (c, w), 1) >> 6
    ew = lax.broadcasted_iota(jnp.int32, (w, w), 0) == lax.broadcasted_iota(jnp.int32, (w, w), 1)

    def stack_f32(x):
        return jnp.concatenate([jnp.where(lane_head == h, x, 0.0) for h in range(hp)], axis=0)

    def stack(x):
        return _bf(stack_f32(x))

    cums = []
    for rev, _, _, _, lw, _, _ in insts:
        hi = lw.astype(BF16)
        mid = (lw - hi.astype(F32)).astype(BF16)
        lo = (lw - hi.astype(F32) - mid.astype(F32)).astype(BF16)
        parts = _dot(tri[rev], jnp.concatenate([hi, mid, lo], axis=1))
        cums.append(parts[:, :w] + parts[:, w:2 * w] + parts[:, 2 * w:])
    tots = [cum[0:1, :] if rev else cum[c - 1:c, :] for rev, cum in zip(revs, cums)]

    xs, rs_f, bs, ks, bh, kh, vs = [], [], [], [], [], [], []
    for (rev, r, v, kk, lw, kz, b), cum, tot in zip(insts, cums, tots):
        r, kk, kz, b = (x.astype(F32) for x in (r, kk, kz, b))
        e_neg = jnp.exp(-cum)
        e_rest = jnp.exp(tot - cum)
        xs.append(stack(kk * jnp.exp(cum - lw)))
        rs_f.append(stack_f32(r * jnp.exp(cum)))
        bs.append(stack(b * e_neg))
        ks.append(stack(kz * e_neg))
        bh.append(stack(b * e_rest))
        kh.append(stack(kz * e_rest))
        vs.append(stack(v))

    grams = [_dot_nt(jnp.concatenate([xs[i], _bf(rs_f[i])], axis=0), jnp.concatenate([bs[i], ks[i]], axis=0))
             for i in range(n)]
    l_b = [jnp.where(strict[revs[i]], grams[i][:rows, :rows], 0.0) for i in range(n)]
    l_k = [_bf(jnp.where(strict[revs[i]], grams[i][:rows, rows:], 0.0)) for i in range(n)]
    incl2 = {rev: jnp.concatenate([m, m], axis=1) for rev, m in incl.items()}
    m_kb = [_bf(jnp.where(incl2[revs[i]], grams[i][rows:, :], 0.0)) for i in range(n)]
    lk_v = [_dot(l_k[i], vs[i]) for i in range(n)]

    t_inv = [eye - l_b[i] for i in range(n)]
    lb = [_bf(x) for x in l_b]
    l_pow = [_dot(x, x) for x in lb]
    for _ in range(int(math.log2(c)) - 2):
        lp = [_bf(x) for x in l_pow]
        both = [_dot(lp[i], jnp.concatenate([_bf(t_inv[i]), lp[i]], axis=1)) for i in range(n)]
        t_inv = [t_inv[i] + both[i][:, :rows] for i in range(n)]
        l_pow = [both[i][:, rows:] for i in range(n)]
    t_inv = [t_inv[i] + _dot(_bf(l_pow[i]), _bf(t_inv[i])) for i in range(n)]

    gh = [_bf(_dot(_bf(t_inv[i]), jnp.concatenate([xs[i], _bf(lk_v[i])], axis=1))) for i in range(n)]
    zero = jnp.zeros((rows, w), BF16)
    lhs = [jnp.concatenate([m_kb[i], jnp.concatenate([bh[i].T, kh[i].T], axis=1)], axis=0) for i in range(n)]
    rhs = [jnp.concatenate([-gh[i], jnp.concatenate([zero, vs[i]], axis=1)], axis=0) for i in range(n)]
    terms = [_dot(lhs[i], rhs[i]) for i in range(n)]
    out = []
    for i in range(n):
        q_p = rs_f[i] + terms[i][:rows, :w]
        y0 = terms[i][:rows, w:]
        phi = jnp.where(ew, jnp.exp(tots[i]), 0.0) + terms[i][rows:, :w]
        psi = terms[i][rows:, w:]
        out.append((_bf(q_p), y0, _bf(phi), psi))
    return out


def _rwkv_scan_body(rf_ref, vf_ref, kkf_ref, lwf_ref, kzf_ref, bf_ref, rb_ref, vb_ref, kkb_ref, lwb_ref, kzb_ref,
                    bb_ref, yf_ref, yb_ref, state_ref, *, hp, gp, cps, chunk):
    c = chunk
    w = A_HEAD * hp

    @pl.when(pl.program_id(2) == 0)
    def _():
        state_ref[...] = jnp.zeros_like(state_ref)

    dirs = ((False, (rf_ref, vf_ref, kkf_ref, lwf_ref, kzf_ref, bf_ref), yf_ref),
            (True, (rb_ref, vb_ref, kkb_ref, lwb_ref, kzb_ref, bb_ref), yb_ref))
    keys = [(d, gi, ci) for d in range(2) for gi in range(gp) for ci in range(cps)]
    insts = [(dirs[d][0], *[x[ci * c:(ci + 1) * c, gi * w:(gi + 1) * w] for x in dirs[d][1]]) for d, gi, ci in keys]
    terms = dict(zip(keys, _rwkv_chunk_terms(insts, hp=hp)))
    chains = [(d, gi) for d in range(2) for gi in range(gp)]
    states = {k: state_ref[k[0], k[1]] for k in chains}
    for step in range(cps):
        for d, gi in chains:
            ci = cps - 1 - step if dirs[d][0] else step
            q_p, y0, phi, psi = terms[(d, gi, ci)]
            a_b = _bf(states[(d, gi)])
            y_st = _dot(q_p, a_b) + y0
            states[(d, gi)] = _dot(phi, a_b) + psi
            y = y_st[0:c, :]
            for h in range(1, hp):
                y = y + y_st[h * c:(h + 1) * c, :]
            dirs[d][2][ci * c:(ci + 1) * c, gi * w:(gi + 1) * w] = y.astype(dirs[d][2].dtype)
    for d, gi in chains:
        state_ref[d, gi] = states[(d, gi)]


def _rwkv_scan(r, v, kk, lw0, kz0, b0, lw1, kz1, b1, batch, seq_len, hp=RWKV_HEADS_PER_GROUP,
               gp=RWKV_GROUPS_PER_STEP, cps=RWKV_CHUNKS_PER_STEP, chunk=RWKV_CHUNK):
    m = r.shape[0]
    w = A_HEAD * hp
    rows = chunk * cps
    ns = seq_len // rows
    fwd = pl.BlockSpec((rows, w * gp), lambda bi, g, n: (bi * ns + n, g))
    bwd = pl.BlockSpec((rows, w * gp), lambda bi, g, n: (bi * ns + ns - 1 - n, g))
    out = jax.ShapeDtypeStruct((m, A_WIDTH), BF16)
    return pl.pallas_call(
        functools.partial(_rwkv_scan_body, hp=hp, gp=gp, cps=cps, chunk=chunk),
        grid=(batch, A_WIDTH // (w * gp), ns),
        in_specs=[fwd] * 6 + [bwd] * 6,
        out_specs=[fwd, bwd],
        out_shape=[out, out],
        scratch_shapes=[pltpu.VMEM((2, gp, w, w), F32)],
        compiler_params=_params(("parallel", "parallel", "arbitrary")),
        name="rwkv_scan",
    )(r, v, kk, lw0, kz0, b0, r, v, kk, lw1, kz1, b1)


def _rwkv_post_body(yf_ref, yb_ref, bonus_ref, g_ref, lnw_ref, lnb_ref, o_ref):
    y = yf_ref[...].astype(F32) + yb_ref[...].astype(F32)
    inv_n = 1.0 / A_HEAD
    mu = _head_sum64(y) * inv_n
    d = y - mu
    var = _head_sum64(d * d) * inv_n
    yn = d * lax.rsqrt(var + A_GN_EPS) * lnw_ref[...] + lnb_ref[...]
    o_ref[...] = ((yn + bonus_ref[...].astype(F32)) * g_ref[...].astype(F32)).astype(o_ref.dtype)


def _rwkv_epilogue(yf, yb, bonus, g, ln_w, ln_b, tm=256, tw=512):
    m = yf.shape[0]
    spec = pl.BlockSpec((tm, tw), lambda i, j: (i, j))
    vec = pl.BlockSpec((1, tw), lambda i, j: (0, j))
    return pl.pallas_call(
        _rwkv_post_body,
        grid=(m // tm, A_WIDTH // tw),
        in_specs=[spec, spec, spec, spec, vec, vec],
        out_specs=spec,
        out_shape=jax.ShapeDtypeStruct((m, A_WIDTH), BF16),
        compiler_params=_params(("parallel", "parallel")),
        name="rwkv_epilogue",
    )(yf, yb, bonus, g, ln_w.reshape(1, A_WIDTH), ln_b.reshape(1, A_WIDTH))


ATTN_QUERY_ROWS = 1024
ATTN_SUBTILES = 4
ATTN_LOOKAHEAD = 2


def _attn_body(q_ref, k_ref, v_ref, o_ref):
    sub = q_ref.shape[0] // ATTN_SUBTILES
    k = k_ref[...]
    v = v_ref[...]

    def scores(i):
        return _dot_nt(q_ref[i * sub:(i + 1) * sub, :], k)

    s = {i: scores(i) for i in range(ATTN_LOOKAHEAD)}
    for i in range(ATTN_SUBTILES):
        si = s.pop(i)
        p = jnp.exp(_bf(si - jnp.max(si, axis=-1, keepdims=True)))
        if i + ATTN_LOOKAHEAD < ATTN_SUBTILES:
            s[i + ATTN_LOOKAHEAD] = scores(i + ATTN_LOOKAHEAD)
        o = _dot(p, v)
        o_ref[i * sub:(i + 1) * sub, :] = (o[:, :B_VDIM] / o[:, B_VDIM:]).astype(o_ref.dtype)


def _attention(q, k, v, batch, seq_len, tq):
    m = q.shape[1]
    nq = seq_len // tq
    return pl.pallas_call(
        _attn_body,
        grid=(B_HEADS, batch, nq),
        in_specs=[pl.BlockSpec((None, tq, B_QK_PAD), lambda h, bi, qi: (h, bi * nq + qi, 0)),
                  pl.BlockSpec((None, seq_len, B_QK_PAD), lambda h, bi, qi: (h, bi, 0)),
                  pl.BlockSpec((None, seq_len, 2 * B_VDIM), lambda h, bi, qi: (h, bi, 0))],
        out_specs=pl.BlockSpec((tq, B_VDIM), lambda h, bi, qi: (bi * nq + qi, h)),
        out_shape=jax.ShapeDtypeStruct((m, B_HEADS * B_VDIM), BF16),
        compiler_params=_params(("parallel", "parallel", "arbitrary")),
        name="mla_attention",
    )(q, k, v)


def _ret_body(lg_ref, q_ref, k_ref, v_ref, *rest, rev, final, chunk, hps):
    if final:
        other_ref, gate_ref, o_ref, state_ref = rest
    else:
        o_ref, state_ref = rest
    c = chunk
    heads = range(hps)
    cols = [slice(h * C_HEAD, (h + 1) * C_HEAD) for h in heads]

    @pl.when(pl.program_id(2) == 0)
    def _():
        state_ref[...] = jnp.zeros_like(state_ref)

    lg = [lg_ref[h, 0:1, 0:1] for h in heads]
    q = [q_ref[:, cols[h]] for h in heads]
    k = [k_ref[:, cols[h]] for h in heads]
    v = [v_ref[:, cols[h]] for h in heads]
    idx = lax.broadcasted_iota(jnp.int32, (c, 1), 0).astype(F32)
    if rev:
        q_pow, k_pow = c - idx, idx
    else:
        q_pow, k_pow = idx + 1.0, c - 1.0 - idx
    qd = [_bf(q[h].astype(F32) * jnp.exp(lg[h] * q_pow)) for h in heads]
    kd = [_bf(k[h].astype(F32) * jnp.exp(lg[h] * k_pow)) for h in heads]
    state = [state_ref[h] for h in heads]
    cross = [_dot(qd[h], _bf(state[h])) for h in heads]
    kv = [_dot_tn(kd[h], v[h]) for h in heads]
    for h in heads:
        state_ref[h] = state[h] * jnp.exp(lg[h] * float(c)) + kv[h]
    if final:
        ti = lax.broadcasted_iota(jnp.int32, (c, c), 0)
        si = lax.broadcasted_iota(jnp.int32, (c, c), 1)
        dist = jnp.abs(ti - si).astype(F32)
        s = [_dot_nt(q[h], k[h]) for h in heads]
        inner = [_dot(_bf(s[h] * jnp.exp(lg[h] * dist)), v[h]) for h in heads]
        for h in heads:
            o = inner[h] + cross[h] + other_ref[:, cols[h]]
            mu = jnp.mean(o, axis=-1, keepdims=True)
            d = o - mu
            var = jnp.mean(d * d, axis=-1, keepdims=True)
            o_ref[:, cols[h]] = (gate_ref[:, cols[h]].astype(F32) * (d * lax.rsqrt(var + C_GN_EPS))).astype(o_ref.dtype)
    else:
        for h in heads:
            o_ref[:, cols[h]] = cross[h]


def _retention_pass(log_gamma, qk, vg, other, batch, seq_len, rev, final, chunk=RET_CHUNK, hps=RET_HEADS_PER_STEP):
    m = qk.shape[0]
    nc = seq_len // chunk
    ng = C_HEADS // hps
    if rev:
        row = lambda bi, n: bi * nc + nc - 1 - n
    else:
        row = lambda bi, n: bi * nc + n
    blk = lambda off: pl.BlockSpec((chunk, C_HEAD * hps), lambda g, bi, n: (row(bi, n), g + off))
    in_specs = [pl.BlockSpec((hps, SUBLANES, LANES), lambda g, bi, n: (g, 0, 0)), blk(0), blk(ng), blk(0)]
    args = [log_gamma, qk, qk, vg]
    if final:
        in_specs += [blk(0), blk(ng)]
        args += [other, vg]
    return pl.pallas_call(
        functools.partial(_ret_body, rev=rev, final=final, chunk=chunk, hps=hps),
        grid=(ng, batch, nc),
        in_specs=in_specs,
        out_specs=blk(0),
        out_shape=jax.ShapeDtypeStruct((m, C_WIDTH), BF16 if final else F32),
        scratch_shapes=[pltpu.VMEM((hps, C_HEAD, C_HEAD), F32)],
        compiler_params=_params(("parallel", "parallel", "arbitrary")),
        name="retention_final" if final else "retention_cross",
    )(*args)


def _ffn_up_body(x_ref, xp_ref, xn_ref, wg_ref, wv_ref, cw_ref, cb_ref, o_ref, xext_ref, *, tiles_per_seq):
    tm = x_ref.shape[0]
    halo = xp_ref.shape[0]

    @pl.when(pl.program_id(1) == 0)
    def _():
        pos = pl.program_id(0) % tiles_per_seq
        xext_ref[0:halo, :] = jnp.where(pos == 0, jnp.zeros_like(xp_ref[...]), xp_ref[...])
        xext_ref[halo:halo + tm, :] = x_ref[...]
        xext_ref[halo + tm:, :] = jnp.where(pos == tiles_per_seq - 1, jnp.zeros_like(xn_ref[...]), xn_ref[...])

    g_ext = _dot(xext_ref[...], wg_ref[...])
    val = _dot(xext_ref[halo:halo + tm, :], wv_ref[...])
    rows = tm + 2 * halo
    g = g_ext[halo:halo + tm]
    gp = pltpu.roll(g_ext, 1, axis=0)[halo:halo + tm]
    gn = pltpu.roll(g_ext, rows - 1, axis=0)[halo:halo + tm]
    conv = cw_ref[0:1, :] * gp + cw_ref[1:2, :] * g + cw_ref[2:3, :] * gn + cb_ref[...]
    o_ref[...] = (_silu(conv) * val).astype(o_ref.dtype)


def _ffn_up_act(xn, w_up, conv_w, conv_b, seq_len, tm=MM_ROWS, tn=256):
    m, k = xn.shape
    nj = FFN_HIDDEN // tn
    halo = 2 * SUBLANES
    return pl.pallas_call(
        functools.partial(_ffn_up_body, tiles_per_seq=seq_len // tm),
        grid=(m // tm, nj),
        in_specs=[*_halo_specs(tm, k, lambda j: 0, m, halo=halo),
                  pl.BlockSpec((k, tn), lambda i, j: (0, j)),
                  pl.BlockSpec((k, tn), lambda i, j: (0, nj + j)),
                  pl.BlockSpec((3, tn), lambda i, j: (0, j)),
                  pl.BlockSpec((1, tn), lambda i, j: (0, j))],
        out_specs=pl.BlockSpec((tm, tn), lambda i, j: (i, j)),
        out_shape=jax.ShapeDtypeStruct((m, FFN_HIDDEN), BF16),
        scratch_shapes=[pltpu.VMEM((tm + 2 * halo, k), BF16)],
        compiler_params=_params(("parallel", "arbitrary")),
        name="ffn_up_conv_gate",
    )(xn, xn, xn, w_up, w_up, conv_w, conv_b.reshape(1, FFN_HIDDEN))


def _rope_tables(seq_len, dim):
    inv = ROPE_BASE ** (-jnp.arange(0, dim, 2, dtype=F32) / dim)
    ang = jnp.arange(seq_len, dtype=F32)[:, None] * inv[None, :]
    return jnp.cos(ang), jnp.sin(ang)


def _swap_halves_cols(w):
    h = w.shape[-1] // 2
    return jnp.concatenate([w[..., h:], w[..., :h]], axis=-1)


def _prep_layer(w_in, b_q_up, b_kv_up, w_branch, w_out, ffn_up, ffn_down):
    o_b = A_IN
    o_kpe = o_b + B_Q_LORA + B_KV_LORA
    o_c = o_kpe + B_ROPE
    o_g = o_c + 4 * C_WIDTH
    w_kpe = w_in[:, o_kpe:o_c]
    qh = b_q_up.reshape(B_Q_LORA, B_HEADS, B_NOPE + B_ROPE)
    q_pe = qh[..., B_NOPE:]
    w_q = jnp.concatenate([qh[..., :B_NOPE], q_pe, _swap_halves_cols(q_pe)], axis=-1)
    return dict(
        w_a=jnp.pad(w_in[:, :A_IN], ((0, 0), (0, LANES))).astype(BF16),
        w_b=w_in[:, o_b:o_kpe].astype(BF16),
        w_kpe=jnp.concatenate([w_kpe, _swap_halves_cols(w_kpe)], axis=-1).astype(BF16),
        w_cqk=w_in[:, o_c:o_c + 2 * C_WIDTH].astype(BF16),
        w_cvg=w_in[:, o_c + 2 * C_WIDTH:o_g].astype(BF16),
        w_g=w_in[:, o_g:].astype(BF16),
        w_q=jnp.transpose(w_q, (1, 0, 2)).astype(BF16),
        w_kv=jnp.transpose(b_kv_up.reshape(B_KV_LORA, B_HEADS, B_NOPE + B_VDIM), (1, 0, 2)).astype(BF16),
        w_br=w_branch.reshape(3, A_WIDTH, D_MODEL).astype(BF16),
        w_out=w_out.astype(BF16),
        ffn_up=ffn_up.astype(BF16),
        ffn_down=ffn_down.astype(BF16),
    )


def _layer(x, batch, seq_len, tabs, pw, attn_norm, a_mu, a_w0, a_w_up, a_a0, a_a_up, a_g_up, a_k_k, a_k_a, a_r_k,
           a_ln_w, a_ln_b, b_q_norm, b_kv_norm, ffn_norm, ffn_conv, ffn_conv_b):
    m = x.shape[0]
    tps = lambda tm: seq_len // tm
    xn = _rmsnorm(x, attn_norm, BF16)

    (ha,) = _matmul("in_proj_a", xn, pw["w_a"], tm=MM_ROWS, tn=512, epilogue=_ep_plain, out_dtypes=[BF16])
    r, v, kk, g, bonus, lw0, lw1, kz0, kz1, b0, b1 = _rwkv_prologue(
        ha, a_mu, a_w0, a_w_up, a_a0, a_a_up, a_g_up, a_k_k, a_k_a, a_r_k, seq_len)
    yf, yb = _rwkv_scan(r, v, kk, lw0, kz0, b0, lw1, kz1, b1, batch, seq_len)
    oa = _rwkv_epilogue(yf, yb, bonus, g, a_ln_w, a_ln_b)

    tm_b = 512
    vec = lambda n: pl.BlockSpec((1, n), lambda i, j: (0, 0))
    qn, kvn = _matmul("in_proj_b", xn, pw["w_b"], tm=tm_b, tn=B_Q_LORA + B_KV_LORA, epilogue=_ep_qkv_norms,
                      out_dtypes=[BF16, BF16], out_widths=[B_Q_LORA, B_KV_LORA],
                      extras=(b_q_norm.reshape(1, -1), b_kv_norm.reshape(1, -1)),
                      extra_specs=(vec(B_Q_LORA), vec(B_KV_LORA)))
    tm_up = 2 * MM_ROWS
    tab64 = lambda tm: pl.BlockSpec((tm, LANES), lambda i, j: (i % tps(tm), 0))
    (kpe,) = _matmul("in_proj_kpe", xn, pw["w_kpe"], tm=MM_ROWS, tn=LANES, epilogue=_ep_kpe, out_dtypes=[BF16],
                     extras=(tabs["cos64"], tabs["sin64"]), extra_specs=(tab64(MM_ROWS), tab64(MM_ROWS)))
    (q,) = _matmul("q_up", qn, pw["w_q"], tm=tm_up, tn=B_QK_PAD, epilogue=_ep_q_up, out_dtypes=[BF16],
                   extras=(tabs["cos64"], tabs["sin64"]), extra_specs=(tab64(tm_up), tab64(tm_up)),
                   b_head_major=True)
    k, vv = _matmul("kv_up", kvn, pw["w_kv"], tm=tm_up, tn=B_NOPE + B_VDIM, epilogue=_ep_kv_up,
                    out_dtypes=[BF16, BF16], out_widths=[B_QK_PAD, 2 * B_VDIM],
                    extras=(kpe,), extra_specs=(pl.BlockSpec((tm_up, LANES), lambda i, j: (i, 0)),),
                    b_head_major=True)
    ob = _attention(q, k, vv, batch, seq_len, tq=ATTN_QUERY_ROWS)

    tab128 = pl.BlockSpec((MM_ROWS, LANES), lambda i, j: (i % tps(MM_ROWS), 0))
    q_tiles = C_WIDTH // 1024
    (qk,) = _matmul("in_proj_c_qk", xn, pw["w_cqk"], tm=MM_ROWS, tn=1024,
                    epilogue=functools.partial(_ep_ret_qk, q_tiles=q_tiles), out_dtypes=[BF16],
                    extras=(tabs["cos256"], tabs["sin256"]), extra_specs=(tab128, tab128))
    (vg,) = _matmul("in_proj_c_vg", xn, pw["w_cvg"], tm=MM_ROWS, tn=1024,
                    epilogue=functools.partial(_ep_ret_vg, v_tiles=q_tiles), out_dtypes=[BF16])
    cross_b = _retention_pass(tabs["log_gamma"], qk, vg, None, batch, seq_len, rev=True, final=False)
    oc = _retention_pass(tabs["log_gamma"], qk, vg, cross_b, batch, seq_len, rev=False, final=True)

    (gates,) = _matmul("in_proj_gates", xn, pw["w_g"], tm=MM_ROWS, tn=1024, epilogue=_ep_sigmoid, out_dtypes=[BF16])
    merged = _merge(oa, ob, oc, pw["w_br"], gates)
    res_spec = lambda tm, tn: pl.BlockSpec((tm, tn), lambda i, j: (i, j))
    (x,) = _matmul("out_proj", merged, pw["w_out"], tm=MM_ROWS, tn=512, epilogue=_ep_residual, out_dtypes=[F32],
                   extras=(x,), extra_specs=(res_spec(MM_ROWS, 512),))

    xn2 = _rmsnorm(x, ffn_norm, BF16)
    act = _ffn_up_act(xn2, pw["ffn_up"], ffn_conv, ffn_conv_b, seq_len)
    (x,) = _matmul("ffn_down", act, pw["ffn_down"], tm=512, tn=512, epilogue=_ep_residual, out_dtypes=[F32],
                   extras=(x,), extra_specs=(res_spec(512, 512),))
    return x


def _tables(seq_len):
    c64, s64 = _rope_tables(seq_len, B_ROPE)
    z = jnp.zeros((seq_len, B_ROPE), F32)
    c256, s256 = _rope_tables(seq_len, C_HEAD)
    log_gamma = jnp.log1p(-(2.0 ** (-5.0 - jnp.arange(C_HEADS, dtype=F32))))
    return dict(
        cos64=jnp.concatenate([c64, c64, z], axis=1),
        sin64=jnp.concatenate([-s64, s64, z], axis=1),
        cos256=c256, sin256=s256,
        log_gamma=jnp.broadcast_to(log_gamma[:, None, None], (C_HEADS, SUBLANES, LANES)),
    )


def _encoder(x3, prepped, per_layer, final_norm):
    batch, seq_len, d = x3.shape
    x = x3.reshape(batch * seq_len, d)
    tabs = _tables(seq_len)
    for pw, lw in zip(prepped, per_layer):
        x = _layer(x, batch, seq_len, tabs, pw, *lw)
    return _rmsnorm(x, final_norm, F32).reshape(batch, seq_len, d)


def kernel(x_prompt, x_sample, attn_norm, w_in, a_mu, a_w0, a_w_up, a_a0, a_a_up, a_g_up, a_k_k, a_k_a, a_r_k, a_ln_w, a_ln_b, b_q_norm, b_q_up, b_kv_norm, b_kv_up, w_branch, w_out, ffn_norm, ffn_up, ffn_conv, ffn_conv_b, ffn_down, final_norm):
    depth = w_in.shape[0]
    prepped = [_prep_layer(w_in[l], b_q_up[l], b_kv_up[l], w_branch[l], w_out[l], ffn_up[l], ffn_down[l])
               for l in range(depth)]
    per_layer = [(attn_norm[l], a_mu[l], a_w0[l], a_w_up[l], a_a0[l], a_a_up[l], a_g_up[l], a_k_k[l], a_k_a[l],
                  a_r_k[l], a_ln_w[l], a_ln_b[l], b_q_norm[l], b_kv_norm[l], ffn_norm[l], ffn_conv[l], ffn_conv_b[l])
                 for l in range(depth)]
    return (_encoder(x_prompt, prepped, per_layer, final_norm), _encoder(x_sample, prepped, per_layer, final_norm))
```
